```python
import math
import jax, jax.numpy as jnp
from jax import lax
import numpy as np

D_MODEL = 1024
BATCH = 2
SEQ = 8192
DEPTH = 2

D_MIX = D_MODEL
LRU_WIDTH = D_MIX // 4
LRU_BLOCKS = 4
LRU_CONV = 4
LRU_C = 8.0
HG_HEADS = 4
HG_DK = 64
HG_DV = 64
HG_CHUNK = 64
ATT_HEADS = 8
ATT_HD = 64
MOBA_BLOCK = 256
MOBA_TOPK = 3
Q_BLOCK = 64
D_FF = 4 * D_MODEL
EPS = 1e-6
NEG = -1e30
TINY = 1e-30

SPLITS = (LRU_WIDTH, LRU_WIDTH,
          HG_HEADS * HG_DK, HG_HEADS * HG_DK, HG_HEADS * HG_DV, HG_HEADS * HG_DV,
          ATT_HEADS * ATT_HD, ATT_HEADS * ATT_HD, ATT_HEADS * ATT_HD)
D_IN = sum(SPLITS)
D_OUT_CAT = LRU_WIDTH + HG_HEADS * HG_DV + ATT_HEADS * ATT_HD

kernel_name = "hybrid_rglru_hgrn2_moba_parallel_heads"


def rms_norm(x, g):
    xf = x.astype(jnp.float32)
    y = xf * lax.rsqrt(jnp.mean(xf * xf, axis=-1, keepdims=True) + EPS)
    return (y * g.astype(jnp.float32)).astype(x.dtype)


def rglru_group(xb, yb, conv_w, conv_b, wa, ba, wx, bx, lam):
    Bn, S, W = xb.shape
    xc = lax.conv_general_dilated(
        xb, conv_w.reshape(LRU_CONV, 1, W).astype(xb.dtype), window_strides=(1,),
        padding=[(LRU_CONV - 1, 0)], dimension_numbers=('NWC', 'WIO', 'NWC'),
        feature_group_count=W) + conv_b.astype(xb.dtype)
    xg = xc.reshape(Bn, S, LRU_BLOCKS, W // LRU_BLOCKS)
    r = jax.nn.sigmoid(jnp.einsum('bsgi,gij->bsgj', xg, wa).reshape(Bn, S, W).astype(jnp.float32)
                       + ba.astype(jnp.float32))
    i = jax.nn.sigmoid(jnp.einsum('bsgi,gij->bsgj', xg, wx).reshape(Bn, S, W).astype(jnp.float32)
                       + bx.astype(jnp.float32))
    log_a = -LRU_C * r * jax.nn.softplus(-lam.astype(jnp.float32))
    a = jnp.exp(log_a)
    u = jnp.sqrt(-jnp.expm1(2.0 * log_a)) * (i * xc.astype(jnp.float32))

    def combine(c1, c2):
        a1, b1 = c1
        a2, b2 = c2
        return a1 * a2, a2 * b1 + b2

    _, h = lax.associative_scan(combine, (a, u), axis=1)
    return (h * jax.nn.gelu(yb.astype(jnp.float32))).astype(xb.dtype)


def hgrn2_group(q, f, i, g, lb, norm_w):
    Bn, S, _ = q.shape
    nc = S // HG_CHUNK
    lb = lb.astype(jnp.float32)
    fpre = f.astype(jnp.float32)
    log_f = jnp.logaddexp(jnp.log(jnp.maximum(lb, TINY)), jnp.log1p(-lb) + jax.nn.log_sigmoid(fpre))
    k = (1.0 - lb) * jax.nn.sigmoid(-fpre)
    qs = jax.nn.silu(q.astype(jnp.float32))

    def heads(t, d):
        return t.astype(jnp.float32).reshape(Bn, nc, HG_CHUNK, HG_HEADS, d).transpose(1, 0, 3, 2, 4)

    qh, kh, lfh, vh = heads(qs, HG_DK), heads(k, HG_DK), heads(log_f, HG_DK), heads(i, HG_DV)
    mask = jnp.tril(jnp.ones((HG_CHUNK, HG_CHUNK), dtype=bool))

    def step(state, inp):
        qc, kc, lfc, vc = inp
        b = jnp.cumsum(lfc, axis=2)
        diff = b[:, :, :, None, :] - b[:, :, None, :, :]
        decay = jnp.exp(jnp.where(mask[:, :, None], diff, NEG))
        att = jnp.einsum('bhtd,bhsd,bhtsd->bhts', qc, kc, decay)
        o = jnp.einsum('bhts,bhse->bhte', att, vc) + jnp.einsum('bhtd,bhde->bhte', qc * jnp.exp(b), state)
        b_last = b[:, :, -1:, :]
        new_state = jnp.exp(b_last[:, :, 0, :])[..., None] * state + \
            jnp.einsum('bhsd,bhse->bhde', kc * jnp.exp(b_last - b), vc)
        return new_state, o

    s0 = jnp.zeros((Bn, HG_HEADS, HG_DK, HG_DV), jnp.float32)
    _, o = lax.scan(step, s0, (qh, kh, lfh, vh))
    o = o.transpose(1, 0, 3, 2, 4).reshape(Bn, S, HG_HEADS, HG_DV)
    o = rms_norm(o, norm_w) * jax.nn.silu(g.reshape(Bn, S, HG_HEADS, HG_DV).astype(jnp.float32))
    return o.reshape(Bn, S, HG_HEADS * HG_DV).astype(q.dtype)


def moba_group(cq, ck, cv):
    Bn, S, _ = cq.shape
    H, hd = ATT_HEADS, ATT_HD
    s_pad = -(-S // MOBA_BLOCK) * MOBA_BLOCK
    pad = s_pad - S

    def heads(t):
        t = t.reshape(Bn, S, H, hd).transpose(0, 2, 1, 3)
        return jnp.pad(t, ((0, 0), (0, 0), (0, pad), (0, 0)))

    q, k, v = heads(cq), heads(ck), heads(cv)
    nblk = s_pad // MOBA_BLOCK
    topk = min(MOBA_TOPK, nblk)
    kb = k.reshape(Bn, H, nblk, MOBA_BLOCK, hd)
    vb = v.reshape(Bn, H, nblk, MOBA_BLOCK, hd)
    kmean = jnp.mean(kb.astype(jnp.float32), axis=3).astype(k.dtype)
    scale = 1.0 / math.sqrt(hd)
    slopes = jnp.asarray(2.0 ** (-8.0 * np.arange(1, H + 1) / H), dtype=jnp.float32)
    nq = s_pad // Q_BLOCK
    qs = q.reshape(Bn, H, nq, Q_BLOCK, hd).transpose(2, 0, 1, 3, 4)
    bi = jnp.arange(Bn)[:, None, None, None]
    hi = jnp.arange(H)[None, :, None, None]
    blk_pos = jnp.arange(MOBA_BLOCK)

    def step(args):
        qi, q_blk = args
        t = qi * Q_BLOCK + jnp.arange(Q_BLOCK)
        j = (qi * Q_BLOCK) // MOBA_BLOCK
        gate = jnp.einsum('bhqd,bhnd->bhqn', q_blk, kmean).astype(jnp.float32)
        gate = jnp.where(jnp.arange(nblk) < j, gate, NEG)
        _, sel = lax.top_k(gate, topk)
        valid = jnp.arange(topk) < j
        ksel = kb[bi, hi, sel]
        vsel = vb[bi, hi, sel]
        s_sel = jnp.einsum('bhqd,bhqkld->bhqkl', q_blk, ksel).astype(jnp.float32) * scale
        pos_sel = sel[..., None] * MOBA_BLOCK + blk_pos
        s_sel = s_sel - slopes[:, None, None, None] * (t[:, None, None] - pos_sel).astype(jnp.float32)
        s_sel = jnp.where(valid[:, None], s_sel, NEG)
        k_own = lax.dynamic_slice_in_dim(kb, j, 1, axis=2)[:, :, 0]
        v_own = lax.dynamic_slice_in_dim(vb, j, 1, axis=2)[:, :, 0]
        s_own = jnp.einsum('bhqd,bhld->bhql', q_blk, k_own).astype(jnp.float32) * scale
        rel = t[:, None] - (j * MOBA_BLOCK + blk_pos)[None, :]
        s_own = jnp.where(rel >= 0, s_own - slopes[:, None, None] * rel.astype(jnp.float32), NEG)
        scores = jnp.concatenate([s_own, s_sel.reshape(Bn, H, Q_BLOCK, topk * MOBA_BLOCK)], axis=-1)
        p = jax.nn.softmax(scores, axis=-1).astype(v.dtype)
        p_own = p[..., :MOBA_BLOCK]
        p_sel = p[..., MOBA_BLOCK:].reshape(Bn, H, Q_BLOCK, topk, MOBA_BLOCK)
        out = jnp.einsum('bhql,bhld->bhqd', p_own, v_own) + jnp.einsum('bhqkl,bhqkld->bhqd', p_sel, vsel)
        return out.astype(v.dtype)

    o = lax.map(step, (jnp.arange(nq), qs))
    o = o.transpose(1, 2, 0, 3, 4).reshape(Bn, H, s_pad, hd)[:, :, :S]
    return o.transpose(0, 2, 1, 3).reshape(Bn, S, H * hd)


def setup_inputs(seed: int = 0) -> dict:
    key = jax.random.key(seed)
    ks = jax.random.split(key, 20)
    f32 = jnp.float32
    L = DEPTH
    nrm = lambda k, shape, fan: jax.random.normal(k, shape, f32) * (fan ** -0.5)
    gain = lambda k, shape: 1.0 + 0.05 * jax.random.normal(k, shape, f32)
    bw = LRU_WIDTH // LRU_BLOCKS
    u = jax.random.uniform(ks[10], (L, LRU_WIDTH), f32, 0.9, 0.999)
    a_base = u ** (1.0 / LRU_C)
    lam = jnp.log(a_base) - jnp.log1p(-a_base)
    return {
        "x": jax.random.normal(ks[0], (BATCH, SEQ, D_MODEL), f32),
        "w_in": nrm(ks[1], (L, D_MODEL, D_IN), D_MODEL),
        "w_out": nrm(ks[2], (L, D_OUT_CAT, D_MODEL), D_OUT_CAT),
        "norm_mix": gain(ks[3], (L, D_MODEL)),
        "norm_mlp": gain(ks[4], (L, D_MODEL)),
        "lru_conv_w": nrm(ks[5], (L, LRU_CONV, LRU_WIDTH), LRU_CONV),
        "lru_conv_b": 0.01 * jax.random.normal(ks[6], (L, LRU_WIDTH), f32),
        "lru_wa": nrm(ks[7], (L, LRU_BLOCKS, bw, bw), bw),
        "lru_ba": 0.01 * jax.random.normal(ks[8], (L, LRU_WIDTH), f32),
        "lru_wx": nrm(ks[9], (L, LRU_BLOCKS, bw, bw), bw),
        "lru_bx": 0.01 * jax.random.normal(ks[11], (L, LRU_WIDTH), f32),
        "lru_lambda": lam,
        "hg_lower_bounds": 0.1 * jax.random.normal(ks[12], (L, HG_HEADS * HG_DK), f32),
        "hg_norm_w": gain(ks[13], (L, HG_DV)),
        "lru_out_norm": gain(ks[14], (L, LRU_WIDTH)),
        "att_out_norm": gain(ks[15], (L, ATT_HEADS * ATT_HD)),
        "w_ff1": nrm(ks[16], (L, D_MODEL, D_FF), D_MODEL),
        "w_ff2": nrm(ks[17], (L, D_FF, D_MODEL), D_FF),
        "norm_final": gain(ks[18], (D_MODEL,)),
    }


def reference(x, w_in, w_out, norm_mix, norm_mlp, lru_conv_w, lru_conv_b, lru_wa, lru_ba, lru_wx, lru_bx,
              lru_lambda, hg_lower_bounds, hg_norm_w, lru_out_norm, att_out_norm, w_ff1, w_ff2, norm_final):
    lb_soft = jax.nn.softmax(hg_lower_bounds.astype(jnp.float32), axis=0)
    lower_bounds = jnp.cumsum(lb_soft, axis=0) - lb_soft[0]
    split_at = np.cumsum(SPLITS)[:-1].tolist()
    for l in range(DEPTH):
        h = rms_norm(x, norm_mix[l])
        z = jnp.einsum('bsd,de->bse', h, w_in[l])
        a_x, a_y, b_q, b_f, b_i, b_g, c_q, c_k, c_v = jnp.split(z, split_at, axis=-1)
        ya = rglru_group(a_x, a_y, lru_conv_w[l], lru_conv_b[l], lru_wa[l], lru_ba[l],
                         lru_wx[l], lru_bx[l], lru_lambda[l])
        yb = hgrn2_group(b_q, b_f, b_i, b_g, lower_bounds[l], hg_norm_w[l])
        yc = moba_group(c_q, c_k, c_v)
        y = jnp.concatenate([rms_norm(ya, lru_out_norm[l]), yb.astype(x.dtype),
                             rms_norm(yc, att_out_norm[l])], axis=-1)
        x = x + jnp.einsum('bse,ed->bsd', y, w_out[l])
        h = rms_norm(x, norm_mlp[l])
        u = jnp.square(jax.nn.relu(jnp.einsum('bsd,df->bsf', h, w_ff1[l])))
        x = x + jnp.einsum('bsf,fd->bsd', u, w_ff2[l])
    return rms_norm(x, norm_final)
```

```python
import functools
import math

import jax
import jax.numpy as jnp
from jax import lax
from jax.experimental import pallas as pl
from jax.experimental.pallas import tpu as pltpu

F32 = jnp.float32
BF16 = jnp.bfloat16
HIGHEST = lax.Precision.HIGHEST

EPS = 1e-6
NEG = -1e30
TINY = 1e-30

LRU_WIDTH = 256
LRU_BLOCKS = 4
LRU_CONV = 4
LRU_C = 8.0
HG_HEADS = 4
HG_DK = 64
ATT_HEADS = 8
ATT_HD = 64
MOBA_BLOCK = 256
MOBA_TOPK = 3

SUBLANES = 8
VMEM_LIMIT = 56 * 1024 * 1024

ROW_TILE = 512
LRU_TILE = 1024
LRU_SUB = 256
HG_TILE = 512
HG_CHUNK = 128
FF_CHUNK = 1024


def _dot(a, b):
    return jnp.dot(a, b, preferred_element_type=F32)


def _dot_t(a, b, precision=None):
    return lax.dot_general(a, b, (((1,), (1,)), ((), ())), precision=precision,
                           preferred_element_type=F32)


def _rms(x, g):
    return x * lax.rsqrt(jnp.mean(x * x, axis=-1, keepdims=True) + EPS) * g


def _sigmoid(x):
    return 1.0 / (1.0 + jnp.exp(-x))


def _softplus(x):
    return jnp.maximum(x, 0.0) + jnp.log1p(jnp.exp(-jnp.abs(x)))


def _params(*sem):
    return pltpu.CompilerParams(dimension_semantics=sem, vmem_limit_bytes=VMEM_LIMIT)


def _in_proj_kernel(x_ref, g_ref, w_ref, lru_ref, hg_ref, q_ref, k_ref, v_ref, km_ref):
    hb = _rms(x_ref[...], g_ref[...]).astype(BF16)
    lru_ref[...] = _dot(hb, w_ref[:, 0:512])
    hg_ref[...] = _dot(hb, w_ref[:, 512:1536])
    q_ref[...] = _dot(hb, w_ref[:, 1536:2048])
    k = _dot(hb, w_ref[:, 2048:2560])
    k_ref[...] = k.astype(BF16)
    for i in range(ROW_TILE // MOBA_BLOCK):
        km_ref[i] = jnp.mean(k[i * MOBA_BLOCK:(i + 1) * MOBA_BLOCK], axis=0, keepdims=True)
    v_ref[...] = _dot(hb, w_ref[:, 2560:3072]).astype(BF16)


def _in_proj(x2, g, w):
    t, d = x2.shape
    nblk = t // MOBA_BLOCK
    row = lambda width: pl.BlockSpec((ROW_TILE, width), lambda i: (i, 0))
    return pl.pallas_call(
        _in_proj_kernel,
        grid=(t // ROW_TILE,),
        in_specs=[row(d), pl.BlockSpec((1, d), lambda i: (0, 0)),
                  pl.BlockSpec(w.shape, lambda i: (0, 0))],
        out_specs=[row(512), row(1024), row(512), row(512), row(512),
                   pl.BlockSpec((ROW_TILE // MOBA_BLOCK, 1, 512), lambda i: (i, 0, 0))],
        out_shape=[jax.ShapeDtypeStruct((t, 512), F32), jax.ShapeDtypeStruct((t, 1024), F32),
                   jax.ShapeDtypeStruct((t, 512), F32), jax.ShapeDtypeStruct((t, 512), BF16),
                   jax.ShapeDtypeStruct((t, 512), BF16), jax.ShapeDtypeStruct((nblk, 1, 512), F32)],
        compiler_params=_params("parallel"),
        name="in_proj",
    )(x2, g, w)


def _rglru_kernel(xy_ref, cw_ref, cb_ref, wg_ref, bg_ref, lam_ref, nw_ref, o_ref, prev_ref, h_ref):
    @pl.when(pl.program_id(1) == 0)
    def _():
        prev_ref[...] = jnp.zeros_like(prev_ref)
        h_ref[...] = jnp.zeros_like(h_ref)

    w = LRU_WIDTH
    row = lax.broadcasted_iota(jnp.int32, (LRU_SUB, w), 0)
    sp = _softplus(-lam_ref[...])
    for c in range(LRU_TILE // LRU_SUB):
        rows = slice(c * LRU_SUB, (c + 1) * LRU_SUB)
        xb = xy_ref[rows, 0:w]
        yb = xy_ref[rows, w:2 * w]
        prev = prev_ref[...]
        xc = xb * cw_ref[LRU_CONV - 1:LRU_CONV, :] + cb_ref[...]
        for k in range(1, LRU_CONV):
            shifted = jnp.where(row >= k, pltpu.roll(xb, k, 0), pltpu.roll(prev, k, 0))
            xc = xc + shifted * cw_ref[LRU_CONV - 1 - k:LRU_CONV - k, :]
        prev_ref[...] = xb
        gates = _dot(xc.astype(BF16), wg_ref[...]) + bg_ref[...]
        r = _sigmoid(gates[:, 0:w])
        i = _sigmoid(gates[:, w:2 * w])
        log_a = -LRU_C * r * sp
        a = jnp.exp(log_a)
        b = jnp.sqrt(1.0 - jnp.exp(2.0 * log_a)) * (i * xc)
        s = 1
        while s < LRU_SUB:
            keep = row >= s
            a_s = jnp.where(keep, pltpu.roll(a, s, 0), 1.0)
            b_s = jnp.where(keep, pltpu.roll(b, s, 0), 0.0)
            b = a * b_s + b
            a = a * a_s
            s *= 2
        h = b + a * h_ref[...]
        h_ref[...] = h[LRU_SUB - 1:LRU_SUB, :]
        y = h * jax.nn.gelu(yb, approximate=True)
        o_ref[rows, :] = _rms(y, nw_ref[...]).astype(o_ref.dtype)


def _rglru(xy, bsz, seq, cw, cb, wg, bg, lam, nw):
    nt = seq // LRU_TILE
    w = LRU_WIDTH
    vec = lambda a: pl.BlockSpec(a.shape, lambda b, t: (0, 0))
    return pl.pallas_call(
        _rglru_kernel,
        grid=(bsz, nt),
        in_specs=[pl.BlockSpec((LRU_TILE, 2 * w), lambda b, t: (b * nt + t, 0)),
                  vec(cw), vec(cb), vec(wg), vec(bg), vec(lam), vec(nw)],
        out_specs=pl.BlockSpec((LRU_TILE, w), lambda b, t: (b * nt + t, 0)),
        out_shape=jax.ShapeDtypeStruct((bsz * seq, w), BF16),
        scratch_shapes=[pltpu.VMEM((LRU_SUB, w), F32), pltpu.VMEM((1, w), F32)],
        compiler_params=_params("parallel", "arbitrary"),
        name="rglru",
    )(xy, cw, cb, wg, bg, lam, nw)


def _hgrn2_kernel(layer, z_ref, lbp_ref, nw_ref, o_ref, state_ref):
    @pl.when(pl.program_id(1) == 0)
    def _():
        state_ref[...] = jnp.zeros_like(state_ref)

    n = HG_HEADS * HG_DK
    L = HG_CHUNK
    lbp = lbp_ref[...]
    e = jnp.exp(lbp - jnp.max(lbp, axis=0, keepdims=True))
    soft = e / jnp.sum(e, axis=0, keepdims=True)
    lb = jnp.sum(soft[0:layer + 1], axis=0, keepdims=True) - soft[0:1]
    log_lb = jnp.log(jnp.maximum(lb, TINY))
    log_1m = jnp.log1p(-lb)

    lane_h = lax.broadcasted_iota(jnp.int32, (1, n), 1) // HG_DK
    head_of_row = lax.broadcasted_iota(jnp.int32, (n, n), 0) // HG_DK
    head_of_col = lax.broadcasted_iota(jnp.int32, (n, n), 1) // HG_DK
    same_head = head_of_row == head_of_col
    ones_bd = jnp.where(same_head, 1.0, 0.0).astype(BF16)
    row = lax.broadcasted_iota(jnp.int32, (L, n), 0)
    sub = lax.broadcasted_iota(jnp.int32, (1, SUBLANES, 1), 1)
    t_idx = lax.broadcasted_iota(jnp.int32, (L, L), 0)
    s_idx = lax.broadcasted_iota(jnp.int32, (L, L), 1)

    def chunk(ci, carry):
        r0 = pl.multiple_of(ci * L, L)
        q = z_ref[pl.ds(r0, L), 0:n]
        f = z_ref[pl.ds(r0, L), n:2 * n]
        v = z_ref[pl.ds(r0, L), 2 * n:3 * n]
        g = z_ref[pl.ds(r0, L), 3 * n:4 * n]
        lsig = -_softplus(-f)
        t1 = log_1m + lsig
        log_f = jnp.maximum(log_lb, t1) + jnp.log1p(jnp.exp(-jnp.abs(log_lb - t1)))
        kk = (1.0 - lb) * _sigmoid(-f)
        qs = q * _sigmoid(q)
        b = log_f
        s = 1
        while s < L:
            b = b + jnp.where(row >= s, pltpu.roll(b, s, 0), 0.0)
            s *= 2

        g8 = L // SUBLANES
        q3 = qs.reshape(g8, SUBLANES, n)
        k3 = kk.reshape(g8, SUBLANES, n)
        b3 = b.reshape(g8, SUBLANES, n)
        v3 = v.reshape(g8, SUBLANES, n)
        o = jnp.zeros((L, n), F32)
        for j in range(SUBLANES):
            d = jnp.where(sub >= j, b3 - b3[:, j:j + 1, :], NEG)
            xj = q3 * k3[:, j:j + 1, :] * jnp.exp(d)
            aj = _dot(xj.reshape(L, n).astype(BF16), ones_bd)
            o = o + (aj.reshape(g8, SUBLANES, n) * v3[:, j:j + 1, :]).reshape(L, n)

        att = [jnp.zeros((L, L), F32) for _ in range(HG_HEADS)]
        c = SUBLANES
        while c < L:
            bc = b.reshape(L // c, c, n)
            bend = jnp.broadcast_to(bc[:, c - 1:c, :], (L // c, c, n)).reshape(L, n)
            bprev = jnp.where(row >= c, pltpu.roll(bend, c, 0), 0.0)
            qt = qs * jnp.exp(b - bprev)
            kt = (kk * jnp.exp(bend - b)).astype(BF16)
            pair = ((t_idx // c) % 2 == 1) & ((s_idx // c) == (t_idx // c) - 1)
            for h in range(HG_HEADS):
                sc = _dot_t(jnp.where(lane_h == h, qt, 0.0).astype(BF16), kt)
                att[h] = att[h] + jnp.where(pair, sc, 0.0)
            c *= 2
        att_cat = jnp.concatenate([a.astype(BF16) for a in att], axis=1)
        v_stack = jnp.concatenate([jnp.where(lane_h == h, v, 0.0).astype(BF16)
                                   for h in range(HG_HEADS)], axis=0)
        o = o + _dot(att_cat, v_stack)

        state = state_ref[...]
        o = o + _dot_t((qs * jnp.exp(b)).astype(BF16), state.astype(BF16))
        b_last = b[L - 1:L, :]
        kdec = (kk * jnp.exp(b_last - b)).astype(BF16)
        upd = lax.dot_general(v.astype(BF16), kdec, (((0,), (0,)), ((), ())),
                              preferred_element_type=F32)
        state_ref[...] = state * jnp.exp(b_last) + jnp.where(same_head, upd, 0.0)

        ms = jnp.dot(o * o, jnp.where(same_head, 1.0 / HG_DK, 0.0), precision=HIGHEST,
                     preferred_element_type=F32)
        y = o * lax.rsqrt(ms + EPS) * nw_ref[...] * (g * _sigmoid(g))
        o_ref[pl.ds(r0, L), :] = y.astype(o_ref.dtype)
        return carry

    lax.fori_loop(0, HG_TILE // L, chunk, 0)


def _hgrn2(z, bsz, seq, lbp, nw, layer):
    nt = seq // HG_TILE
    n = HG_HEADS * HG_DK
    return pl.pallas_call(
        functools.partial(_hgrn2_kernel, layer),
        grid=(bsz, nt),
        in_specs=[pl.BlockSpec((HG_TILE, 4 * n), lambda b, t: (b * nt + t, 0)),
                  pl.BlockSpec(lbp.shape, lambda b, t: (0, 0)),
                  pl.BlockSpec(nw.shape, lambda b, t: (0, 0))],
        out_specs=pl.BlockSpec((HG_TILE, n), lambda b, t: (b * nt + t, 0)),
        out_shape=jax.ShapeDtypeStruct((bsz * seq, n), BF16),
        scratch_shapes=[pltpu.VMEM((n, n), F32)],
        compiler_params=_params("parallel", "arbitrary"),
        name="hgrn2",
    )(z, lbp, nw)


def _moba_kernel(nblk, q_ref, k_ref, v_ref, km_ref, o_ref, m_ref, l_ref, acc_ref):
    j = pl.program_id(2)
    hp = pl.program_id(1)
    blk = MOBA_BLOCK
    hd = ATT_HD
    scale = 1.0 / math.sqrt(hd)
    lane = lax.broadcasted_iota(jnp.int32, (1, 2 * hd), 1)
    n_iota = lax.broadcasted_iota(jnp.int32, (blk, nblk), 1)
    rel = (lax.broadcasted_iota(jnp.int32, (blk, blk), 0)
           - lax.broadcasted_iota(jnp.int32, (blk, blk), 1))
    relf = rel.astype(F32)
    q = q_ref[...]
    km = km_ref[...]
    j0 = pl.multiple_of(j * blk, blk)
    k_own = k_ref[pl.ds(j0, blk), :]
    v_own = v_ref[pl.ds(j0, blk), :]

    qh, sel, slope = [], [], []
    for h in range(2):
        mine = (lane // hd) == h
        qm = jnp.where(mine, q, 0.0)
        slope.append(jnp.exp2(-8.0 * (2 * hp + h + 1).astype(F32) / ATT_HEADS))
        gate = _dot_t(qm, km, precision=HIGHEST)
        gate = jnp.where(n_iota < j, gate, NEG)
        chosen = jnp.zeros((blk, nblk), F32)
        for r in range(MOBA_TOPK):
            top = jnp.max(gate, axis=1, keepdims=True)
            first = jnp.min(jnp.where(gate == top, n_iota, nblk), axis=1, keepdims=True)
            pick = n_iota == first
            chosen = jnp.where(pick, jnp.where(j > r, 1.0, 0.0), chosen)
            gate = jnp.where(pick, -jnp.inf, gate)
        sel.append(chosen)
        qb = (qm * scale).astype(BF16)
        qh.append(qb)
        s = _dot_t(qb, k_own)
        s = jnp.where(rel >= 0, s - slope[h] * relf, NEG)
        m = jnp.max(s, axis=1, keepdims=True)
        p = jnp.exp(s - m)
        m_ref[h] = m
        l_ref[h] = jnp.sum(p, axis=1, keepdims=True)
        acc_ref[h] = _dot(p.astype(BF16), v_own)

    def body(n, carry):
        n0 = pl.multiple_of(n * blk, blk)
        k_n = k_ref[pl.ds(n0, blk), :]
        v_n = v_ref[pl.ds(n0, blk), :]
        dist = ((j - n) * blk).astype(F32)
        for h in range(2):
            picked = jnp.max(jnp.where(n_iota == n, sel[h], 0.0), axis=1, keepdims=True)
            s = _dot_t(qh[h], k_n) - slope[h] * (relf + dist)
            s = jnp.where(picked > 0.0, s, NEG)
            m_old = m_ref[h]
            m_new = jnp.maximum(m_old, jnp.max(s, axis=1, keepdims=True))
            alpha = jnp.exp(m_old - m_new)
            p = jnp.exp(s - m_new)
            m_ref[h] = m_new
            l_ref[h] = alpha * l_ref[h] + jnp.sum(p, axis=1, keepdims=True)
            acc_ref[h] = alpha * acc_ref[h] + _dot(p.astype(BF16), v_n)
        return carry

    lax.fori_loop(0, j, body, 0)
    out0 = acc_ref[0] / l_ref[0]
    out1 = acc_ref[1] / l_ref[1]
    o_ref[...] = jnp.where(lane < hd, out0, out1)


def _moba(q, k, v, km, bsz, seq):
    nblk = seq // MOBA_BLOCK
    blk = MOBA_BLOCK
    pair = 2 * ATT_HD
    npair = ATT_HEADS // 2
    return pl.pallas_call(
        functools.partial(_moba_kernel, nblk),
        grid=(bsz, npair, nblk),
        in_specs=[pl.BlockSpec((blk, pair), lambda b, p, j: (b * nblk + j, p)),
                  pl.BlockSpec((seq, pair), lambda b, p, j: (b, p)),
                  pl.BlockSpec((seq, pair), lambda b, p, j: (b, p)),
                  pl.BlockSpec((nblk, pair), lambda b, p, j: (b, p))],
        out_specs=pl.BlockSpec((blk, pair), lambda b, p, j: (b * nblk + j, p)),
        out_shape=jax.ShapeDtypeStruct((bsz * seq, ATT_HEADS * ATT_HD), F32),
        scratch_shapes=[pltpu.VMEM((2, blk, 1), F32), pltpu.VMEM((2, blk, 1), F32),
                        pltpu.VMEM((2, blk, pair), F32)],
        compiler_params=_params("parallel", "parallel", "arbitrary"),
        name="moba",
    )(q, k, v, km)


def _out_mlp_kernel(final, x_ref, ya_ref, yb_ref, yc_ref, an_ref, wo_ref, gm_ref, w1_ref, w2_ref,
                    gf_ref, o_ref):
    yc = _rms(yc_ref[...], an_ref[...]).astype(BF16)
    y = jnp.concatenate([ya_ref[...], yb_ref[...], yc], axis=1)
    x = x_ref[...] + _dot(y, wo_ref[...])
    hb = _rms(x, gm_ref[...]).astype(BF16)
    mlp = None
    for c in range(w1_ref.shape[1] // FF_CHUNK):
        cols = slice(c * FF_CHUNK, (c + 1) * FF_CHUNK)
        u = jnp.square(jnp.maximum(_dot(hb, w1_ref[:, cols]), 0.0))
        part = _dot(u.astype(BF16), w2_ref[cols, :])
        mlp = part if mlp is None else mlp + part
    x = x + mlp
    if final:
        x = _rms(x, gf_ref[...])
    o_ref[...] = x


def _out_mlp(x2, ya, yb, yc, an, wo, gm, w1, w2, gf, final):
    t, d = x2.shape
    row = lambda a: pl.BlockSpec((ROW_TILE, a.shape[1]), lambda i: (i, 0))
    full = lambda a: pl.BlockSpec(a.shape, lambda i: (0, 0))
    return pl.pallas_call(
        functools.partial(_out_mlp_kernel, final),
        grid=(t // ROW_TILE,),
        in_specs=[row(x2), row(ya), row(yb), row(yc), full(an), full(wo), full(gm), full(w1),
                  full(w2), full(gf)],
        out_specs=pl.BlockSpec((ROW_TILE, d), lambda i: (i, 0)),
        out_shape=jax.ShapeDtypeStruct((t, d), F32),
        compiler_params=_params("parallel"),
        name="out_mlp",
    )(x2, ya, yb, yc, an, wo, gm, w1, w2, gf)


def _block_diag(w):
    g, n, _ = w.shape
    eye = jnp.eye(g, dtype=w.dtype)
    return (eye[:, None, :, None] * w[:, :, None, :]).reshape(g * n, g * n)


def kernel(x, w_in, w_out, norm_mix, norm_mlp, lru_conv_w, lru_conv_b, lru_wa, lru_ba, lru_wx, lru_bx,
           lru_lambda, hg_lower_bounds, hg_norm_w, lru_out_norm, att_out_norm, w_ff1, w_ff2, norm_final):
    bsz, seq, d = x.shape
    depth = w_in.shape[0]
    x2 = x.reshape(bsz * seq, d)
    r2 = lambda a: a.reshape(1, -1)
    for l in range(depth):
        lru, hg, cq, ck, cv, km = _in_proj(x2, r2(norm_mix[l]), w_in[l].astype(BF16))
        wg = jnp.concatenate([_block_diag(lru_wa[l]), _block_diag(lru_wx[l])], axis=1).astype(BF16)
        bg = jnp.concatenate([lru_ba[l], lru_bx[l]]).reshape(1, -1)
        ya = _rglru(lru, bsz, seq, lru_conv_w[l], r2(lru_conv_b[l]), wg, bg, r2(lru_lambda[l]),
                    r2(lru_out_norm[l]))
        yb = _hgrn2(hg, bsz, seq, hg_lower_bounds, r2(jnp.tile(hg_norm_w[l], HG_HEADS)), l)
        yc = _moba(cq, ck, cv, km.reshape(-1, km.shape[-1]), bsz, seq)
        x2 = _out_mlp(x2, ya, yb, yc, r2(att_out_norm[l]), w_out[l].astype(BF16), r2(norm_mlp[l]),
                      w_ff1[l].astype(BF16), w_ff2[l].astype(BF16), r2(norm_final), l == depth - 1)
    return x2.reshape(bsz, seq, d)
```

```python
import functools
import math

import jax
import jax.numpy as jnp
from jax import lax
from jax.experimental import pallas as pl
from jax.experimental.pallas import tpu as pltpu

F32 = jnp.float32
BF16 = jnp.bfloat16
HIGHEST = lax.Precision.HIGHEST

EPS = 1e-6
NEG = -1e30
TINY = 1e-30

LRU_WIDTH = 256
LRU_BLOCKS = 4
LRU_CONV = 4
LRU_C = 8.0
HG_HEADS = 4
HG_DK = 64
ATT_HEADS = 8
ATT_HD = 64
MOBA_BLOCK = 256
MOBA_TOPK = 3

SUBLANES = 8
VMEM_LIMIT = 56 * 1024 * 1024

ROW_TILE = 512
LRU_TILE = 1024
LRU_SUB = 256
HG_TILE = 512
HG_CHUNK = 128
FF_CHUNK = 1024
MOBA_GROUP = 4


def _dot(a, b):
    return jnp.dot(a, b, preferred_element_type=F32)


def _dot_t(a, b, precision=None):
    return lax.dot_general(a, b, (((1,), (1,)), ((), ())), precision=precision,
                           preferred_element_type=F32)


def _rms(x, g):
    return x * lax.rsqrt(jnp.mean(x * x, axis=-1, keepdims=True) + EPS) * g


def _sigmoid(x):
    return 1.0 / (1.0 + jnp.exp(-x))


def _softplus(x):
    return jnp.maximum(x, 0.0) + jnp.log1p(jnp.exp(-jnp.abs(x)))


def _params(*sem):
    return pltpu.CompilerParams(dimension_semantics=sem, vmem_limit_bytes=VMEM_LIMIT)


def _in_proj_kernel(x_ref, g_ref, w_ref, lru_ref, hg_ref, q_ref, kt_ref, v_ref, km_ref):
    hb = _rms(x_ref[...], g_ref[...]).astype(BF16)
    lru_ref[...] = _dot(hb, w_ref[:, 0:512])
    hg_ref[...] = _dot(hb, w_ref[:, 512:1536])
    q_ref[...] = _dot(hb, w_ref[:, 1536:2048])
    k = _dot(hb, w_ref[:, 2048:2560])
    kt_ref[...] = k.T.astype(BF16)
    for i in range(ROW_TILE // MOBA_BLOCK):
        km_ref[i] = jnp.mean(k[i * MOBA_BLOCK:(i + 1) * MOBA_BLOCK], axis=0, keepdims=True)
    v_ref[...] = _dot(hb, w_ref[:, 2560:3072]).astype(BF16)


def _in_proj(x2, g, w):
    t, d = x2.shape
    nblk = t // MOBA_BLOCK
    row = lambda width: pl.BlockSpec((ROW_TILE, width), lambda i: (i, 0))
    return pl.pallas_call(
        _in_proj_kernel,
        grid=(t // ROW_TILE,),
        in_specs=[row(d), pl.BlockSpec((1, d), lambda i: (0, 0)),
                  pl.BlockSpec(w.shape, lambda i: (0, 0))],
        out_specs=[row(512), row(1024), row(512), pl.BlockSpec((512, ROW_TILE), lambda i: (0, i)), row(512),
                   pl.BlockSpec((ROW_TILE // MOBA_BLOCK, 1, 512), lambda i: (i, 0, 0))],
        out_shape=[jax.ShapeDtypeStruct((t, 512), F32), jax.ShapeDtypeStruct((t, 1024), F32),
                   jax.ShapeDtypeStruct((t, 512), F32), jax.ShapeDtypeStruct((512, t), BF16),
                   jax.ShapeDtypeStruct((t, 512), BF16), jax.ShapeDtypeStruct((nblk, 1, 512), F32)],
        compiler_params=_params("parallel"),
        name="in_proj",
    )(x2, g, w)


def _rglru_kernel(xy_ref, cw_ref, cb_ref, wg_ref, bg_ref, lam_ref, nw_ref, o_ref, prev_ref, h_ref):
    @pl.when(pl.program_id(1) == 0)
    def _():
        prev_ref[...] = jnp.zeros_like(prev_ref)
        h_ref[...] = jnp.zeros_like(h_ref)

    w = LRU_WIDTH
    row = lax.broadcasted_iota(jnp.int32, (LRU_SUB, w), 0)
    sp = _softplus(-lam_ref[...])
    for c in range(LRU_TILE // LRU_SUB):
        rows = slice(c * LRU_SUB, (c + 1) * LRU_SUB)
        xb = xy_ref[rows, 0:w]
        yb = xy_ref[rows, w:2 * w]
        prev = prev_ref[...]
        xc = xb * cw_ref[LRU_CONV - 1:LRU_CONV, :] + cb_ref[...]
        for k in range(1, LRU_CONV):
            shifted = jnp.where(row >= k, pltpu.roll(xb, k, 0), pltpu.roll(prev, k, 0))
            xc = xc + shifted * cw_ref[LRU_CONV - 1 - k:LRU_CONV - k, :]
        prev_ref[...] = xb
        gates = _dot(xc.astype(BF16), wg_ref[...]) + bg_ref[...]
        r = _sigmoid(gates[:, 0:w])
        i = _sigmoid(gates[:, w:2 * w])
        log_a = -LRU_C * r * sp
        a = jnp.exp(log_a)
        b = jnp.sqrt(1.0 - jnp.exp(2.0 * log_a)) * (i * xc)
        s = 1
        while s < LRU_SUB:
            keep = row >= s
            a_s = jnp.where(keep, pltpu.roll(a, s, 0), 1.0)
            b_s = jnp.where(keep, pltpu.roll(b, s, 0), 0.0)
            b = a * b_s + b
            a = a * a_s
            s *= 2
        h = b + a * h_ref[...]
        h_ref[...] = h[LRU_SUB - 1:LRU_SUB, :]
        y = h * jax.nn.gelu(yb, approximate=True)
        o_ref[rows, :] = _rms(y, nw_ref[...]).astype(o_ref.dtype)


def _rglru(xy, bsz, seq, cw, cb, wg, bg, lam, nw):
    nt = seq // LRU_TILE
    w = LRU_WIDTH
    vec = lambda a: pl.BlockSpec(a.shape, lambda b, t: (0, 0))
    return pl.pallas_call(
        _rglru_kernel,
        grid=(bsz, nt),
        in_specs=[pl.BlockSpec((LRU_TILE, 2 * w), lambda b, t: (b * nt + t, 0)),
                  vec(cw), vec(cb), vec(wg), vec(bg), vec(lam), vec(nw)],
        out_specs=pl.BlockSpec((LRU_TILE, w), lambda b, t: (b * nt + t, 0)),
        out_shape=jax.ShapeDtypeStruct((bsz * seq, w), BF16),
        scratch_shapes=[pltpu.VMEM((LRU_SUB, w), F32), pltpu.VMEM((1, w), F32)],
        compiler_params=_params("parallel", "arbitrary"),
        name="rglru",
    )(xy, cw, cb, wg, bg, lam, nw)


def _hgrn2_kernel(layer, z_ref, lbp_ref, nw_ref, o_ref, state_ref):
    @pl.when(pl.program_id(1) == 0)
    def _():
        state_ref[...] = jnp.zeros_like(state_ref)

    n = HG_HEADS * HG_DK
    L = HG_CHUNK
    lbp = lbp_ref[...]
    e = jnp.exp(lbp - jnp.max(lbp, axis=0, keepdims=True))
    soft = e / jnp.sum(e, axis=0, keepdims=True)
    lb = jnp.sum(soft[0:layer + 1], axis=0, keepdims=True) - soft[0:1]
    log_lb = jnp.log(jnp.maximum(lb, TINY))
    log_1m = jnp.log1p(-lb)

    lane_h = lax.broadcasted_iota(jnp.int32, (1, n), 1) // HG_DK
    head_of_row = lax.broadcasted_iota(jnp.int32, (n, n), 0) // HG_DK
    head_of_col = lax.broadcasted_iota(jnp.int32, (n, n), 1) // HG_DK
    same_head = head_of_row == head_of_col
    ones_bd = jnp.where(same_head, 1.0, 0.0).astype(BF16)
    row = lax.broadcasted_iota(jnp.int32, (L, n), 0)
    sub = lax.broadcasted_iota(jnp.int32, (1, SUBLANES, 1), 1)
    t_idx = lax.broadcasted_iota(jnp.int32, (L, L), 0)
    s_idx = lax.broadcasted_iota(jnp.int32, (L, L), 1)

    def chunk(ci, carry):
        r0 = pl.multiple_of(ci * L, L)
        q = z_ref[pl.ds(r0, L), 0:n]
        f = z_ref[pl.ds(r0, L), n:2 * n]
        v = z_ref[pl.ds(r0, L), 2 * n:3 * n]
        g = z_ref[pl.ds(r0, L), 3 * n:4 * n]
        lsig = -_softplus(-f)
        t1 = log_1m + lsig
        log_f = jnp.maximum(log_lb, t1) + jnp.log1p(jnp.exp(-jnp.abs(log_lb - t1)))
        kk = (1.0 - lb) * _sigmoid(-f)
        qs = q * _sigmoid(q)
        b = log_f
        s = 1
        while s < L:
            b = b + jnp.where(row >= s, pltpu.roll(b, s, 0), 0.0)
            s *= 2

        g8 = L // SUBLANES
        q3 = qs.reshape(g8, SUBLANES, n)
        k3 = kk.reshape(g8, SUBLANES, n)
        b3 = b.reshape(g8, SUBLANES, n)
        v3 = v.reshape(g8, SUBLANES, n)
        o = jnp.zeros((L, n), F32)
        for j in range(SUBLANES):
            d = jnp.where(sub >= j, b3 - b3[:, j:j + 1, :], NEG)
            xj = q3 * k3[:, j:j + 1, :] * jnp.exp(d)
            aj = _dot(xj.reshape(L, n).astype(BF16), ones_bd)
            o = o + (aj.reshape(g8, SUBLANES, n) * v3[:, j:j + 1, :]).reshape(L, n)

        att = [jnp.zeros((L, L), F32) for _ in range(HG_HEADS)]
        c = SUBLANES
        while c < L:
            bc = b.reshape(L // c, c, n)
            bend = jnp.broadcast_to(bc[:, c - 1:c, :], (L // c, c, n)).reshape(L, n)
            bprev = jnp.where(row >= c, pltpu.roll(bend, c, 0), 0.0)
            qt = qs * jnp.exp(b - bprev)
            kt = (kk * jnp.exp(bend - b)).astype(BF16)
            pair = ((t_idx // c) % 2 == 1) & ((s_idx // c) == (t_idx // c) - 1)
            for h in range(HG_HEADS):
                sc = _dot_t(jnp.where(lane_h == h, qt, 0.0).astype(BF16), kt)
                att[h] = att[h] + jnp.where(pair, sc, 0.0)
            c *= 2
        att_cat = jnp.concatenate([a.astype(BF16) for a in att], axis=1)
        v_stack = jnp.concatenate([jnp.where(lane_h == h, v, 0.0).astype(BF16)
                                   for h in range(HG_HEADS)], axis=0)
        o = o + _dot(att_cat, v_stack)

        state = state_ref[...]
        o = o + _dot_t((qs * jnp.exp(b)).astype(BF16), state.astype(BF16))
        b_last = b[L - 1:L, :]
        kdec = (kk * jnp.exp(b_last - b)).astype(BF16)
        upd = lax.dot_general(v.astype(BF16), kdec, (((0,), (0,)), ((), ())),
                              preferred_element_type=F32)
        state_ref[...] = state * jnp.exp(b_last) + jnp.where(same_head, upd, 0.0)

        ms = jnp.dot(o * o, jnp.where(same_head, 1.0 / HG_DK, 0.0), precision=HIGHEST,
                     preferred_element_type=F32)
        y = o * lax.rsqrt(ms + EPS) * nw_ref[...] * (g * _sigmoid(g))
        o_ref[pl.ds(r0, L), :] = y.astype(o_ref.dtype)
        return carry

    lax.fori_loop(0, HG_TILE // L, chunk, 0)


def _hgrn2(z, bsz, seq, lbp, nw, layer):
    nt = seq // HG_TILE
    n = HG_HEADS * HG_DK
    return pl.pallas_call(
        functools.partial(_hgrn2_kernel, layer),
        grid=(bsz, nt),
        in_specs=[pl.BlockSpec((HG_TILE, 4 * n), lambda b, t: (b * nt + t, 0)),
                  pl.BlockSpec(lbp.shape, lambda b, t: (0, 0)),
                  pl.BlockSpec(nw.shape, lambda b, t: (0, 0))],
        out_specs=pl.BlockSpec((HG_TILE, n), lambda b, t: (b * nt + t, 0)),
        out_shape=jax.ShapeDtypeStruct((bsz * seq, n), BF16),
        scratch_shapes=[pltpu.VMEM((n, n), F32)],
        compiler_params=_params("parallel", "arbitrary"),
        name="hgrn2",
    )(z, lbp, nw)


def _moba_kernel(nblk, q_ref, kt_ref, v_ref, km_ref, kxt_ref, o_ref, qa_ref, m_ref, acc_ref):
    j = pl.program_id(2)
    hp = pl.program_id(1)
    blk = MOBA_BLOCK
    hd = ATT_HD
    pair = 2 * hd
    span = MOBA_GROUP * blk
    scale = 1.0 / math.sqrt(hd)
    lane = lax.broadcasted_iota(jnp.int32, (1, pair), 1)
    n_t = lax.broadcasted_iota(jnp.int32, (pair, blk), 0)
    t_row = lax.broadcasted_iota(jnp.int32, (pair, blk), 1)
    causal = (lax.broadcasted_iota(jnp.int32, (blk, blk), 0)
              >= lax.broadcasted_iota(jnp.int32, (blk, blk), 1))
    q = q_ref[...]
    km = jnp.concatenate([km_ref[...], jnp.zeros((pair - nblk, pair), F32)], axis=0)
    j0 = pl.multiple_of(j * blk, blk)
    jf = j.astype(F32)

    def keys(start, size):
        return jnp.concatenate([kt_ref[:, pl.ds(start, size)], kxt_ref[:, pl.ds(start, size)]], axis=0)

    k_own = keys(j0, blk)
    v_own = v_ref[pl.ds(j0, blk), :]
    mine = [(lane // hd) == h for h in range(2)]
    for h in range(2):
        slope = jnp.exp2(-8.0 * (2 * hp + h + 1).astype(F32) / ATT_HEADS)
        qm = jnp.where(mine[h], q, 0.0)
        gate = _dot_t(km, qm, precision=HIGHEST)
        gate = jnp.where(n_t < j, gate, NEG)
        chosen = jnp.zeros((pair, blk), F32)
        for r in range(MOBA_TOPK):
            top = jnp.max(gate, axis=0, keepdims=True)
            first = jnp.min(jnp.where(gate == top, n_t, pair), axis=0, keepdims=True)
            pick = n_t == first
            chosen = jnp.where(pick, jnp.where(j > r, 1.0, 0.0), chosen)
            gate = jnp.where(pick, -jnp.inf, gate)
        bias = jnp.where(n_t == nblk, -slope * t_row.astype(F32),
                         jnp.where(n_t == nblk + 1, slope,
                                   jnp.where(n_t == nblk + 2, slope * blk,
                                             jnp.where(n_t == nblk + 3, -slope * blk * jf, 0.0))))
        past = jnp.where(n_t < nblk, jnp.where(chosen > 0.0, 0.0, NEG), bias)
        own = jnp.where(n_t < nblk, jnp.where(n_t == j, 0.0, NEG), bias)
        qs = (qm * scale).astype(BF16)
        qa_ref[h] = jnp.concatenate([qs, past.T.astype(BF16)], axis=1)
        s = _dot(jnp.concatenate([qs, own.T.astype(BF16)], axis=1), k_own)
        s = jnp.where(causal, s, NEG)
        m = jnp.max(s, axis=1, keepdims=True)
        p = jnp.exp(s - m)
        m_ref[h] = m
        acc_ref[h] = _dot(p.astype(BF16), jnp.where(mine[h], v_own, 1.0).astype(BF16))

    def body(i, carry):
        start = pl.multiple_of(i * span, span)
        k_g = keys(start, span)
        v_g = v_ref[pl.ds(start, span), :]
        for h in range(2):
            s = _dot(qa_ref[h], k_g)
            m_old = m_ref[h]
            m_new = jnp.maximum(m_old, jnp.max(s, axis=1, keepdims=True))
            p = jnp.exp(s - m_new)
            m_ref[h] = m_new
            pv = _dot(p.astype(BF16), jnp.where(mine[h], v_g, 1.0).astype(BF16))
            acc_ref[h] = jnp.exp(m_old - m_new) * acc_ref[h] + pv
        return carry

    lax.fori_loop(0, (j + MOBA_GROUP - 1) // MOBA_GROUP, body, 0)
    out = []
    for h in range(2):
        acc = acc_ref[h]
        out.append(acc / pltpu.roll(acc, hd, 1))
    o_ref[...] = jnp.where(mine[0], out[0], out[1])


def _moba(q, kt, v, km, bsz, seq):
    nblk = seq // MOBA_BLOCK
    blk = MOBA_BLOCK
    pair = 2 * ATT_HD
    npair = ATT_HEADS // 2
    assert nblk + 4 <= pair and nblk % MOBA_GROUP == 0
    pos = jnp.arange(seq, dtype=jnp.int32)[None, :]
    row = jnp.arange(pair, dtype=jnp.int32)[:, None]
    extras = [jnp.ones_like(pos), pos % blk, pos // blk, jnp.ones_like(pos)]
    kxt = (pos // blk == row).astype(F32)
    for i, e in enumerate(extras):
        kxt = jnp.where(row == nblk + i, e.astype(F32), kxt)
    kxt = kxt.astype(BF16)
    return pl.pallas_call(
        functools.partial(_moba_kernel, nblk),
        grid=(bsz, npair, nblk),
        in_specs=[pl.BlockSpec((blk, pair), lambda b, p, j: (b * nblk + j, p)),
                  pl.BlockSpec((pair, seq), lambda b, p, j: (p, b)),
                  pl.BlockSpec((seq, pair), lambda b, p, j: (b, p)),
                  pl.BlockSpec((nblk, pair), lambda b, p, j: (b, p)),
                  pl.BlockSpec((pair, seq), lambda b, p, j: (0, 0))],
        out_specs=pl.BlockSpec((blk, pair), lambda b, p, j: (b * nblk + j, p)),
        out_shape=jax.ShapeDtypeStruct((bsz * seq, ATT_HEADS * ATT_HD), F32),
        scratch_shapes=[pltpu.VMEM((2, blk, 2 * pair), BF16), pltpu.VMEM((2, blk, 1), F32),
                        pltpu.VMEM((2, blk, pair), F32)],
        compiler_params=_params("parallel", "parallel", "arbitrary"),
        name="moba",
    )(q, kt, v, km, kxt)


def _out_mlp_kernel(final, x_ref, ya_ref, yb_ref, yc_ref, an_ref, wo_ref, gm_ref, w1_ref, w2_ref,
                    gf_ref, o_ref):
    yc = _rms(yc_ref[...], an_ref[...]).astype(BF16)
    y = jnp.concatenate([ya_ref[...], yb_ref[...], yc], axis=1)
    x = x_ref[...] + _dot(y, wo_ref[...])
    hb = _rms(x, gm_ref[...]).astype(BF16)
    mlp = None
    for c in range(w1_ref.shape[1] // FF_CHUNK):
        cols = slice(c * FF_CHUNK, (c + 1) * FF_CHUNK)
        u = jnp.square(jnp.maximum(_dot(hb, w1_ref[:, cols]), 0.0))
        part = _dot(u.astype(BF16), w2_ref[cols, :])
        mlp = part if mlp is None else mlp + part
    x = x + mlp
    if final:
        x = _rms(x, gf_ref[...])
    o_ref[...] = x


def _out_mlp(x2, ya, yb, yc, an, wo, gm, w1, w2, gf, final):
    t, d = x2.shape
    row = lambda a: pl.BlockSpec((ROW_TILE, a.shape[1]), lambda i: (i, 0))
    full = lambda a: pl.BlockSpec(a.shape, lambda i: (0, 0))
    return pl.pallas_call(
        functools.partial(_out_mlp_kernel, final),
        grid=(t // ROW_TILE,),
        in_specs=[row(x2), row(ya), row(yb), row(yc), full(an), full(wo), full(gm), full(w1),
                  full(w2), full(gf)],
        out_specs=pl.BlockSpec((ROW_TILE, d), lambda i: (i, 0)),
        out_shape=jax.ShapeDtypeStruct((t, d), F32),
        compiler_params=_params("parallel"),
        name="out_mlp",
    )(x2, ya, yb, yc, an, wo, gm, w1, w2, gf)


def _block_diag(w):
    g, n, _ = w.shape
    eye = jnp.eye(g, dtype=w.dtype)
    return (eye[:, None, :, None] * w[:, :, None, :]).reshape(g * n, g * n)


def kernel(x, w_in, w_out, norm_mix, norm_mlp, lru_conv_w, lru_conv_b, lru_wa, lru_ba, lru_wx, lru_bx,
           lru_lambda, hg_lower_bounds, hg_norm_w, lru_out_norm, att_out_norm, w_ff1, w_ff2, norm_final):
    bsz, seq, d = x.shape
    depth = w_in.shape[0]
    x2 = x.reshape(bsz * seq, d)
    r2 = lambda a: a.reshape(1, -1)
    for l in range(depth):
        lru, hg, cq, ckt, cv, km = _in_proj(x2, r2(norm_mix[l]), w_in[l].astype(BF16))
        wg = jnp.concatenate([_block_diag(lru_wa[l]), _block_diag(lru_wx[l])], axis=1).astype(BF16)
        bg = jnp.concatenate([lru_ba[l], lru_bx[l]]).reshape(1, -1)
        ya = _rglru(lru, bsz, seq, lru_conv_w[l], r2(lru_conv_b[l]), wg, bg, r2(lru_lambda[l]),
                    r2(lru_out_norm[l]))
        yb = _hgrn2(hg, bsz, seq, hg_lower_bounds, r2(jnp.tile(hg_norm_w[l], HG_HEADS)), l)
        yc = _moba(cq, ckt, cv, km.reshape(-1, km.shape[-1]), bsz, seq)
        x2 = _out_mlp(x2, ya, yb, yc, r2(att_out_norm[l]), w_out[l].astype(BF16), r2(norm_mlp[l]),
                      w_ff1[l].astype(BF16), w_ff2[l].astype(BF16), r2(norm_final), l == depth - 1)
    return x2.reshape(bsz, seq, d)
```

```python
import functools
import math

import jax
import jax.numpy as jnp
from jax import lax
from jax.experimental import pallas as pl
from jax.experimental.pallas import tpu as pltpu

F32 = jnp.float32
BF16 = jnp.bfloat16
HIGHEST = lax.Precision.HIGHEST

EPS = 1e-6
NEG = -1e30
TINY = 1e-30

LRU_WIDTH = 256
LRU_BLOCKS = 4
LRU_CONV = 4
LRU_C = 8.0
HG_HEADS = 4
HG_DK = 64
ATT_HEADS = 8
ATT_HD = 64
MOBA_BLOCK = 256
MOBA_TOPK = 3

SUBLANES = 8
VMEM_LIMIT = 56 * 1024 * 1024

ROW_TILE = 512
LRU_TILE = 1024
LRU_SUB = 256
HG_TILE = 512
HG_CHUNK = 128
FF_CHUNK = 1024
MOBA_GROUP = 2


def _dot(a, b):
    return jnp.dot(a, b, preferred_element_type=F32)


def _dot_t(a, b, precision=None):
    return lax.dot_general(a, b, (((1,), (1,)), ((), ())), precision=precision,
                           preferred_element_type=F32)


def _rms(x, g):
    return x * lax.rsqrt(jnp.mean(x * x, axis=-1, keepdims=True) + EPS) * g


def _sigmoid(x):
    return 1.0 / (1.0 + jnp.exp(-x))


def _softplus(x):
    return jnp.maximum(x, 0.0) + jnp.log1p(jnp.exp(-jnp.abs(x)))


def _params(*sem, flags=None):
    return pltpu.CompilerParams(dimension_semantics=sem, vmem_limit_bytes=VMEM_LIMIT, flags=flags)


def _in_proj_kernel(x_ref, g_ref, w_ref, lru_ref, hg_ref, q_ref, kt_ref, v_ref, km_ref):
    hb = _rms(x_ref[...], g_ref[...]).astype(BF16)
    lru_ref[...] = _dot(hb, w_ref[:, 0:512])
    hg_ref[...] = _dot(hb, w_ref[:, 512:1536])
    q_ref[...] = _dot(hb, w_ref[:, 1536:2048])
    k = _dot(hb, w_ref[:, 2048:2560])
    kt_ref[...] = k.T.astype(BF16)
    for i in range(ROW_TILE // MOBA_BLOCK):
        km_ref[i] = jnp.mean(k[i * MOBA_BLOCK:(i + 1) * MOBA_BLOCK], axis=0, keepdims=True)
    v = _dot(hb, w_ref[:, 2560:3072]).astype(BF16)
    pair = 2 * ATT_HD
    low = lax.broadcasted_iota(jnp.int32, (1, pair), 1) < ATT_HD
    for p in range(ATT_HEADS // 2):
        vp = v[:, p * pair:(p + 1) * pair]
        v_ref[:, 2 * p * pair:(2 * p + 1) * pair] = jnp.where(low, vp, 1.0).astype(BF16)
        v_ref[:, (2 * p + 1) * pair:(2 * p + 2) * pair] = jnp.where(low, 1.0, vp).astype(BF16)


def _in_proj(x2, g, w):
    t, d = x2.shape
    nblk = t // MOBA_BLOCK
    row = lambda width: pl.BlockSpec((ROW_TILE, width), lambda i: (i, 0))
    return pl.pallas_call(
        _in_proj_kernel,
        grid=(t // ROW_TILE,),
        in_specs=[row(d), pl.BlockSpec((1, d), lambda i: (0, 0)),
                  pl.BlockSpec(w.shape, lambda i: (0, 0))],
        out_specs=[row(512), row(1024), row(512), pl.BlockSpec((512, ROW_TILE), lambda i: (0, i)), row(1024),
                   pl.BlockSpec((ROW_TILE // MOBA_BLOCK, 1, 512), lambda i: (i, 0, 0))],
        out_shape=[jax.ShapeDtypeStruct((t, 512), F32), jax.ShapeDtypeStruct((t, 1024), F32),
                   jax.ShapeDtypeStruct((t, 512), F32), jax.ShapeDtypeStruct((512, t), BF16),
                   jax.ShapeDtypeStruct((t, 1024), BF16), jax.ShapeDtypeStruct((nblk, 1, 512), F32)],
        compiler_params=_params("parallel"),
        name="in_proj",
    )(x2, g, w)


def _rglru_kernel(xy_ref, cw_ref, cb_ref, wg_ref, bg_ref, lam_ref, nw_ref, o_ref, prev_ref, h_ref):
    @pl.when(pl.program_id(1) == 0)
    def _():
        prev_ref[...] = jnp.zeros_like(prev_ref)
        h_ref[...] = jnp.zeros_like(h_ref)

    w = LRU_WIDTH
    row = lax.broadcasted_iota(jnp.int32, (LRU_SUB, w), 0)
    sp = _softplus(-lam_ref[...])
    for c in range(LRU_TILE // LRU_SUB):
        rows = slice(c * LRU_SUB, (c + 1) * LRU_SUB)
        xb = xy_ref[rows, 0:w]
        yb = xy_ref[rows, w:2 * w]
        prev = prev_ref[...]
        xc = xb * cw_ref[LRU_CONV - 1:LRU_CONV, :] + cb_ref[...]
        for k in range(1, LRU_CONV):
            shifted = jnp.where(row >= k, pltpu.roll(xb, k, 0), pltpu.roll(prev, k, 0))
            xc = xc + shifted * cw_ref[LRU_CONV - 1 - k:LRU_CONV - k, :]
        prev_ref[...] = xb
        gates = _dot(xc.astype(BF16), wg_ref[...]) + bg_ref[...]
        r = _sigmoid(gates[:, 0:w])
        i = _sigmoid(gates[:, w:2 * w])
        log_a = -LRU_C * r * sp
        a = jnp.exp(log_a)
        b = jnp.sqrt(1.0 - jnp.exp(2.0 * log_a)) * (i * xc)
        s = 1
        while s < LRU_SUB:
            keep = row >= s
            a_s = jnp.where(keep, pltpu.roll(a, s, 0), 1.0)
            b_s = jnp.where(keep, pltpu.roll(b, s, 0), 0.0)
            b = a * b_s + b
            a = a * a_s
            s *= 2
        h = b + a * h_ref[...]
        h_ref[...] = h[LRU_SUB - 1:LRU_SUB, :]
        y = h * jax.nn.gelu(yb, approximate=True)
        o_ref[rows, :] = _rms(y, nw_ref[...]).astype(o_ref.dtype)


def _rglru(xy, bsz, seq, cw, cb, wg, bg, lam, nw):
    nt = seq // LRU_TILE
    w = LRU_WIDTH
    vec = lambda a: pl.BlockSpec(a.shape, lambda b, t: (0, 0))
    return pl.pallas_call(
        _rglru_kernel,
        grid=(bsz, nt),
        in_specs=[pl.BlockSpec((LRU_TILE, 2 * w), lambda b, t: (b * nt + t, 0)),
                  vec(cw), vec(cb), vec(wg), vec(bg), vec(lam), vec(nw)],
        out_specs=pl.BlockSpec((LRU_TILE, w), lambda b, t: (b * nt + t, 0)),
        out_shape=jax.ShapeDtypeStruct((bsz * seq, w), BF16),
        scratch_shapes=[pltpu.VMEM((LRU_SUB, w), F32), pltpu.VMEM((1, w), F32)],
        compiler_params=_params("parallel", "arbitrary"),
        name="rglru",
    )(xy, cw, cb, wg, bg, lam, nw)


def _hgrn2_kernel(layer, z_ref, lbp_ref, nw_ref, o_ref, state_ref):
    @pl.when(pl.program_id(1) == 0)
    def _():
        state_ref[...] = jnp.zeros_like(state_ref)

    n = HG_HEADS * HG_DK
    L = HG_CHUNK
    lbp = lbp_ref[...]
    e = jnp.exp(lbp - jnp.max(lbp, axis=0, keepdims=True))
    soft = e / jnp.sum(e, axis=0, keepdims=True)
    lb = jnp.sum(soft[0:layer + 1], axis=0, keepdims=True) - soft[0:1]
    log_lb = jnp.log(jnp.maximum(lb, TINY))
    log_1m = jnp.log1p(-lb)

    lane_h = lax.broadcasted_iota(jnp.int32, (1, n), 1) // HG_DK
    head_of_row = lax.broadcasted_iota(jnp.int32, (n, n), 0) // HG_DK
    head_of_col = lax.broadcasted_iota(jnp.int32, (n, n), 1) // HG_DK
    same_head = head_of_row == head_of_col
    ones_bd = jnp.where(same_head, 1.0, 0.0).astype(BF16)
    row = lax.broadcasted_iota(jnp.int32, (L, n), 0)
    sub = lax.broadcasted_iota(jnp.int32, (1, SUBLANES, 1), 1)
    t_idx = lax.broadcasted_iota(jnp.int32, (L, L), 0)
    s_idx = lax.broadcasted_iota(jnp.int32, (L, L), 1)

    def chunk(ci, carry):
        r0 = pl.multiple_of(ci * L, L)
        q = z_ref[pl.ds(r0, L), 0:n]
        f = z_ref[pl.ds(r0, L), n:2 * n]
        v = z_ref[pl.ds(r0, L), 2 * n:3 * n]
        g = z_ref[pl.ds(r0, L), 3 * n:4 * n]
        lsig = -_softplus(-f)
        t1 = log_1m + lsig
        log_f = jnp.maximum(log_lb, t1) + jnp.log1p(jnp.exp(-jnp.abs(log_lb - t1)))
        kk = (1.0 - lb) * _sigmoid(-f)
        qs = q * _sigmoid(q)
        b = log_f
        s = 1
        while s < L:
            b = b + jnp.where(row >= s, pltpu.roll(b, s, 0), 0.0)
            s *= 2

        g8 = L // SUBLANES
        q3 = qs.reshape(g8, SUBLANES, n)
        k3 = kk.reshape(g8, SUBLANES, n)
        b3 = b.reshape(g8, SUBLANES, n)
        v3 = v.reshape(g8, SUBLANES, n)
        o = jnp.zeros((L, n), F32)
        for j in range(SUBLANES):
            d = jnp.where(sub >= j, b3 - b3[:, j:j + 1, :], NEG)
            xj = q3 * k3[:, j:j + 1, :] * jnp.exp(d)
            aj = _dot(xj.reshape(L, n).astype(BF16), ones_bd)
            o = o + (aj.reshape(g8, SUBLANES, n) * v3[:, j:j + 1, :]).reshape(L, n)

        att = [jnp.zeros((L, L), F32) for _ in range(HG_HEADS)]
        c = SUBLANES
        while c < L:
            bc = b.reshape(L // c, c, n)
            bend = jnp.broadcast_to(bc[:, c - 1:c, :], (L // c, c, n)).reshape(L, n)
            bprev = jnp.where(row >= c, pltpu.roll(bend, c, 0), 0.0)
            qt = qs * jnp.exp(b - bprev)
            kt = (kk * jnp.exp(bend - b)).astype(BF16)
            pair = ((t_idx // c) % 2 == 1) & ((s_idx // c) == (t_idx // c) - 1)
            for h in range(HG_HEADS):
                sc = _dot_t(jnp.where(lane_h == h, qt, 0.0).astype(BF16), kt)
                att[h] = att[h] + jnp.where(pair, sc, 0.0)
            c *= 2
        att_cat = jnp.concatenate([a.astype(BF16) for a in att], axis=1)
        v_stack = jnp.concatenate([jnp.where(lane_h == h, v, 0.0).astype(BF16)
                                   for h in range(HG_HEADS)], axis=0)
        o = o + _dot(att_cat, v_stack)

        state = state_ref[...]
        o = o + _dot_t((qs * jnp.exp(b)).astype(BF16), state.astype(BF16))
        b_last = b[L - 1:L, :]
        kdec = (kk * jnp.exp(b_last - b)).astype(BF16)
        upd = lax.dot_general(v.astype(BF16), kdec, (((0,), (0,)), ((), ())),
                              preferred_element_type=F32)
        state_ref[...] = state * jnp.exp(b_last) + jnp.where(same_head, upd, 0.0)

        ms = jnp.dot(o * o, jnp.where(same_head, 1.0 / HG_DK, 0.0), precision=HIGHEST,
                     preferred_element_type=F32)
        y = o * lax.rsqrt(ms + EPS) * nw_ref[...] * (g * _sigmoid(g))
        o_ref[pl.ds(r0, L), :] = y.astype(o_ref.dtype)
        return carry

    lax.fori_loop(0, HG_TILE // L, chunk, 0)


def _hgrn2(z, bsz, seq, lbp, nw, layer):
    nt = seq // HG_TILE
    n = HG_HEADS * HG_DK
    return pl.pallas_call(
        functools.partial(_hgrn2_kernel, layer),
        grid=(bsz, nt),
        in_specs=[pl.BlockSpec((HG_TILE, 4 * n), lambda b, t: (b * nt + t, 0)),
                  pl.BlockSpec(lbp.shape, lambda b, t: (0, 0)),
                  pl.BlockSpec(nw.shape, lambda b, t: (0, 0))],
        out_specs=pl.BlockSpec((HG_TILE, n), lambda b, t: (b * nt + t, 0)),
        out_shape=jax.ShapeDtypeStruct((bsz * seq, n), BF16),
        scratch_shapes=[pltpu.VMEM((n, n), F32)],
        compiler_params=_params("parallel", "arbitrary"),
        name="hgrn2",
    )(z, lbp, nw)


def _moba_kernel(nblk, q_ref, kt_ref, v_ref, km_ref, kxt_ref, o_ref, qa_ref, sa_ref, sb_ref,
                 m_ref, acc_ref):
    j = pl.program_id(2)
    hp = pl.program_id(1)
    blk = MOBA_BLOCK
    hd = ATT_HD
    pair = 2 * hd
    span = MOBA_GROUP * blk
    scale = 1.0 / math.sqrt(hd)
    lane = lax.broadcasted_iota(jnp.int32, (1, pair), 1)
    n_t = lax.broadcasted_iota(jnp.int32, (pair, blk), 0)
    t_row = lax.broadcasted_iota(jnp.int32, (pair, blk), 1)
    causal = (lax.broadcasted_iota(jnp.int32, (blk, blk), 0)
              >= lax.broadcasted_iota(jnp.int32, (blk, blk), 1))
    q = q_ref[...]
    km = jnp.concatenate([km_ref[...], jnp.zeros((pair - nblk, pair), F32)], axis=0)
    j0 = pl.multiple_of(j * blk, blk)
    jf = j.astype(F32)

    def keys(start, size):
        return jnp.concatenate([kt_ref[:, pl.ds(start, size)], kxt_ref[:, pl.ds(start, size)]], axis=0)

    k_own = keys(j0, blk)
    v_own = v_ref[pl.ds(j0, blk), :]
    mine = [(lane // hd) == h for h in range(2)]
    for h in range(2):
        slope = jnp.exp2(-8.0 * (2 * hp + h + 1).astype(F32) / ATT_HEADS)
        qm = jnp.where(mine[h], q, 0.0)
        gate = _dot_t(km, qm, precision=HIGHEST)
        gate = jnp.where(n_t < j, gate, NEG)
        chosen = jnp.zeros((pair, blk), F32)
        for r in range(MOBA_TOPK):
            top = jnp.max(gate, axis=0, keepdims=True)
            first = jnp.min(jnp.where(gate == top, n_t, pair), axis=0, keepdims=True)
            pick = n_t == first
            chosen = jnp.where(pick, jnp.where(j > r, 1.0, 0.0), chosen)
            gate = jnp.where(pick, -jnp.inf, gate)
        bias = jnp.where(n_t == nblk, -slope * t_row.astype(F32),
                         jnp.where(n_t == nblk + 1, slope,
                                   jnp.where(n_t == nblk + 2, slope * blk,
                                             jnp.where(n_t == nblk + 3, -slope * blk * jf, 0.0))))
        past = jnp.where(n_t < nblk, jnp.where(chosen > 0.0, 0.0, NEG), bias)
        own = jnp.where(n_t < nblk, jnp.where(n_t == j, 0.0, NEG), bias)
        qs = (qm * scale).astype(BF16)
        qa_ref[h] = jnp.concatenate([qs, past.T.astype(BF16)], axis=1)
        s = _dot(jnp.concatenate([qs, own.T.astype(BF16)], axis=1), k_own)
        s = jnp.where(causal, s, NEG)
        m = jnp.max(s, axis=1, keepdims=True)
        p = jnp.exp((s - m).astype(BF16))
        m_ref[h] = m
        acc_ref[h] = _dot(p, v_own[:, h * pair:(h + 1) * pair])

    def score(g, s_ref):
        k_g = keys(pl.multiple_of(g * span, span), span)
        for h in range(2):
            s_ref[h] = _dot(qa_ref[h], k_g)

    def attend(g, s_ref):
        v_g = v_ref[pl.ds(pl.multiple_of(g * span, span), span), :]
        for h in range(2):
            s = s_ref[h]
            m_old = m_ref[h]
            m_new = jnp.maximum(m_old, jnp.max(s, axis=1, keepdims=True))
            p = jnp.exp((s - m_new).astype(BF16))
            m_ref[h] = m_new
            pv = _dot(p, v_g[:, h * pair:(h + 1) * pair])
            acc_ref[h] = jnp.exp(m_old - m_new) * acc_ref[h] + pv

    last = nblk // MOBA_GROUP - 1
    nsteps = (j + 2 * MOBA_GROUP - 1) // (2 * MOBA_GROUP)

    @pl.when(nsteps > 0)
    def _():
        score(0, sa_ref)

    def body(i, carry):
        score(2 * i + 1, sb_ref)
        attend(2 * i, sa_ref)
        score(jnp.minimum(2 * i + 2, last), sa_ref)
        attend(2 * i + 1, sb_ref)
        return carry

    lax.fori_loop(0, nsteps, body, 0)
    out = []
    for h in range(2):
        acc = acc_ref[h]
        out.append(acc / pltpu.roll(acc, hd, 1))
    o_ref[...] = jnp.where(mine[0], out[0], out[1])


def _moba(q, kt, v, km, bsz, seq):
    nblk = seq // MOBA_BLOCK
    blk = MOBA_BLOCK
    pair = 2 * ATT_HD
    npair = ATT_HEADS // 2
    assert nblk + 4 <= pair and nblk % (2 * MOBA_GROUP) == 0
    pos = jnp.arange(seq, dtype=jnp.int32)[None, :]
    row = jnp.arange(pair, dtype=jnp.int32)[:, None]
    extras = [jnp.ones_like(pos), pos % blk, pos // blk, jnp.ones_like(pos)]
    kxt = (pos // blk == row).astype(F32)
    for i, e in enumerate(extras):
        kxt = jnp.where(row == nblk + i, e.astype(F32), kxt)
    kxt = kxt.astype(BF16)
    return pl.pallas_call(
        functools.partial(_moba_kernel, nblk),
        grid=(bsz, npair, nblk),
        in_specs=[pl.BlockSpec((blk, pair), lambda b, p, j: (b * nblk + j, p)),
                  pl.BlockSpec((pair, seq), lambda b, p, j: (p, b)),
                  pl.BlockSpec((seq, 2 * pair), lambda b, p, j: (b, p)),
                  pl.BlockSpec((nblk, pair), lambda b, p, j: (b, p)),
                  pl.BlockSpec((pair, seq), lambda b, p, j: (0, 0))],
        out_specs=pl.BlockSpec((blk, pair), lambda b, p, j: (b * nblk + j, p)),
        out_shape=jax.ShapeDtypeStruct((bsz * seq, ATT_HEADS * ATT_HD), F32),
        scratch_shapes=[pltpu.VMEM((2, blk, 2 * pair), BF16),
                        pltpu.VMEM((2, blk, MOBA_GROUP * blk), F32),
                        pltpu.VMEM((2, blk, MOBA_GROUP * blk), F32),
                        pltpu.VMEM((2, blk, 1), F32),
                        pltpu.VMEM((2, blk, pair), F32)],
        compiler_params=_params("parallel", "parallel", "arbitrary"),
        name="moba",
    )(q, kt, v, km, kxt)


def _out_mlp_kernel(final, x_ref, ya_ref, yb_ref, yc_ref, an_ref, wo_ref, gm_ref, w1_ref, w2_ref,
                    gf_ref, o_ref):
    yc = _rms(yc_ref[...], an_ref[...]).astype(BF16)
    y = jnp.concatenate([ya_ref[...], yb_ref[...], yc], axis=1)
    x = x_ref[...] + _dot(y, wo_ref[...])
    hb = _rms(x, gm_ref[...]).astype(BF16)
    mlp = None
    for c in range(w1_ref.shape[1] // FF_CHUNK):
        cols = slice(c * FF_CHUNK, (c + 1) * FF_CHUNK)
        u = jnp.square(jnp.maximum(_dot(hb, w1_ref[:, cols]), 0.0))
        part = _dot(u.astype(BF16), w2_ref[cols, :])
        mlp = part if mlp is None else mlp + part
    x = x + mlp
    if final:
        x = _rms(x, gf_ref[...])
    o_ref[...] = x


def _out_mlp(x2, ya, yb, yc, an, wo, gm, w1, w2, gf, final):
    t, d = x2.shape
    row = lambda a: pl.BlockSpec((ROW_TILE, a.shape[1]), lambda i: (i, 0))
    full = lambda a: pl.BlockSpec(a.shape, lambda i: (0, 0))
    return pl.pallas_call(
        functools.partial(_out_mlp_kernel, final),
        grid=(t // ROW_TILE,),
        in_specs=[row(x2), row(ya), row(yb), row(yc), full(an), full(wo), full(gm), full(w1),
                  full(w2), full(gf)],
        out_specs=pl.BlockSpec((ROW_TILE, d), lambda i: (i, 0)),
        out_shape=jax.ShapeDtypeStruct((t, d), F32),
        compiler_params=_params("parallel"),
        name="out_mlp",
    )(x2, ya, yb, yc, an, wo, gm, w1, w2, gf)


def _block_diag(w):
    g, n, _ = w.shape
    eye = jnp.eye(g, dtype=w.dtype)
    return (eye[:, None, :, None] * w[:, :, None, :]).reshape(g * n, g * n)


def kernel(x, w_in, w_out, norm_mix, norm_mlp, lru_conv_w, lru_conv_b, lru_wa, lru_ba, lru_wx, lru_bx,
           lru_lambda, hg_lower_bounds, hg_norm_w, lru_out_norm, att_out_norm, w_ff1, w_ff2, norm_final):
    bsz, seq, d = x.shape
    depth = w_in.shape[0]
    x2 = x.reshape(bsz * seq, d)
    r2 = lambda a: a.reshape(1, -1)
    for l in range(depth):
        lru, hg, cq, ckt, cv, km = _in_proj(x2, r2(norm_mix[l]), w_in[l].astype(BF16))
        wg = jnp.concatenate([_block_diag(lru_wa[l]), _block_diag(lru_wx[l])], axis=1).astype(BF16)
        bg = jnp.concatenate([lru_ba[l], lru_bx[l]]).reshape(1, -1)
        ya = _rglru(lru, bsz, seq, lru_conv_w[l], r2(lru_conv_b[l]), wg, bg, r2(lru_lambda[l]),
                    r2(lru_out_norm[l]))
        yb = _hgrn2(hg, bsz, seq, hg_lower_bounds, r2(jnp.tile(hg_norm_w[l], HG_HEADS)), l)
        yc = _moba(cq, ckt, cv, km.reshape(-1, km.shape[-1]), bsz, seq)
        x2 = _out_mlp(x2, ya, yb, yc, r2(att_out_norm[l]), w_out[l].astype(BF16), r2(norm_mlp[l]),
                      w_ff1[l].astype(BF16), w_ff2[l].astype(BF16), r2(norm_final), l == depth - 1)
    return x2.reshape(bsz, seq, d)
```

```python
import functools
import math

import jax
import jax.numpy as jnp
from jax import lax
from jax.experimental import pallas as pl
from jax.experimental.pallas import tpu as pltpu

F32 = jnp.float32
BF16 = jnp.bfloat16
HIGHEST = lax.Precision.HIGHEST

EPS = 1e-6
NEG = -1e30
TINY = 1e-30

LRU_WIDTH = 256
LRU_BLOCKS = 4
LRU_CONV = 4
LRU_C = 8.0
HG_HEADS = 4
HG_DK = 64
ATT_HEADS = 8
ATT_HD = 64
MOBA_BLOCK = 256
MOBA_TOPK = 3

SUBLANES = 8
VMEM_LIMIT = 56 * 1024 * 1024

ROW_TILE = 512
LRU_TILE = 1024
LRU_SUB = 256
HG_TILE = 512
HG_CHUNK = 128
FF_CHUNK = 1024
MOBA_GROUP = 2


def _dot(a, b):
    return jnp.dot(a, b, preferred_element_type=F32)


def _dot_t(a, b, precision=None):
    return lax.dot_general(a, b, (((1,), (1,)), ((), ())), precision=precision,
                           preferred_element_type=F32)


def _rms(x, g):
    return x * lax.rsqrt(jnp.mean(x * x, axis=-1, keepdims=True) + EPS) * g


def _sigmoid(x):
    return 1.0 / (1.0 + jnp.exp(-x))


def _softplus(x):
    return jnp.maximum(x, 0.0) + jnp.log1p(jnp.exp(-jnp.abs(x)))


def _params(*sem, flags=None):
    return pltpu.CompilerParams(dimension_semantics=sem, vmem_limit_bytes=VMEM_LIMIT, flags=flags)


def _in_proj_kernel(x_ref, g_ref, w_ref, wt_ref, lru_ref, hg_ref, qt_ref, k_ref, vt_ref, km_ref):
    hb = _rms(x_ref[...], g_ref[...]).astype(BF16)
    lru_ref[...] = _dot(hb, w_ref[:, 0:512])
    hg_ref[...] = _dot(hb, w_ref[:, 512:1536])
    qt_ref[...] = _dot_t(wt_ref[0:512, :], hb)
    k = _dot(hb, w_ref[:, 2048:2560])
    k_ref[...] = k.astype(BF16)
    for i in range(ROW_TILE // MOBA_BLOCK):
        km_ref[i] = jnp.mean(k[i * MOBA_BLOCK:(i + 1) * MOBA_BLOCK], axis=0, keepdims=True)
    vt = _dot_t(wt_ref[512:1024, :], hb).astype(BF16)
    ones = jnp.ones((ATT_HD, ROW_TILE), BF16)
    for h in range(ATT_HEADS):
        vt_ref[2 * h * ATT_HD:(2 * h + 1) * ATT_HD, :] = vt[h * ATT_HD:(h + 1) * ATT_HD, :]
        vt_ref[(2 * h + 1) * ATT_HD:(2 * h + 2) * ATT_HD, :] = ones


def _in_proj(x2, g, w, wt):
    t, d = x2.shape
    nblk = t // MOBA_BLOCK
    row = lambda width: pl.BlockSpec((ROW_TILE, width), lambda i: (i, 0))
    col = lambda height: pl.BlockSpec((height, ROW_TILE), lambda i: (0, i))
    return pl.pallas_call(
        _in_proj_kernel,
        grid=(t // ROW_TILE,),
        in_specs=[row(d), pl.BlockSpec((1, d), lambda i: (0, 0)),
                  pl.BlockSpec(w.shape, lambda i: (0, 0)), pl.BlockSpec(wt.shape, lambda i: (0, 0))],
        out_specs=[row(512), row(1024), col(512), row(512), col(1024),
                   pl.BlockSpec((ROW_TILE // MOBA_BLOCK, 1, 512), lambda i: (i, 0, 0))],
        out_shape=[jax.ShapeDtypeStruct((t, 512), F32), jax.ShapeDtypeStruct((t, 1024), F32),
                   jax.ShapeDtypeStruct((512, t), F32), jax.ShapeDtypeStruct((t, 512), BF16),
                   jax.ShapeDtypeStruct((1024, t), BF16), jax.ShapeDtypeStruct((nblk, 1, 512), F32)],
        compiler_params=_params("parallel"),
        name="in_proj",
    )(x2, g, w, wt)


def _rglru_kernel(xy_ref, cw_ref, cb_ref, wg_ref, bg_ref, lam_ref, nw_ref, o_ref, prev_ref, h_ref):
    @pl.when(pl.program_id(1) == 0)
    def _():
        prev_ref[...] = jnp.zeros_like(prev_ref)
        h_ref[...] = jnp.zeros_like(h_ref)

    w = LRU_WIDTH
    row = lax.broadcasted_iota(jnp.int32, (LRU_SUB, w), 0)
    sp = _softplus(-lam_ref[...])
    for c in range(LRU_TILE // LRU_SUB):
        rows = slice(c * LRU_SUB, (c + 1) * LRU_SUB)
        xb = xy_ref[rows, 0:w]
        yb = xy_ref[rows, w:2 * w]
        prev = prev_ref[...]
        xc = xb * cw_ref[LRU_CONV - 1:LRU_CONV, :] + cb_ref[...]
        for k in range(1, LRU_CONV):
            shifted = jnp.where(row >= k, pltpu.roll(xb, k, 0), pltpu.roll(prev, k, 0))
            xc = xc + shifted * cw_ref[LRU_CONV - 1 - k:LRU_CONV - k, :]
        prev_ref[...] = xb
        gates = _dot(xc.astype(BF16), wg_ref[...]) + bg_ref[...]
        r = _sigmoid(gates[:, 0:w])
        i = _sigmoid(gates[:, w:2 * w])
        log_a = -LRU_C * r * sp
        a = jnp.exp(log_a)
        b = jnp.sqrt(1.0 - jnp.exp(2.0 * log_a)) * (i * xc)
        s = 1
        while s < LRU_SUB:
            keep = row >= s
            a_s = jnp.where(keep, pltpu.roll(a, s, 0), 1.0)
            b_s = jnp.where(keep, pltpu.roll(b, s, 0), 0.0)
            b = a * b_s + b
            a = a * a_s
            s *= 2
        h = b + a * h_ref[...]
        h_ref[...] = h[LRU_SUB - 1:LRU_SUB, :]
        y = h * jax.nn.gelu(yb, approximate=True)
        o_ref[rows, :] = _rms(y, nw_ref[...]).astype(o_ref.dtype)


def _rglru(xy, bsz, seq, cw, cb, wg, bg, lam, nw):
    nt = seq // LRU_TILE
    w = LRU_WIDTH
    vec = lambda a: pl.BlockSpec(a.shape, lambda b, t: (0, 0))
    return pl.pallas_call(
        _rglru_kernel,
        grid=(bsz, nt),
        in_specs=[pl.BlockSpec((LRU_TILE, 2 * w), lambda b, t: (b * nt + t, 0)),
                  vec(cw), vec(cb), vec(wg), vec(bg), vec(lam), vec(nw)],
        out_specs=pl.BlockSpec((LRU_TILE, w), lambda b, t: (b * nt + t, 0)),
        out_shape=jax.ShapeDtypeStruct((bsz * seq, w), BF16),
        scratch_shapes=[pltpu.VMEM((LRU_SUB, w), F32), pltpu.VMEM((1, w), F32)],
        compiler_params=_params("parallel", "arbitrary"),
        name="rglru",
    )(xy, cw, cb, wg, bg, lam, nw)


def _hgrn2_kernel(layer, z_ref, lbp_ref, nw_ref, o_ref, state_ref):
    @pl.when(pl.program_id(1) == 0)
    def _():
        state_ref[...] = jnp.zeros_like(state_ref)

    n = HG_HEADS * HG_DK
    L = HG_CHUNK
    lbp = lbp_ref[...]
    e = jnp.exp(lbp - jnp.max(lbp, axis=0, keepdims=True))
    soft = e / jnp.sum(e, axis=0, keepdims=True)
    lb = jnp.sum(soft[0:layer + 1], axis=0, keepdims=True) - soft[0:1]
    log_lb = jnp.log(jnp.maximum(lb, TINY))
    log_1m = jnp.log1p(-lb)

    lane_h = lax.broadcasted_iota(jnp.int32, (1, n), 1) // HG_DK
    head_of_row = lax.broadcasted_iota(jnp.int32, (n, n), 0) // HG_DK
    head_of_col = lax.broadcasted_iota(jnp.int32, (n, n), 1) // HG_DK
    same_head = head_of_row == head_of_col
    ones_bd = jnp.where(same_head, 1.0, 0.0).astype(BF16)
    row = lax.broadcasted_iota(jnp.int32, (L, n), 0)
    sub = lax.broadcasted_iota(jnp.int32, (1, SUBLANES, 1), 1)
    t_idx = lax.broadcasted_iota(jnp.int32, (L, L), 0)
    s_idx = lax.broadcasted_iota(jnp.int32, (L, L), 1)

    def chunk(ci, carry):
        r0 = pl.multiple_of(ci * L, L)
        q = z_ref[pl.ds(r0, L), 0:n]
        f = z_ref[pl.ds(r0, L), n:2 * n]
        v = z_ref[pl.ds(r0, L), 2 * n:3 * n]
        g = z_ref[pl.ds(r0, L), 3 * n:4 * n]
        lsig = -_softplus(-f)
        t1 = log_1m + lsig
        log_f = jnp.maximum(log_lb, t1) + jnp.log1p(jnp.exp(-jnp.abs(log_lb - t1)))
        kk = (1.0 - lb) * _sigmoid(-f)
        qs = q * _sigmoid(q)
        b = log_f
        s = 1
        while s < L:
            b = b + jnp.where(row >= s, pltpu.roll(b, s, 0), 0.0)
            s *= 2

        g8 = L // SUBLANES
        q3 = qs.reshape(g8, SUBLANES, n)
        k3 = kk.reshape(g8, SUBLANES, n)
        b3 = b.reshape(g8, SUBLANES, n)
        v3 = v.reshape(g8, SUBLANES, n)
        o = jnp.zeros((L, n), F32)
        for j in range(SUBLANES):
            d = jnp.where(sub >= j, b3 - b3[:, j:j + 1, :], NEG)
            xj = q3 * k3[:, j:j + 1, :] * jnp.exp(d)
            aj = _dot(xj.reshape(L, n).astype(BF16), ones_bd)
            o = o + (aj.reshape(g8, SUBLANES, n) * v3[:, j:j + 1, :]).reshape(L, n)

        att = [jnp.zeros((L, L), F32) for _ in range(HG_HEADS)]
        c = SUBLANES
        while c < L:
            bc = b.reshape(L // c, c, n)
            bend = jnp.broadcast_to(bc[:, c - 1:c, :], (L // c, c, n)).reshape(L, n)
            bprev = jnp.where(row >= c, pltpu.roll(bend, c, 0), 0.0)
            qt = qs * jnp.exp(b - bprev)
            kt = (kk * jnp.exp(bend - b)).astype(BF16)
            pair = ((t_idx // c) % 2 == 1) & ((s_idx // c) == (t_idx // c) - 1)
            for h in range(HG_HEADS):
                sc = _dot_t(jnp.where(lane_h == h, qt, 0.0).astype(BF16), kt)
                att[h] = att[h] + jnp.where(pair, sc, 0.0)
            c *= 2
        att_cat = jnp.concatenate([a.astype(BF16) for a in att], axis=1)
        v_stack = jnp.concatenate([jnp.where(lane_h == h, v, 0.0).astype(BF16)
                                   for h in range(HG_HEADS)], axis=0)
        o = o + _dot(att_cat, v_stack)

        state = state_ref[...]
        o = o + _dot_t((qs * jnp.exp(b)).astype(BF16), state.astype(BF16))
        b_last = b[L - 1:L, :]
        kdec = (kk * jnp.exp(b_last - b)).astype(BF16)
        upd = lax.dot_general(v.astype(BF16), kdec, (((0,), (0,)), ((), ())),
                              preferred_element_type=F32)
        state_ref[...] = state * jnp.exp(b_last) + jnp.where(same_head, upd, 0.0)

        ms = jnp.dot(o * o, jnp.where(same_head, 1.0 / HG_DK, 0.0), precision=HIGHEST,
                     preferred_element_type=F32)
        y = o * lax.rsqrt(ms + EPS) * nw_ref[...] * (g * _sigmoid(g))
        o_ref[pl.ds(r0, L), :] = y.astype(o_ref.dtype)
        return carry

    lax.fori_loop(0, HG_TILE // L, chunk, 0)


def _hgrn2(z, bsz, seq, lbp, nw, layer):
    nt = seq // HG_TILE
    n = HG_HEADS * HG_DK
    return pl.pallas_call(
        functools.partial(_hgrn2_kernel, layer),
        grid=(bsz, nt),
        in_specs=[pl.BlockSpec((HG_TILE, 4 * n), lambda b, t: (b * nt + t, 0)),
                  pl.BlockSpec(lbp.shape, lambda b, t: (0, 0)),
                  pl.BlockSpec(nw.shape, lambda b, t: (0, 0))],
        out_specs=pl.BlockSpec((HG_TILE, n), lambda b, t: (b * nt + t, 0)),
        out_shape=jax.ShapeDtypeStruct((bsz * seq, n), BF16),
        scratch_shapes=[pltpu.VMEM((n, n), F32)],
        compiler_params=_params("parallel", "arbitrary"),
        name="hgrn2",
    )(z, lbp, nw)


def _moba_kernel(nblk, qt_ref, k_ref, vt_ref, km_ref, kx_ref, o_ref, qa_ref, sa_ref, sb_ref,
                 m_ref, acc_ref):
    j = pl.program_id(2)
    hp = pl.program_id(1)
    blk = MOBA_BLOCK
    hd = ATT_HD
    pair = 2 * hd
    span = MOBA_GROUP * blk
    scale = 1.0 / math.sqrt(hd)
    feat = lax.broadcasted_iota(jnp.int32, (pair, 1), 0)
    n_t = lax.broadcasted_iota(jnp.int32, (nblk, blk), 0)
    b_row = lax.broadcasted_iota(jnp.int32, (SUBLANES, blk), 0)
    t_row = lax.broadcasted_iota(jnp.int32, (SUBLANES, blk), 1)
    causal = (lax.broadcasted_iota(jnp.int32, (blk, blk), 0)
              <= lax.broadcasted_iota(jnp.int32, (blk, blk), 1))
    qt = qt_ref[...]
    km = km_ref[...]
    j0 = pl.multiple_of(j * blk, blk)
    jf = j.astype(F32)

    def keys(start, size):
        return jnp.concatenate([k_ref[pl.ds(start, size), :], kx_ref[pl.ds(start, size), :]], axis=1)

    k_own = keys(j0, blk)
    vt_own = vt_ref[:, pl.ds(j0, blk)]
    pad = jnp.zeros((pair - nblk - SUBLANES, blk), F32)
    for h in range(2):
        slope = jnp.exp2(-8.0 * (2 * hp + h + 1).astype(F32) / ATT_HEADS)
        qm = jnp.where(feat // hd == h, qt, 0.0)
        gate = jnp.dot(km, qm, precision=HIGHEST, preferred_element_type=F32)
        gate = jnp.where(n_t < j, gate, NEG)
        chosen = jnp.zeros((nblk, blk), F32)
        for r in range(MOBA_TOPK):
            top = jnp.max(gate, axis=0, keepdims=True)
            first = jnp.min(jnp.where(gate == top, n_t, nblk), axis=0, keepdims=True)
            pick = n_t == first
            chosen = jnp.where(pick, jnp.where(j > r, 1.0, 0.0), chosen)
            gate = jnp.where(pick, -jnp.inf, gate)
        bias = jnp.where(b_row == 0, -slope * t_row.astype(F32),
                         jnp.where(b_row == 1, slope,
                                   jnp.where(b_row == 2, slope * blk,
                                             jnp.where(b_row == 3, -slope * blk * jf, 0.0))))
        qs = (qm * scale).astype(BF16)
        past = jnp.concatenate([jnp.where(chosen > 0.0, 0.0, NEG), bias, pad], axis=0)
        own = jnp.concatenate([jnp.where(n_t == j, 0.0, NEG), bias, pad], axis=0)
        qa_ref[h] = jnp.concatenate([qs, past.astype(BF16)], axis=0)
        s = _dot(k_own, jnp.concatenate([qs, own.astype(BF16)], axis=0))
        s = jnp.where(causal, s, NEG)
        m = jnp.max(s, axis=0, keepdims=True)
        p = jnp.exp((s - m).astype(BF16))
        m_ref[h] = m
        acc_ref[h] = _dot(vt_own[h * pair:(h + 1) * pair, :], p)

    def score(g, s_ref):
        k_g = keys(pl.multiple_of(g * span, span), span)
        for h in range(2):
            s_ref[h] = _dot(k_g, qa_ref[h])

    def attend(g, s_ref):
        vt_g = vt_ref[:, pl.ds(pl.multiple_of(g * span, span), span)]
        for h in range(2):
            s = s_ref[h]
            m_old = m_ref[h]
            m_new = jnp.maximum(m_old, jnp.max(s, axis=0, keepdims=True))
            p = jnp.exp((s - m_new).astype(BF16))
            m_ref[h] = m_new
            pv = _dot(vt_g[h * pair:(h + 1) * pair, :], p)
            acc_ref[h] = jnp.exp(m_old - m_new) * acc_ref[h] + pv

    last = nblk // MOBA_GROUP - 1
    nsteps = (j + 2 * MOBA_GROUP - 1) // (2 * MOBA_GROUP)

    @pl.when(nsteps > 0)
    def _():
        score(0, sa_ref)

    def body(i, carry):
        score(2 * i + 1, sb_ref)
        attend(2 * i, sa_ref)
        score(jnp.minimum(2 * i + 2, last), sa_ref)
        attend(2 * i + 1, sb_ref)
        return carry

    lax.fori_loop(0, nsteps, body, 0)
    out = []
    for h in range(2):
        acc = acc_ref[h]
        out.append(acc[0:hd] / acc[hd:hd + 1])
    o_ref[...] = jnp.concatenate(out, axis=0).T


def _moba(qt, k, vt, km, bsz, seq):
    nblk = seq // MOBA_BLOCK
    blk = MOBA_BLOCK
    pair = 2 * ATT_HD
    npair = ATT_HEADS // 2
    assert nblk + SUBLANES <= pair and nblk % (2 * MOBA_GROUP) == 0
    pos = jnp.arange(seq, dtype=jnp.int32)[:, None]
    col = jnp.arange(pair, dtype=jnp.int32)[None, :]
    extras = [jnp.ones_like(pos), pos % blk, pos // blk, jnp.ones_like(pos)]
    kx = (pos // blk == col).astype(F32)
    for i, e in enumerate(extras):
        kx = jnp.where(col == nblk + i, e.astype(F32), kx)
    kx = kx.astype(BF16)
    return pl.pallas_call(
        functools.partial(_moba_kernel, nblk),
        grid=(bsz, npair, nblk),
        in_specs=[pl.BlockSpec((pair, blk), lambda b, p, j: (p, b * nblk + j)),
                  pl.BlockSpec((seq, pair), lambda b, p, j: (b, p)),
                  pl.BlockSpec((2 * pair, seq), lambda b, p, j: (p, b)),
                  pl.BlockSpec((nblk, pair), lambda b, p, j: (b, p)),
                  pl.BlockSpec((seq, pair), lambda b, p, j: (0, 0))],
        out_specs=pl.BlockSpec((blk, pair), lambda b, p, j: (b * nblk + j, p)),
        out_shape=jax.ShapeDtypeStruct((bsz * seq, ATT_HEADS * ATT_HD), F32),
        scratch_shapes=[pltpu.VMEM((2, 2 * pair, blk), BF16),
                        pltpu.VMEM((2, MOBA_GROUP * blk, blk), F32),
                        pltpu.VMEM((2, MOBA_GROUP * blk, blk), F32),
                        pltpu.VMEM((2, 1, blk), F32),
                        pltpu.VMEM((2, pair, blk), F32)],
        compiler_params=_params("parallel", "parallel", "arbitrary"),
        name="moba",
    )(qt, k, vt, km, kx)


def _out_mlp_kernel(final, x_ref, ya_ref, yb_ref, yc_ref, an_ref, wo_ref, gm_ref, w1_ref, w2_ref,
                    gf_ref, o_ref):
    yc = _rms(yc_ref[...], an_ref[...]).astype(BF16)
    y = jnp.concatenate([ya_ref[...], yb_ref[...], yc], axis=1)
    x = x_ref[...] + _dot(y, wo_ref[...])
    hb = _rms(x, gm_ref[...]).astype(BF16)
    mlp = None
    for c in range(w1_ref.shape[1] // FF_CHUNK):
        cols = slice(c * FF_CHUNK, (c + 1) * FF_CHUNK)
        u = jnp.square(jnp.maximum(_dot(hb, w1_ref[:, cols]), 0.0))
        part = _dot(u.astype(BF16), w2_ref[cols, :])
        mlp = part if mlp is None else mlp + part
    x = x + mlp
    if final:
        x = _rms(x, gf_ref[...])
    o_ref[...] = x


def _out_mlp(x2, ya, yb, yc, an, wo, gm, w1, w2, gf, final):
    t, d = x2.shape
    row = lambda a: pl.BlockSpec((ROW_TILE, a.shape[1]), lambda i: (i, 0))
    full = lambda a: pl.BlockSpec(a.shape, lambda i: (0, 0))
    return pl.pallas_call(
        functools.partial(_out_mlp_kernel, final),
        grid=(t // ROW_TILE,),
        in_specs=[row(x2), row(ya), row(yb), row(yc), full(an), full(wo), full(gm), full(w1),
                  full(w2), full(gf)],
        out_specs=pl.BlockSpec((ROW_TILE, d), lambda i: (i, 0)),
        out_shape=jax.ShapeDtypeStruct((t, d), F32),
        compiler_params=_params("parallel"),
        name="out_mlp",
    )(x2, ya, yb, yc, an, wo, gm, w1, w2, gf)


def _block_diag(w):
    g, n, _ = w.shape
    eye = jnp.eye(g, dtype=w.dtype)
    return (eye[:, None, :, None] * w[:, :, None, :]).reshape(g * n, g * n)


def kernel(x, w_in, w_out, norm_mix, norm_mlp, lru_conv_w, lru_conv_b, lru_wa, lru_ba, lru_wx, lru_bx,
           lru_lambda, hg_lower_bounds, hg_norm_w, lru_out_norm, att_out_norm, w_ff1, w_ff2, norm_final):
    bsz, seq, d = x.shape
    depth = w_in.shape[0]
    x2 = x.reshape(bsz * seq, d)
    r2 = lambda a: a.reshape(1, -1)
    for l in range(depth):
        w = w_in[l].astype(BF16)
        wt = jnp.concatenate([w[:, 1536:2048], w[:, 2560:3072]], axis=1).T
        lru, hg, cqt, ck, cvt, km = _in_proj(x2, r2(norm_mix[l]), w, wt)
        wg = jnp.concatenate([_block_diag(lru_wa[l]), _block_diag(lru_wx[l])], axis=1).astype(BF16)
        bg = jnp.concatenate([lru_ba[l], lru_bx[l]]).reshape(1, -1)
        ya = _rglru(lru, bsz, seq, lru_conv_w[l], r2(lru_conv_b[l]), wg, bg, r2(lru_lambda[l]),
                    r2(lru_out_norm[l]))
        yb = _hgrn2(hg, bsz, seq, hg_lower_bounds, r2(jnp.tile(hg_norm_w[l], HG_HEADS)), l)
        yc = _moba(cqt, ck, cvt, km.reshape(-1, km.shape[-1]), bsz, seq)
        x2 = _out_mlp(x2, ya, yb, yc, r2(att_out_norm[l]), w_out[l].astype(BF16), r2(norm_mlp[l]),
                      w_ff1[l].astype(BF16), w_ff2[l].astype(BF16), r2(norm_final), l == depth - 1)
    return x2.reshape(bsz, seq, d)
```

```python
import functools
import math
import struct

import jax
import jax.numpy as jnp
from jax import lax
from jax.experimental import pallas as pl
from jax.experimental.pallas import tpu as pltpu

F32 = jnp.float32
BF16 = jnp.bfloat16
HIGHEST = lax.Precision.HIGHEST

EPS = 1e-6
NEG = -1e30
TINY = 1e-30

LRU_WIDTH = 256
LRU_BLOCKS = 4
LRU_CONV = 4
LRU_C = 8.0
HG_HEADS = 4
HG_DK = 64
ATT_HEADS = 8
ATT_HD = 64
MOBA_BLOCK = 256
MOBA_TOPK = 3

SUBLANES = 8
VMEM_LIMIT = 56 * 1024 * 1024

ROW_TILE = 512
LRU_TILE = 1024
LRU_SUB = 256
HG_TILE = 512
HG_CHUNK = 128
FF_CHUNK = 1024
MOBA_GROUP = 2
ONES_ROWS = 16
V_ROWS = ATT_HD + ONES_ROWS


def _bf16_pieces(x, n):
    pieces = []
    for _ in range(n):
        bits = struct.unpack("<I", struct.pack("<f", x))[0]
        bits = (bits + 0x7FFF + ((bits >> 16) & 1)) & 0xFFFF0000
        piece = struct.unpack("<f", struct.pack("<I", bits))[0]
        pieces.append(piece)
        x -= piece
    return tuple(pieces)


LOG2E = math.log2(math.e)
LOG2E_PIECES = _bf16_pieces(LOG2E, 3)


def _dot(a, b):
    return jnp.dot(a, b, preferred_element_type=F32)


def _dot_t(a, b, precision=None):
    return lax.dot_general(a, b, (((1,), (1,)), ((), ())), precision=precision,
                           preferred_element_type=F32)


def _rms(x, g):
    return x * lax.rsqrt(jnp.mean(x * x, axis=-1, keepdims=True) + EPS) * g


def _sigmoid(x):
    return 1.0 / (1.0 + jnp.exp(-x))


def _softplus(x):
    return jnp.maximum(x, 0.0) + jnp.log1p(jnp.exp(-jnp.abs(x)))


def _params(*sem, flags=None):
    return pltpu.CompilerParams(dimension_semantics=sem, vmem_limit_bytes=VMEM_LIMIT, flags=flags)


def _in_proj_kernel(x_ref, g_ref, w_ref, wt_ref, lru_ref, hg_ref, qt_ref, k_ref, vt_ref, km_ref):
    hb = _rms(x_ref[...], g_ref[...]).astype(BF16)
    lru_ref[...] = _dot(hb, w_ref[:, 0:512])
    hg_ref[...] = _dot(hb, w_ref[:, 512:1536])
    qt_ref[...] = _dot_t(wt_ref[0:512, :], hb)
    k = _dot(hb, w_ref[:, 2048:2560])
    k_ref[...] = k.astype(BF16)
    for i in range(ROW_TILE // MOBA_BLOCK):
        km_ref[i] = jnp.mean(k[i * MOBA_BLOCK:(i + 1) * MOBA_BLOCK], axis=0, keepdims=True)
    vt = _dot_t(wt_ref[512:1024, :], hb).astype(BF16)
    ones = jnp.ones((ONES_ROWS, ROW_TILE), BF16)
    for h in range(ATT_HEADS):
        vt_ref[h * V_ROWS:h * V_ROWS + ATT_HD, :] = vt[h * ATT_HD:(h + 1) * ATT_HD, :]
        vt_ref[h * V_ROWS + ATT_HD:(h + 1) * V_ROWS, :] = ones


def _in_proj(x2, g, w, wt):
    t, d = x2.shape
    nblk = t // MOBA_BLOCK
    row = lambda width: pl.BlockSpec((ROW_TILE, width), lambda i: (i, 0))
    col = lambda height: pl.BlockSpec((height, ROW_TILE), lambda i: (0, i))
    return pl.pallas_call(
        _in_proj_kernel,
        grid=(t // ROW_TILE,),
        in_specs=[row(d), pl.BlockSpec((1, d), lambda i: (0, 0)),
                  pl.BlockSpec(w.shape, lambda i: (0, 0)), pl.BlockSpec(wt.shape, lambda i: (0, 0))],
        out_specs=[row(512), row(1024), col(512), row(512), col(ATT_HEADS * V_ROWS),
                   pl.BlockSpec((ROW_TILE // MOBA_BLOCK, 1, 512), lambda i: (i, 0, 0))],
        out_shape=[jax.ShapeDtypeStruct((t, 512), F32), jax.ShapeDtypeStruct((t, 1024), F32),
                   jax.ShapeDtypeStruct((512, t), F32), jax.ShapeDtypeStruct((t, 512), BF16),
                   jax.ShapeDtypeStruct((ATT_HEADS * V_ROWS, t), BF16),
                   jax.ShapeDtypeStruct((nblk, 1, 512), F32)],
        compiler_params=_params("parallel"),
        name="in_proj",
    )(x2, g, w, wt)


def _rglru_kernel(xy_ref, cw_ref, cb_ref, wg_ref, bg_ref, lam_ref, nw_ref, o_ref, prev_ref, h_ref):
    @pl.when(pl.program_id(1) == 0)
    def _():
        prev_ref[...] = jnp.zeros_like(prev_ref)
        h_ref[...] = jnp.zeros_like(h_ref)

    w = LRU_WIDTH
    row = lax.broadcasted_iota(jnp.int32, (LRU_SUB, w), 0)
    sp = _softplus(-lam_ref[...])
    for c in range(LRU_TILE // LRU_SUB):
        rows = slice(c * LRU_SUB, (c + 1) * LRU_SUB)
        xb = xy_ref[rows, 0:w]
        yb = xy_ref[rows, w:2 * w]
        prev = prev_ref[...]
        xc = xb * cw_ref[LRU_CONV - 1:LRU_CONV, :] + cb_ref[...]
        for k in range(1, LRU_CONV):
            shifted = jnp.where(row >= k, pltpu.roll(xb, k, 0), pltpu.roll(prev, k, 0))
            xc = xc + shifted * cw_ref[LRU_CONV - 1 - k:LRU_CONV - k, :]
        prev_ref[...] = xb
        gates = _dot(xc.astype(BF16), wg_ref[...]) + bg_ref[...]
        r = _sigmoid(gates[:, 0:w])
        i = _sigmoid(gates[:, w:2 * w])
        log_a = -LRU_C * r * sp
        a = jnp.exp(log_a)
        b = jnp.sqrt(1.0 - jnp.exp(2.0 * log_a)) * (i * xc)
        s = 1
        while s < LRU_SUB:
            keep = row >= s
            a_s = jnp.where(keep, pltpu.roll(a, s, 0), 1.0)
            b_s = jnp.where(keep, pltpu.roll(b, s, 0), 0.0)
            b = a * b_s + b
            a = a * a_s
            s *= 2
        h = b + a * h_ref[...]
        h_ref[...] = h[LRU_SUB - 1:LRU_SUB, :]
        y = h * jax.nn.gelu(yb, approximate=True)
        o_ref[rows, :] = _rms(y, nw_ref[...]).astype(o_ref.dtype)


def _rglru(xy, bsz, seq, cw, cb, wg, bg, lam, nw):
    nt = seq // LRU_TILE
    w = LRU_WIDTH
    vec = lambda a: pl.BlockSpec(a.shape, lambda b, t: (0, 0))
    return pl.pallas_call(
        _rglru_kernel,
        grid=(bsz, nt),
        in_specs=[pl.BlockSpec((LRU_TILE, 2 * w), lambda b, t: (b * nt + t, 0)),
                  vec(cw), vec(cb), vec(wg), vec(bg), vec(lam), vec(nw)],
        out_specs=pl.BlockSpec((LRU_TILE, w), lambda b, t: (b * nt + t, 0)),
        out_shape=jax.ShapeDtypeStruct((bsz * seq, w), BF16),
        scratch_shapes=[pltpu.VMEM((LRU_SUB, w), F32), pltpu.VMEM((1, w), F32)],
        compiler_params=_params("parallel", "arbitrary"),
        name="rglru",
    )(xy, cw, cb, wg, bg, lam, nw)


def _hgrn2_kernel(layer, z_ref, lbp_ref, nw_ref, o_ref, state_ref):
    @pl.when(pl.program_id(1) == 0)
    def _():
        state_ref[...] = jnp.zeros_like(state_ref)

    n = HG_HEADS * HG_DK
    L = HG_CHUNK
    lbp = lbp_ref[...]
    e = jnp.exp(lbp - jnp.max(lbp, axis=0, keepdims=True))
    soft = e / jnp.sum(e, axis=0, keepdims=True)
    lb = jnp.sum(soft[0:layer + 1], axis=0, keepdims=True) - soft[0:1]
    log_lb = jnp.log(jnp.maximum(lb, TINY))
    log_1m = jnp.log1p(-lb)

    lane_h = lax.broadcasted_iota(jnp.int32, (1, n), 1) // HG_DK
    head_of_row = lax.broadcasted_iota(jnp.int32, (n, n), 0) // HG_DK
    head_of_col = lax.broadcasted_iota(jnp.int32, (n, n), 1) // HG_DK
    same_head = head_of_row == head_of_col
    ones_bd = jnp.where(same_head, 1.0, 0.0).astype(BF16)
    row = lax.broadcasted_iota(jnp.int32, (L, n), 0)
    sub = lax.broadcasted_iota(jnp.int32, (1, SUBLANES, 1), 1)
    t_idx = lax.broadcasted_iota(jnp.int32, (L, L), 0)
    s_idx = lax.broadcasted_iota(jnp.int32, (L, L), 1)

    def chunk(ci, carry):
        r0 = pl.multiple_of(ci * L, L)
        q = z_ref[pl.ds(r0, L), 0:n]
        f = z_ref[pl.ds(r0, L), n:2 * n]
        v = z_ref[pl.ds(r0, L), 2 * n:3 * n]
        g = z_ref[pl.ds(r0, L), 3 * n:4 * n]
        lsig = -_softplus(-f)
        t1 = log_1m + lsig
        log_f = jnp.maximum(log_lb, t1) + jnp.log1p(jnp.exp(-jnp.abs(log_lb - t1)))
        kk = (1.0 - lb) * _sigmoid(-f)
        qs = q * _sigmoid(q)
        b = log_f
        s = 1
        while s < L:
            b = b + jnp.where(row >= s, pltpu.roll(b, s, 0), 0.0)
            s *= 2

        g8 = L // SUBLANES
        q3 = qs.reshape(g8, SUBLANES, n)
        k3 = kk.reshape(g8, SUBLANES, n)
        b3 = b.reshape(g8, SUBLANES, n)
        v3 = v.reshape(g8, SUBLANES, n)
        o = jnp.zeros((L, n), F32)
        for j in range(SUBLANES):
            d = jnp.where(sub >= j, b3 - b3[:, j:j + 1, :], NEG)
            xj = q3 * k3[:, j:j + 1, :] * jnp.exp(d)
            aj = _dot(xj.reshape(L, n).astype(BF16), ones_bd)
            o = o + (aj.reshape(g8, SUBLANES, n) * v3[:, j:j + 1, :]).reshape(L, n)

        att = [jnp.zeros((L, L), F32) for _ in range(HG_HEADS)]
        c = SUBLANES
        while c < L:
            bc = b.reshape(L // c, c, n)
            bend = jnp.broadcast_to(bc[:, c - 1:c, :], (L // c, c, n)).reshape(L, n)
            bprev = jnp.where(row >= c, pltpu.roll(bend, c, 0), 0.0)
            qt = qs * jnp.exp(b - bprev)
            kt = (kk * jnp.exp(bend - b)).astype(BF16)
            pair = ((t_idx // c) % 2 == 1) & ((s_idx // c) == (t_idx // c) - 1)
            for h in range(HG_HEADS):
                sc = _dot_t(jnp.where(lane_h == h, qt, 0.0).astype(BF16), kt)
                att[h] = att[h] + jnp.where(pair, sc, 0.0)
            c *= 2
        att_cat = jnp.concatenate([a.astype(BF16) for a in att], axis=1)
        v_stack = jnp.concatenate([jnp.where(lane_h == h, v, 0.0).astype(BF16)
                                   for h in range(HG_HEADS)], axis=0)
        o = o + _dot(att_cat, v_stack)

        state = state_ref[...]
        o = o + _dot_t((qs * jnp.exp(b)).astype(BF16), state.astype(BF16))
        b_last = b[L - 1:L, :]
        kdec = (kk * jnp.exp(b_last - b)).astype(BF16)
        upd = lax.dot_general(v.astype(BF16), kdec, (((0,), (0,)), ((), ())),
                              preferred_element_type=F32)
        state_ref[...] = state * jnp.exp(b_last) + jnp.where(same_head, upd, 0.0)

        ms = jnp.dot(o * o, jnp.where(same_head, 1.0 / HG_DK, 0.0), precision=HIGHEST,
                     preferred_element_type=F32)
        y = o * lax.rsqrt(ms + EPS) * nw_ref[...] * (g * _sigmoid(g))
        o_ref[pl.ds(r0, L), :] = y.astype(o_ref.dtype)
        return carry

    lax.fori_loop(0, HG_TILE // L, chunk, 0)


def _hgrn2(z, bsz, seq, lbp, nw, layer):
    nt = seq // HG_TILE
    n = HG_HEADS * HG_DK
    return pl.pallas_call(
        functools.partial(_hgrn2_kernel, layer),
        grid=(bsz, nt),
        in_specs=[pl.BlockSpec((HG_TILE, 4 * n), lambda b, t: (b * nt + t, 0)),
                  pl.BlockSpec(lbp.shape, lambda b, t: (0, 0)),
                  pl.BlockSpec(nw.shape, lambda b, t: (0, 0))],
        out_specs=pl.BlockSpec((HG_TILE, n), lambda b, t: (b * nt + t, 0)),
        out_shape=jax.ShapeDtypeStruct((bsz * seq, n), BF16),
        scratch_shapes=[pltpu.VMEM((n, n), F32)],
        compiler_params=_params("parallel", "arbitrary"),
        name="hgrn2",
    )(z, lbp, nw)


def _moba_kernel(nblk, qt_ref, k_ref, vt_ref, km_ref, kx_ref, o_ref, qa_ref, sa_ref, sb_ref,
                 m_ref, acc_ref):
    j = pl.program_id(2)
    hp = pl.program_id(1)
    blk = MOBA_BLOCK
    hd = ATT_HD
    pair = 2 * hd
    span = MOBA_GROUP * blk
    scale = 1.0 / math.sqrt(hd)
    feat = lax.broadcasted_iota(jnp.int32, (pair, 1), 0)
    n_t = lax.broadcasted_iota(jnp.int32, (nblk, blk), 0)
    b_row = lax.broadcasted_iota(jnp.int32, (SUBLANES, blk), 0)
    causal = (lax.broadcasted_iota(jnp.int32, (blk, blk), 0)
              <= lax.broadcasted_iota(jnp.int32, (blk, blk), 1))
    qt = qt_ref[...]
    km = km_ref[...]
    j0 = pl.multiple_of(j * blk, blk)

    def keys(start, size):
        return jnp.concatenate([k_ref[pl.ds(start, size), :], kx_ref[pl.ds(start, size), :]], axis=1)

    k_own = keys(j0, blk)
    vt_own = vt_ref[:, pl.ds(j0, blk)]
    pad = jnp.zeros((pair - nblk - SUBLANES, blk), F32)
    q_own = []
    for h in range(2):
        slope = jnp.exp2(-8.0 * (2 * hp + h + 1).astype(F32) / ATT_HEADS)
        qm = jnp.where(feat // hd == h, qt, 0.0)
        gate = jnp.dot(km, qm, precision=HIGHEST, preferred_element_type=F32)
        gate = jnp.where(n_t < j, gate, NEG)
        chosen = jnp.zeros((nblk, blk), F32)
        for r in range(MOBA_TOPK):
            top = jnp.max(gate, axis=0, keepdims=True)
            first = jnp.min(jnp.where(gate == top, n_t, nblk), axis=0, keepdims=True)
            pick = n_t == first
            chosen = jnp.where(pick, jnp.where(j > r, 1.0, 0.0), chosen)
            gate = jnp.where(pick, -jnp.inf, gate)
        bias = jnp.zeros((SUBLANES, blk), F32)
        for i, piece in enumerate(LOG2E_PIECES):
            bias = jnp.where(b_row == i, slope * piece, bias)
            bias = jnp.where(b_row == len(LOG2E_PIECES) + i, slope * (piece * blk), bias)
        qs = (qm * (scale * LOG2E)).astype(BF16)
        past = jnp.concatenate([jnp.where(chosen > 0.0, 0.0, NEG), bias, pad], axis=0)
        own = jnp.concatenate([jnp.where(n_t == j, 0.0, NEG), bias, pad], axis=0)
        qa_ref[h] = jnp.concatenate([qs, past.astype(BF16)], axis=0)
        q_own.append(jnp.concatenate([qs, own.astype(BF16)], axis=0))

    def score(g, s_ref):
        k_g = keys(pl.multiple_of(g * span, span), span)
        for h in range(2):
            s_ref[h] = _dot(k_g, qa_ref[h])

    def attend(g, s_ref):
        vt_g = vt_ref[:, pl.ds(pl.multiple_of(g * span, span), span)]
        for h in range(2):
            s = s_ref[h]
            m_old = m_ref[h]
            m_new = jnp.maximum(m_old, jnp.max(s, axis=0, keepdims=True))
            p = jnp.exp2((s - m_new).astype(BF16))
            m_ref[h] = m_new
            pv = _dot(vt_g[h * V_ROWS:(h + 1) * V_ROWS, :], p)
            acc_ref[h] = jnp.exp2(m_old - m_new) * acc_ref[h] + pv

    last = nblk // MOBA_GROUP - 1
    nsteps = (j + 2 * MOBA_GROUP - 1) // (2 * MOBA_GROUP)
    score(0, sa_ref)

    for h in range(2):
        s = jnp.where(causal, _dot(k_own, q_own[h]), NEG)
        m = jnp.max(s, axis=0, keepdims=True)
        p = jnp.exp2((s - m).astype(BF16))
        m_ref[h] = m
        acc_ref[h] = _dot(vt_own[h * V_ROWS:(h + 1) * V_ROWS, :], p)

    def body(i, carry):
        score(2 * i + 1, sb_ref)
        attend(2 * i, sa_ref)
        score(jnp.minimum(2 * i + 2, last), sa_ref)
        attend(2 * i + 1, sb_ref)
        return carry

    lax.fori_loop(0, nsteps, body, 0)
    out = []
    for h in range(2):
        acc = acc_ref[h]
        out.append(acc[0:hd] / acc[hd:hd + 1])
    o_ref[...] = jnp.concatenate(out, axis=0).T


def _moba(qt, k, vt, km, bsz, seq):
    nblk = seq // MOBA_BLOCK
    blk = MOBA_BLOCK
    pair = 2 * ATT_HD
    npair = ATT_HEADS // 2
    assert nblk + SUBLANES <= pair and 2 * len(LOG2E_PIECES) <= SUBLANES and nblk % (2 * MOBA_GROUP) == 0
    pos = jnp.arange(seq, dtype=jnp.int32)[:, None]
    col = jnp.arange(pair, dtype=jnp.int32)[None, :]
    extras = [pos % blk] * len(LOG2E_PIECES) + [pos // blk] * len(LOG2E_PIECES)
    kx = (pos // blk == col).astype(F32)
    for i, e in enumerate(extras):
        kx = jnp.where(col == nblk + i, e.astype(F32), kx)
    kx = kx.astype(BF16)
    return pl.pallas_call(
        functools.partial(_moba_kernel, nblk),
        grid=(bsz, npair, nblk),
        in_specs=[pl.BlockSpec((pair, blk), lambda b, p, j: (p, b * nblk + j)),
                  pl.BlockSpec((seq, pair), lambda b, p, j: (b, p)),
                  pl.BlockSpec((2 * V_ROWS, seq), lambda b, p, j: (p, b)),
                  pl.BlockSpec((nblk, pair), lambda b, p, j: (b, p)),
                  pl.BlockSpec((seq, pair), lambda b, p, j: (0, 0))],
        out_specs=pl.BlockSpec((blk, pair), lambda b, p, j: (b * nblk + j, p)),
        out_shape=jax.ShapeDtypeStruct((bsz * seq, ATT_HEADS * ATT_HD), F32),
        scratch_shapes=[pltpu.VMEM((2, 2 * pair, blk), BF16),
                        pltpu.VMEM((2, MOBA_GROUP * blk, blk), F32),
                        pltpu.VMEM((2, MOBA_GROUP * blk, blk), F32),
                        pltpu.VMEM((2, 1, blk), F32),
                        pltpu.VMEM((2, V_ROWS, blk), F32)],
        compiler_params=_params("parallel", "parallel", "arbitrary"),
        name="moba",
    )(qt, k, vt, km, kx)


def _out_mlp_kernel(final, x_ref, ya_ref, yb_ref, yc_ref, an_ref, wo_ref, gm_ref, w1_ref, w2_ref,
                    gf_ref, o_ref):
    yc = _rms(yc_ref[...], an_ref[...]).astype(BF16)
    y = jnp.concatenate([ya_ref[...], yb_ref[...], yc], axis=1)
    x = x_ref[...] + _dot(y, wo_ref[...])
    hb = _rms(x, gm_ref[...]).astype(BF16)
    mlp = None
    for c in range(w1_ref.shape[1] // FF_CHUNK):
        cols = slice(c * FF_CHUNK, (c + 1) * FF_CHUNK)
        u = jnp.square(jnp.maximum(_dot(hb, w1_ref[:, cols]), 0.0))
        part = _dot(u.astype(BF16), w2_ref[cols, :])
        mlp = part if mlp is None else mlp + part
    x = x + mlp
    if final:
        x = _rms(x, gf_ref[...])
    o_ref[...] = x


def _out_mlp(x2, ya, yb, yc, an, wo, gm, w1, w2, gf, final):
    t, d = x2.shape
    row = lambda a: pl.BlockSpec((ROW_TILE, a.shape[1]), lambda i: (i, 0))
    full = lambda a: pl.BlockSpec(a.shape, lambda i: (0, 0))
    return pl.pallas_call(
        functools.partial(_out_mlp_kernel, final),
        grid=(t // ROW_TILE,),
        in_specs=[row(x2), row(ya), row(yb), row(yc), full(an), full(wo), full(gm), full(w1),
                  full(w2), full(gf)],
        out_specs=pl.BlockSpec((ROW_TILE, d), lambda i: (i, 0)),
        out_shape=jax.ShapeDtypeStruct((t, d), F32),
        compiler_params=_params("parallel"),
        name="out_mlp",
    )(x2, ya, yb, yc, an, wo, gm, w1, w2, gf)


def _block_diag(w):
    g, n, _ = w.shape
    eye = jnp.eye(g, dtype=w.dtype)
    return (eye[:, None, :, None] * w[:, :, None, :]).reshape(g * n, g * n)


def kernel(x, w_in, w_out, norm_mix, norm_mlp, lru_conv_w, lru_conv_b, lru_wa, lru_ba, lru_wx, lru_bx,
           lru_lambda, hg_lower_bounds, hg_norm_w, lru_out_norm, att_out_norm, w_ff1, w_ff2, norm_final):
    bsz, seq, d = x.shape
    depth = w_in.shape[0]
    x2 = x.reshape(bsz * seq, d)
    r2 = lambda a: a.reshape(1, -1)
    for l in range(depth):
        w = w_in[l].astype(BF16)
        wt = jnp.concatenate([w[:, 1536:2048], w[:, 2560:3072]], axis=1).T
        lru, hg, cqt, ck, cvt, km = _in_proj(x2, r2(norm_mix[l]), w, wt)
        wg = jnp.concatenate([_block_diag(lru_wa[l]), _block_diag(lru_wx[l])], axis=1).astype(BF16)
        bg = jnp.concatenate([lru_ba[l], lru_bx[l]]).reshape(1, -1)
        ya = _rglru(lru, bsz, seq, lru_conv_w[l], r2(lru_conv_b[l]), wg, bg, r2(lru_lambda[l]),
                    r2(lru_out_norm[l]))
        yb = _hgrn2(hg, bsz, seq, hg_lower_bounds, r2(jnp.tile(hg_norm_w[l], HG_HEADS)), l)
        yc = _moba(cqt, ck, cvt, km.reshape(-1, km.shape[-1]), bsz, seq)
        x2 = _out_mlp(x2, ya, yb, yc, r2(att_out_norm[l]), w_out[l].astype(BF16), r2(norm_mlp[l]),
                      w_ff1[l].astype(BF16), w_ff2[l].astype(BF16), r2(norm_final), l == depth - 1)
    return x2.reshape(bsz, seq, d)
```

```python
import functools
import math
import struct

import jax
import jax.numpy as jnp
from jax import lax
from jax.experimental import pallas as pl
from jax.experimental.pallas import tpu as pltpu

F32 = jnp.float32
BF16 = jnp.bfloat16
HIGHEST = lax.Precision.HIGHEST

EPS = 1e-6
NEG = -1e30
TINY = 1e-30

LRU_WIDTH = 256
LRU_BLOCKS = 4
LRU_CONV = 4
LRU_C = 8.0
HG_HEADS = 4
HG_DK = 64
ATT_HEADS = 8
ATT_HD = 64
MOBA_BLOCK = 256
MOBA_TOPK = 3

SUBLANES = 8
VMEM_LIMIT = 56 * 1024 * 1024

ROW_TILE = 512
LRU_TILE = 1024
LRU_SUB = 256
HG_TILE = 512
HG_CHUNK = 128
FF_CHUNK = 1024
MOBA_GROUP = 2
MOBA_LONG = 4
MOBA_HEADS = 2
ONES_ROWS = 16
V_ROWS = ATT_HD + ONES_ROWS


def _bf16_pieces(x, n):
    pieces = []
    for _ in range(n):
        bits = struct.unpack("<I", struct.pack("<f", x))[0]
        bits = (bits + 0x7FFF + ((bits >> 16) & 1)) & 0xFFFF0000
        piece = struct.unpack("<f", struct.pack("<I", bits))[0]
        pieces.append(piece)
        x -= piece
    return tuple(pieces)


LOG2E = math.log2(math.e)
LOG2E_PIECES = _bf16_pieces(LOG2E, 3)


def _dot(a, b):
    return jnp.dot(a, b, preferred_element_type=F32)


def _dot_t(a, b, precision=None):
    return lax.dot_general(a, b, (((1,), (1,)), ((), ())), precision=precision,
                           preferred_element_type=F32)


def _rms(x, g):
    return x * lax.rsqrt(jnp.mean(x * x, axis=-1, keepdims=True) + EPS) * g


def _sigmoid(x):
    return 1.0 / (1.0 + jnp.exp(-x))


def _softplus(x):
    return jnp.maximum(x, 0.0) + jnp.log1p(jnp.exp(-jnp.abs(x)))


def _params(*sem, flags=None):
    return pltpu.CompilerParams(dimension_semantics=sem, vmem_limit_bytes=VMEM_LIMIT, flags=flags)


def _in_proj_kernel(x_ref, g_ref, w_ref, wt_ref, lru_ref, hg_ref, qt_ref, k_ref, vt_ref, km_ref):
    hb = _rms(x_ref[...], g_ref[...]).astype(BF16)
    lru_ref[...] = _dot(hb, w_ref[:, 0:512])
    hg_ref[...] = _dot(hb, w_ref[:, 512:1536])
    qt_ref[...] = _dot_t(wt_ref[0:512, :], hb)
    k = _dot(hb, w_ref[:, 2048:2560])
    k_ref[...] = k.astype(BF16)
    for i in range(ROW_TILE // MOBA_BLOCK):
        km_ref[i] = jnp.mean(k[i * MOBA_BLOCK:(i + 1) * MOBA_BLOCK], axis=0, keepdims=True)
    vt = _dot_t(wt_ref[512:1024, :], hb).astype(BF16)
    ones = jnp.ones((ONES_ROWS, ROW_TILE), BF16)
    for h in range(ATT_HEADS):
        vt_ref[h * V_ROWS:h * V_ROWS + ATT_HD, :] = vt[h * ATT_HD:(h + 1) * ATT_HD, :]
        vt_ref[h * V_ROWS + ATT_HD:(h + 1) * V_ROWS, :] = ones


def _in_proj(x2, g, w, wt):
    t, d = x2.shape
    nblk = t // MOBA_BLOCK
    row = lambda width: pl.BlockSpec((ROW_TILE, width), lambda i: (i, 0))
    col = lambda height: pl.BlockSpec((height, ROW_TILE), lambda i: (0, i))
    return pl.pallas_call(
        _in_proj_kernel,
        grid=(t // ROW_TILE,),
        in_specs=[row(d), pl.BlockSpec((1, d), lambda i: (0, 0)),
                  pl.BlockSpec(w.shape, lambda i: (0, 0)), pl.BlockSpec(wt.shape, lambda i: (0, 0))],
        out_specs=[row(512), row(1024), col(512), row(512), col(ATT_HEADS * V_ROWS),
                   pl.BlockSpec((ROW_TILE // MOBA_BLOCK, 1, 512), lambda i: (i, 0, 0))],
        out_shape=[jax.ShapeDtypeStruct((t, 512), F32), jax.ShapeDtypeStruct((t, 1024), F32),
                   jax.ShapeDtypeStruct((512, t), F32), jax.ShapeDtypeStruct((t, 512), BF16),
                   jax.ShapeDtypeStruct((ATT_HEADS * V_ROWS, t), BF16),
                   jax.ShapeDtypeStruct((nblk, 1, 512), F32)],
        compiler_params=_params("parallel"),
        name="in_proj",
    )(x2, g, w, wt)


def _rglru_kernel(xy_ref, cw_ref, cb_ref, wg_ref, bg_ref, lam_ref, nw_ref, o_ref, prev_ref, h_ref):
    @pl.when(pl.program_id(1) == 0)
    def _():
        prev_ref[...] = jnp.zeros_like(prev_ref)
        h_ref[...] = jnp.zeros_like(h_ref)

    w = LRU_WIDTH
    row = lax.broadcasted_iota(jnp.int32, (LRU_SUB, w), 0)
    sp = _softplus(-lam_ref[...])
    for c in range(LRU_TILE // LRU_SUB):
        rows = slice(c * LRU_SUB, (c + 1) * LRU_SUB)
        xb = xy_ref[rows, 0:w]
        yb = xy_ref[rows, w:2 * w]
        prev = prev_ref[...]
        xc = xb * cw_ref[LRU_CONV - 1:LRU_CONV, :] + cb_ref[...]
        for k in range(1, LRU_CONV):
            shifted = jnp.where(row >= k, pltpu.roll(xb, k, 0), pltpu.roll(prev, k, 0))
            xc = xc + shifted * cw_ref[LRU_CONV - 1 - k:LRU_CONV - k, :]
        prev_ref[...] = xb
        gates = _dot(xc.astype(BF16), wg_ref[...]) + bg_ref[...]
        r = _sigmoid(gates[:, 0:w])
        i = _sigmoid(gates[:, w:2 * w])
        log_a = -LRU_C * r * sp
        a = jnp.exp(log_a)
        b = jnp.sqrt(1.0 - jnp.exp(2.0 * log_a)) * (i * xc)
        s = 1
        while s < LRU_SUB:
            keep = row >= s
            a_s = jnp.where(keep, pltpu.roll(a, s, 0), 1.0)
            b_s = jnp.where(keep, pltpu.roll(b, s, 0), 0.0)
            b = a * b_s + b
            a = a * a_s
            s *= 2
        h = b + a * h_ref[...]
        h_ref[...] = h[LRU_SUB - 1:LRU_SUB, :]
        y = h * jax.nn.gelu(yb, approximate=True)
        o_ref[rows, :] = _rms(y, nw_ref[...]).astype(o_ref.dtype)


def _rglru(xy, bsz, seq, cw, cb, wg, bg, lam, nw):
    nt = seq // LRU_TILE
    w = LRU_WIDTH
    vec = lambda a: pl.BlockSpec(a.shape, lambda b, t: (0, 0))
    return pl.pallas_call(
        _rglru_kernel,
        grid=(bsz, nt),
        in_specs=[pl.BlockSpec((LRU_TILE, 2 * w), lambda b, t: (b * nt + t, 0)),
                  vec(cw), vec(cb), vec(wg), vec(bg), vec(lam), vec(nw)],
        out_specs=pl.BlockSpec((LRU_TILE, w), lambda b, t: (b * nt + t, 0)),
        out_shape=jax.ShapeDtypeStruct((bsz * seq, w), BF16),
        scratch_shapes=[pltpu.VMEM((LRU_SUB, w), F32), pltpu.VMEM((1, w), F32)],
        compiler_params=_params("parallel", "arbitrary"),
        name="rglru",
    )(xy, cw, cb, wg, bg, lam, nw)


def _hgrn2_kernel(layer, z_ref, lbp_ref, nw_ref, o_ref, state_ref):
    @pl.when(pl.program_id(1) == 0)
    def _():
        state_ref[...] = jnp.zeros_like(state_ref)

    n = HG_HEADS * HG_DK
    L = HG_CHUNK
    lbp = lbp_ref[...]
    e = jnp.exp(lbp - jnp.max(lbp, axis=0, keepdims=True))
    soft = e / jnp.sum(e, axis=0, keepdims=True)
    lb = jnp.sum(soft[0:layer + 1], axis=0, keepdims=True) - soft[0:1]
    log_lb = jnp.log(jnp.maximum(lb, TINY))
    log_1m = jnp.log1p(-lb)

    lane_h = lax.broadcasted_iota(jnp.int32, (1, n), 1) // HG_DK
    head_of_row = lax.broadcasted_iota(jnp.int32, (n, n), 0) // HG_DK
    head_of_col = lax.broadcasted_iota(jnp.int32, (n, n), 1) // HG_DK
    same_head = head_of_row == head_of_col
    ones_bd = jnp.where(same_head, 1.0, 0.0).astype(BF16)
    row = lax.broadcasted_iota(jnp.int32, (L, n), 0)
    sub = lax.broadcasted_iota(jnp.int32, (1, SUBLANES, 1), 1)
    t_idx = lax.broadcasted_iota(jnp.int32, (L, L), 0)
    s_idx = lax.broadcasted_iota(jnp.int32, (L, L), 1)

    def chunk(ci, carry):
        r0 = pl.multiple_of(ci * L, L)
        q = z_ref[pl.ds(r0, L), 0:n]
        f = z_ref[pl.ds(r0, L), n:2 * n]
        v = z_ref[pl.ds(r0, L), 2 * n:3 * n]
        g = z_ref[pl.ds(r0, L), 3 * n:4 * n]
        lsig = -_softplus(-f)
        t1 = log_1m + lsig
        log_f = jnp.maximum(log_lb, t1) + jnp.log1p(jnp.exp(-jnp.abs(log_lb - t1)))
        kk = (1.0 - lb) * _sigmoid(-f)
        qs = q * _sigmoid(q)
        b = log_f
        s = 1
        while s < L:
            b = b + jnp.where(row >= s, pltpu.roll(b, s, 0), 0.0)
            s *= 2

        g8 = L // SUBLANES
        q3 = qs.reshape(g8, SUBLANES, n)
        k3 = kk.reshape(g8, SUBLANES, n)
        b3 = b.reshape(g8, SUBLANES, n)
        v3 = v.reshape(g8, SUBLANES, n)
        o = jnp.zeros((L, n), F32)
        for j in range(SUBLANES):
            d = jnp.where(sub >= j, b3 - b3[:, j:j + 1, :], NEG)
            xj = q3 * k3[:, j:j + 1, :] * jnp.exp(d)
            aj = _dot(xj.reshape(L, n).astype(BF16), ones_bd)
            o = o + (aj.reshape(g8, SUBLANES, n) * v3[:, j:j + 1, :]).reshape(L, n)

        att = [jnp.zeros((L, L), F32) for _ in range(HG_HEADS)]
        c = SUBLANES
        while c < L:
            bc = b.reshape(L // c, c, n)
            bend = jnp.broadcast_to(bc[:, c - 1:c, :], (L // c, c, n)).reshape(L, n)
            bprev = jnp.where(row >= c, pltpu.roll(bend, c, 0), 0.0)
            qt = qs * jnp.exp(b - bprev)
            kt = (kk * jnp.exp(bend - b)).astype(BF16)
            pair = ((t_idx // c) % 2 == 1) & ((s_idx // c) == (t_idx // c) - 1)
            for h in range(HG_HEADS):
                sc = _dot_t(jnp.where(lane_h == h, qt, 0.0).astype(BF16), kt)
                att[h] = att[h] + jnp.where(pair, sc, 0.0)
            c *= 2
        att_cat = jnp.concatenate([a.astype(BF16) for a in att], axis=1)
        v_stack = jnp.concatenate([jnp.where(lane_h == h, v, 0.0).astype(BF16)
                                   for h in range(HG_HEADS)], axis=0)
        o = o + _dot(att_cat, v_stack)

        state = state_ref[...]
        o = o + _dot_t((qs * jnp.exp(b)).astype(BF16), state.astype(BF16))
        b_last = b[L - 1:L, :]
        kdec = (kk * jnp.exp(b_last - b)).astype(BF16)
        upd = lax.dot_general(v.astype(BF16), kdec, (((0,), (0,)), ((), ())),
                              preferred_element_type=F32)
        state_ref[...] = state * jnp.exp(b_last) + jnp.where(same_head, upd, 0.0)

        ms = jnp.dot(o * o, jnp.where(same_head, 1.0 / HG_DK, 0.0), precision=HIGHEST,
                     preferred_element_type=F32)
        y = o * lax.rsqrt(ms + EPS) * nw_ref[...] * (g * _sigmoid(g))
        o_ref[pl.ds(r0, L), :] = y.astype(o_ref.dtype)
        return carry

    lax.fori_loop(0, HG_TILE // L, chunk, 0)


def _hgrn2(z, bsz, seq, lbp, nw, layer):
    nt = seq // HG_TILE
    n = HG_HEADS * HG_DK
    return pl.pallas_call(
        functools.partial(_hgrn2_kernel, layer),
        grid=(bsz, nt),
        in_specs=[pl.BlockSpec((HG_TILE, 4 * n), lambda b, t: (b * nt + t, 0)),
                  pl.BlockSpec(lbp.shape, lambda b, t: (0, 0)),
                  pl.BlockSpec(nw.shape, lambda b, t: (0, 0))],
        out_specs=pl.BlockSpec((HG_TILE, n), lambda b, t: (b * nt + t, 0)),
        out_shape=jax.ShapeDtypeStruct((bsz * seq, n), BF16),
        scratch_shapes=[pltpu.VMEM((n, n), F32)],
        compiler_params=_params("parallel", "arbitrary"),
        name="hgrn2",
    )(z, lbp, nw)


def _moba_kernel(nblk, qt_ref, k_ref, vt_ref, km_ref, kx_ref, o_ref, qa_ref, sa_ref, sb_ref,
                 m_ref, acc_ref):
    j = pl.program_id(2)
    hp = pl.program_id(1)
    blk = MOBA_BLOCK
    hd = ATT_HD
    pair = 2 * hd
    span = MOBA_GROUP * blk
    scale = 1.0 / math.sqrt(hd)
    feat = lax.broadcasted_iota(jnp.int32, (pair, 1), 0)
    n_t = lax.broadcasted_iota(jnp.int32, (nblk, blk), 0)
    b_row = lax.broadcasted_iota(jnp.int32, (SUBLANES, blk), 0)
    causal = (lax.broadcasted_iota(jnp.int32, (blk, blk), 0)
              <= lax.broadcasted_iota(jnp.int32, (blk, blk), 1))
    qt = qt_ref[...]
    km = km_ref[...]
    j0 = pl.multiple_of(j * blk, blk)

    def keys(start, size, pr):
        k_pair = k_ref[pl.ds(start, size), pr * pair:(pr + 1) * pair]
        return jnp.concatenate([k_pair, kx_ref[pl.ds(start, size), :]], axis=1)

    k_own = [keys(j0, blk, pr) for pr in range(MOBA_HEADS // 2)]
    vt_own = vt_ref[:, pl.ds(j0, blk)]
    pad = jnp.zeros((pair - nblk - SUBLANES, blk), F32)
    q_own = []
    for h in range(MOBA_HEADS):
        pr = h // 2
        slope = jnp.exp2(-8.0 * (MOBA_HEADS * hp + h + 1).astype(F32) / ATT_HEADS)
        qm = jnp.where(feat // hd == h % 2, qt[pr * pair:(pr + 1) * pair], 0.0)
        gate = jnp.dot(km[:, pr * pair:(pr + 1) * pair], qm, precision=HIGHEST,
                       preferred_element_type=F32)
        gate = jnp.where(n_t < j, gate, NEG)
        chosen = jnp.zeros((nblk, blk), F32)
        for r in range(MOBA_TOPK):
            top = jnp.max(gate, axis=0, keepdims=True)
            first = jnp.min(jnp.where(gate == top, n_t, nblk), axis=0, keepdims=True)
            pick = n_t == first
            chosen = jnp.where(pick, jnp.where(j > r, 1.0, 0.0), chosen)
            gate = jnp.where(pick, -jnp.inf, gate)
        bias = jnp.zeros((SUBLANES, blk), F32)
        for i, piece in enumerate(LOG2E_PIECES):
            bias = jnp.where(b_row == i, slope * piece, bias)
            bias = jnp.where(b_row == len(LOG2E_PIECES) + i, slope * (piece * blk), bias)
        qs = (qm * (scale * LOG2E)).astype(BF16)
        past = jnp.concatenate([jnp.where(chosen > 0.0, 0.0, NEG), bias, pad], axis=0)
        own = jnp.concatenate([jnp.where(n_t == j, 0.0, NEG), bias, pad], axis=0)
        qa_ref[h] = jnp.concatenate([qs, past.astype(BF16)], axis=0)
        q_own.append(jnp.concatenate([qs, own.astype(BF16)], axis=0))

    def score(g, s_ref):
        for pr in range(MOBA_HEADS // 2):
            k_g = keys(pl.multiple_of(g * span, span), span, pr)
            for h in (2 * pr, 2 * pr + 1):
                s_ref[h] = _dot(k_g, qa_ref[h])

    def attend(g, s_ref):
        vt_g = vt_ref[:, pl.ds(pl.multiple_of(g * span, span), span)]
        for h in range(MOBA_HEADS):
            s = s_ref[h]
            m_old = m_ref[h]
            m_new = jnp.maximum(m_old, jnp.max(s, axis=0, keepdims=True))
            p = jnp.exp2((s - m_new).astype(BF16))
            m_ref[h] = m_new
            pv = _dot(vt_g[h * V_ROWS:(h + 1) * V_ROWS, :], p)
            acc_ref[h] = jnp.exp2(m_old - m_new) * acc_ref[h] + pv

    last = nblk // MOBA_GROUP - 1

    def chain(g0, count, more):
        for t in range(count):
            cur, nxt = (sa_ref, sb_ref) if t % 2 == 0 else (sb_ref, sa_ref)
            if t + 1 < count or more:
                score(jnp.minimum(g0 + t + 1, last), nxt)
            attend(g0 + t, cur)

    per_long = MOBA_LONG * MOBA_GROUP
    rem = j % per_long
    nlong = j // per_long + (rem > 2 * MOBA_GROUP).astype(jnp.int32)
    short = (rem >= 1) & (rem <= 2 * MOBA_GROUP)
    score(0, sa_ref)

    for h in range(MOBA_HEADS):
        s = jnp.where(causal, _dot(k_own[h // 2], q_own[h]), NEG)
        m = jnp.max(s, axis=0, keepdims=True)
        p = jnp.exp2((s - m).astype(BF16))
        m_ref[h] = m
        acc_ref[h] = _dot(vt_own[h * V_ROWS:(h + 1) * V_ROWS, :], p)

    def body(i, carry):
        chain(i * MOBA_LONG, MOBA_LONG, True)
        return carry

    lax.fori_loop(0, nlong, body, 0)

    @pl.when(short)
    def _():
        chain(nlong * MOBA_LONG, 2, False)

    out = []
    for h in range(MOBA_HEADS):
        acc = acc_ref[h]
        out.append(acc[0:hd] / acc[hd:hd + 1])
    o_ref[...] = jnp.concatenate(out, axis=0).T


def _moba(qt, k, vt, km, bsz, seq):
    nblk = seq // MOBA_BLOCK
    blk = MOBA_BLOCK
    pair = 2 * ATT_HD
    width = MOBA_HEADS * ATT_HD
    assert nblk + SUBLANES <= pair and 2 * len(LOG2E_PIECES) <= SUBLANES
    assert MOBA_LONG % 2 == 0 and nblk % (MOBA_LONG * MOBA_GROUP) == 0
    pos = jnp.arange(seq, dtype=jnp.int32)[:, None]
    col = jnp.arange(pair, dtype=jnp.int32)[None, :]
    extras = [pos % blk] * len(LOG2E_PIECES) + [pos // blk] * len(LOG2E_PIECES)
    kx = (pos // blk == col).astype(F32)
    for i, e in enumerate(extras):
        kx = jnp.where(col == nblk + i, e.astype(F32), kx)
    kx = kx.astype(BF16)
    return pl.pallas_call(
        functools.partial(_moba_kernel, nblk),
        grid=(bsz, ATT_HEADS // MOBA_HEADS, nblk),
        in_specs=[pl.BlockSpec((width, blk), lambda b, p, j: (p, b * nblk + j)),
                  pl.BlockSpec((seq, width), lambda b, p, j: (b, p)),
                  pl.BlockSpec((MOBA_HEADS * V_ROWS, seq), lambda b, p, j: (p, b)),
                  pl.BlockSpec((nblk, width), lambda b, p, j: (b, p)),
                  pl.BlockSpec((seq, pair), lambda b, p, j: (0, 0))],
        out_specs=pl.BlockSpec((blk, width), lambda b, p, j: (b * nblk + j, p)),
        out_shape=jax.ShapeDtypeStruct((bsz * seq, ATT_HEADS * ATT_HD), F32),
        scratch_shapes=[pltpu.VMEM((MOBA_HEADS, 2 * pair, blk), BF16),
                        pltpu.VMEM((MOBA_HEADS, MOBA_GROUP * blk, blk), F32),
                        pltpu.VMEM((MOBA_HEADS, MOBA_GROUP * blk, blk), F32),
                        pltpu.VMEM((MOBA_HEADS, 1, blk), F32),
                        pltpu.VMEM((MOBA_HEADS, V_ROWS, blk), F32)],
        compiler_params=_params("parallel", "parallel", "arbitrary"),
        name="moba",
    )(qt, k, vt, km, kx)


def _out_mlp_kernel(final, x_ref, ya_ref, yb_ref, yc_ref, an_ref, wo_ref, gm_ref, w1_ref, w2_ref,
                    gf_ref, o_ref):
    yc = _rms(yc_ref[...], an_ref[...]).astype(BF16)
    y = jnp.concatenate([ya_ref[...], yb_ref[...], yc], axis=1)
    x = x_ref[...] + _dot(y, wo_ref[...])
    hb = _rms(x, gm_ref[...]).astype(BF16)
    mlp = None
    for c in range(w1_ref.shape[1] // FF_CHUNK):
        cols = slice(c * FF_CHUNK, (c + 1) * FF_CHUNK)
        u = jnp.square(jnp.maximum(_dot(hb, w1_ref[:, cols]), 0.0))
        part = _dot(u.astype(BF16), w2_ref[cols, :])
        mlp = part if mlp is None else mlp + part
    x = x + mlp
    if final:
        x = _rms(x, gf_ref[...])
    o_ref[...] = x


def _out_mlp(x2, ya, yb, yc, an, wo, gm, w1, w2, gf, final):
    t, d = x2.shape
    row = lambda a: pl.BlockSpec((ROW_TILE, a.shape[1]), lambda i: (i, 0))
    full = lambda a: pl.BlockSpec(a.shape, lambda i: (0, 0))
    return pl.pallas_call(
        functools.partial(_out_mlp_kernel, final),
        grid=(t // ROW_TILE,),
        in_specs=[row(x2), row(ya), row(yb), row(yc), full(an), full(wo), full(gm), full(w1),
                  full(w2), full(gf)],
        out_specs=pl.BlockSpec((ROW_TILE, d), lambda i: (i, 0)),
        out_shape=jax.ShapeDtypeStruct((t, d), F32),
        compiler_params=_params("parallel"),
        name="out_mlp",
    )(x2, ya, yb, yc, an, wo, gm, w1, w2, gf)


def _block_diag(w):
    g, n, _ = w.shape
    eye = jnp.eye(g, dtype=w.dtype)
    return (eye[:, None, :, None] * w[:, :, None, :]).reshape(g * n, g * n)


def kernel(x, w_in, w_out, norm_mix, norm_mlp, lru_conv_w, lru_conv_b, lru_wa, lru_ba, lru_wx, lru_bx,
           lru_lambda, hg_lower_bounds, hg_norm_w, lru_out_norm, att_out_norm, w_ff1, w_ff2, norm_final):
    bsz, seq, d = x.shape
    depth = w_in.shape[0]
    x2 = x.reshape(bsz * seq, d)
    r2 = lambda a: a.reshape(1, -1)
    for l in range(depth):
        w = w_in[l].astype(BF16)
        wt = jnp.concatenate([w[:, 1536:2048], w[:, 2560:3072]], axis=1).T
        lru, hg, cqt, ck, cvt, km = _in_proj(x2, r2(norm_mix[l]), w, wt)
        wg = jnp.concatenate([_block_diag(lru_wa[l]), _block_diag(lru_wx[l])], axis=1).astype(BF16)
        bg = jnp.concatenate([lru_ba[l], lru_bx[l]]).reshape(1, -1)
        ya = _rglru(lru, bsz, seq, lru_conv_w[l], r2(lru_conv_b[l]), wg, bg, r2(lru_lambda[l]),
                    r2(lru_out_norm[l]))
        yb = _hgrn2(hg, bsz, seq, hg_lower_bounds, r2(jnp.tile(hg_norm_w[l], HG_HEADS)), l)
        yc = _moba(cqt, ck, cvt, km.reshape(-1, km.shape[-1]), bsz, seq)
        x2 = _out_mlp(x2, ya, yb, yc, r2(att_out_norm[l]), w_out[l].astype(BF16), r2(norm_mlp[l]),
                      w_ff1[l].astype(BF16), w_ff2[l].astype(BF16), r2(norm_final), l == depth - 1)
    return x2.reshape(bsz, seq, d)
```

```python
import functools
import math
import struct

import jax
import jax.numpy as jnp
from jax import lax
from jax.experimental import pallas as pl
from jax.experimental.pallas import tpu as pltpu

F32 = jnp.float32
BF16 = jnp.bfloat16
HIGHEST = lax.Precision.HIGHEST

EPS = 1e-6
NEG = -1e30
TINY = 1e-30

LRU_WIDTH = 256
LRU_BLOCKS = 4
LRU_CONV = 4
LRU_C = 8.0
HG_HEADS = 4
HG_DK = 64
ATT_HEADS = 8
ATT_HD = 64
MOBA_BLOCK = 256
MOBA_TOPK = 3

SUBLANES = 8
VMEM_LIMIT = 56 * 1024 * 1024

ROW_TILE = 512
LRU_TILE = 1024
LRU_SUB = 256
HG_TILE = 512
HG_CHUNK = 128
FF_CHUNK = 1024
MOBA_GROUP = 2
MOBA_LONG = 8
MOBA_QBLOCKS = 1
MOBA_HEADS = 2
ONES_ROWS = 16
V_ROWS = ATT_HD + ONES_ROWS


def _bf16_pieces(x, n):
    pieces = []
    for _ in range(n):
        bits = struct.unpack("<I", struct.pack("<f", x))[0]
        bits = (bits + 0x7FFF + ((bits >> 16) & 1)) & 0xFFFF0000
        piece = struct.unpack("<f", struct.pack("<I", bits))[0]
        pieces.append(piece)
        x -= piece
    return tuple(pieces)


LOG2E = math.log2(math.e)
LOG2E_PIECES = _bf16_pieces(LOG2E, 3)


def _dot(a, b):
    return jnp.dot(a, b, preferred_element_type=F32)


def _dot_t(a, b, precision=None):
    return lax.dot_general(a, b, (((1,), (1,)), ((), ())), precision=precision,
                           preferred_element_type=F32)


def _rms(x, g):
    return x * lax.rsqrt(jnp.mean(x * x, axis=-1, keepdims=True) + EPS) * g


def _sigmoid(x):
    return 1.0 / (1.0 + jnp.exp(-x))


def _softplus(x):
    return jnp.maximum(x, 0.0) + jnp.log(1.0 + jnp.exp(-jnp.abs(x)))


def _params(*sem, flags=None):
    return pltpu.CompilerParams(dimension_semantics=sem, vmem_limit_bytes=VMEM_LIMIT, flags=flags)


def _in_proj_kernel(x_ref, g_ref, w_ref, wt_ref, lru_ref, hg_ref, qt_ref, k_ref, vt_ref, km_ref):
    hb = _rms(x_ref[...], g_ref[...]).astype(BF16)
    lru_ref[...] = _dot(hb, w_ref[:, 0:512])
    hg_ref[...] = _dot(hb, w_ref[:, 512:1536])
    qt_ref[...] = _dot_t(wt_ref[0:512, :], hb)
    k = _dot(hb, w_ref[:, 2048:2560])
    k_ref[...] = k.astype(BF16)
    for i in range(ROW_TILE // MOBA_BLOCK):
        km_ref[i] = jnp.mean(k[i * MOBA_BLOCK:(i + 1) * MOBA_BLOCK], axis=0, keepdims=True)
    vt = _dot_t(wt_ref[512:1024, :], hb).astype(BF16)
    ones = jnp.ones((ONES_ROWS, ROW_TILE), BF16)
    for h in range(ATT_HEADS):
        vt_ref[h * V_ROWS:h * V_ROWS + ATT_HD, :] = vt[h * ATT_HD:(h + 1) * ATT_HD, :]
        vt_ref[h * V_ROWS + ATT_HD:(h + 1) * V_ROWS, :] = ones


def _in_proj(x2, g, w, wt):
    t, d = x2.shape
    nblk = t // MOBA_BLOCK
    row = lambda width: pl.BlockSpec((ROW_TILE, width), lambda i: (i, 0))
    col = lambda height: pl.BlockSpec((height, ROW_TILE), lambda i: (0, i))
    return pl.pallas_call(
        _in_proj_kernel,
        grid=(t // ROW_TILE,),
        in_specs=[row(d), pl.BlockSpec((1, d), lambda i: (0, 0)),
                  pl.BlockSpec(w.shape, lambda i: (0, 0)), pl.BlockSpec(wt.shape, lambda i: (0, 0))],
        out_specs=[row(512), row(1024), col(512), row(512), col(ATT_HEADS * V_ROWS),
                   pl.BlockSpec((ROW_TILE // MOBA_BLOCK, 1, 512), lambda i: (i, 0, 0))],
        out_shape=[jax.ShapeDtypeStruct((t, 512), F32), jax.ShapeDtypeStruct((t, 1024), F32),
                   jax.ShapeDtypeStruct((512, t), F32), jax.ShapeDtypeStruct((t, 512), BF16),
                   jax.ShapeDtypeStruct((ATT_HEADS * V_ROWS, t), BF16),
                   jax.ShapeDtypeStruct((nblk, 1, 512), F32)],
        compiler_params=_params("parallel"),
        name="in_proj",
    )(x2, g, w, wt)


def _rglru_kernel(xy_ref, cw_ref, cb_ref, wg_ref, bg_ref, lam_ref, nw_ref, o_ref, prev_ref, h_ref):
    @pl.when(pl.program_id(1) == 0)
    def _():
        prev_ref[...] = jnp.zeros_like(prev_ref)
        h_ref[...] = jnp.zeros_like(h_ref)

    w = LRU_WIDTH
    row = lax.broadcasted_iota(jnp.int32, (LRU_SUB, w), 0)
    sp = _softplus(-lam_ref[...])
    for c in range(LRU_TILE // LRU_SUB):
        rows = slice(c * LRU_SUB, (c + 1) * LRU_SUB)
        xb = xy_ref[rows, 0:w]
        yb = xy_ref[rows, w:2 * w]
        prev = prev_ref[...]
        xc = xb * cw_ref[LRU_CONV - 1:LRU_CONV, :] + cb_ref[...]
        for k in range(1, LRU_CONV):
            shifted = jnp.where(row >= k, pltpu.roll(xb, k, 0), pltpu.roll(prev, k, 0))
            xc = xc + shifted * cw_ref[LRU_CONV - 1 - k:LRU_CONV - k, :]
        prev_ref[...] = xb
        gates = _dot(xc.astype(BF16), wg_ref[...]) + bg_ref[...]
        r = _sigmoid(gates[:, 0:w])
        i = _sigmoid(gates[:, w:2 * w])
        log_a = -LRU_C * r * sp
        a = jnp.exp(log_a)
        b = jnp.sqrt(1.0 - jnp.exp(2.0 * log_a)) * (i * xc)
        s = 1
        while s < LRU_SUB:
            keep = row >= s
            a_s = jnp.where(keep, pltpu.roll(a, s, 0), 1.0)
            b_s = jnp.where(keep, pltpu.roll(b, s, 0), 0.0)
            b = a * b_s + b
            a = a * a_s
            s *= 2
        h = b + a * h_ref[...]
        h_ref[...] = h[LRU_SUB - 1:LRU_SUB, :]
        y = h * jax.nn.gelu(yb, approximate=True)
        o_ref[rows, :] = _rms(y, nw_ref[...]).astype(o_ref.dtype)


def _rglru(xy, bsz, seq, cw, cb, wg, bg, lam, nw):
    nt = seq // LRU_TILE
    w = LRU_WIDTH
    vec = lambda a: pl.BlockSpec(a.shape, lambda b, t: (0, 0))
    return pl.pallas_call(
        _rglru_kernel,
        grid=(bsz, nt),
        in_specs=[pl.BlockSpec((LRU_TILE, 2 * w), lambda b, t: (b * nt + t, 0)),
                  vec(cw), vec(cb), vec(wg), vec(bg), vec(lam), vec(nw)],
        out_specs=pl.BlockSpec((LRU_TILE, w), lambda b, t: (b * nt + t, 0)),
        out_shape=jax.ShapeDtypeStruct((bsz * seq, w), BF16),
        scratch_shapes=[pltpu.VMEM((LRU_SUB, w), F32), pltpu.VMEM((1, w), F32)],
        compiler_params=_params("parallel", "arbitrary"),
        name="rglru",
    )(xy, cw, cb, wg, bg, lam, nw)


def _hgrn2_kernel(layer, z_ref, lbp_ref, nw_ref, o_ref, state_ref):
    @pl.when(pl.program_id(1) == 0)
    def _():
        state_ref[...] = jnp.zeros_like(state_ref)

    n = HG_HEADS * HG_DK
    L = HG_CHUNK
    lbp = lbp_ref[...]
    e = jnp.exp(lbp - jnp.max(lbp, axis=0, keepdims=True))
    soft = e / jnp.sum(e, axis=0, keepdims=True)
    lb = jnp.sum(soft[0:layer + 1], axis=0, keepdims=True) - soft[0:1]
    log_lb = jnp.log(jnp.maximum(lb, TINY))
    log_1m = jnp.log1p(-lb)

    lane_h = lax.broadcasted_iota(jnp.int32, (1, n), 1) // HG_DK
    head_of_row = lax.broadcasted_iota(jnp.int32, (n, n), 0) // HG_DK
    head_of_col = lax.broadcasted_iota(jnp.int32, (n, n), 1) // HG_DK
    same_head = head_of_row == head_of_col
    ones_bd = jnp.where(same_head, 1.0, 0.0).astype(BF16)
    row = lax.broadcasted_iota(jnp.int32, (L, n), 0)
    sub = lax.broadcasted_iota(jnp.int32, (1, SUBLANES, 1), 1)
    t_idx = lax.broadcasted_iota(jnp.int32, (L, L), 0)
    s_idx = lax.broadcasted_iota(jnp.int32, (L, L), 1)

    def chunk(ci, carry):
        r0 = pl.multiple_of(ci * L, L)
        q = z_ref[pl.ds(r0, L), 0:n]
        f = z_ref[pl.ds(r0, L), n:2 * n]
        v = z_ref[pl.ds(r0, L), 2 * n:3 * n]
        g = z_ref[pl.ds(r0, L), 3 * n:4 * n]
        lsig = -_softplus(-f)
        t1 = log_1m + lsig
        log_f = jnp.maximum(log_lb, t1) + jnp.log(1.0 + jnp.exp(-jnp.abs(log_lb - t1)))
        kk = (1.0 - lb) * _sigmoid(-f)
        qs = q * _sigmoid(q)
        b = log_f * LOG2E
        s = 1
        while s < L:
            b = b + jnp.where(row >= s, pltpu.roll(b, s, 0), 0.0)
            s *= 2

        g8 = L // SUBLANES
        q3 = qs.reshape(g8, SUBLANES, n)
        k3 = kk.reshape(g8, SUBLANES, n)
        b3 = b.reshape(g8, SUBLANES, n)
        v3 = v.reshape(g8, SUBLANES, n)
        o = jnp.zeros((L, n), F32)
        for j in range(SUBLANES):
            d = jnp.where(sub >= j, b3 - b3[:, j:j + 1, :], NEG)
            xj = q3 * k3[:, j:j + 1, :] * jnp.exp2(d)
            aj = _dot(xj.reshape(L, n).astype(BF16), ones_bd)
            o = o + (aj.reshape(g8, SUBLANES, n) * v3[:, j:j + 1, :]).reshape(L, n)

        att = [jnp.zeros((L, L), F32) for _ in range(HG_HEADS)]
        c = SUBLANES
        while c < L:
            bc = b.reshape(L // c, c, n)
            bend = jnp.broadcast_to(bc[:, c - 1:c, :], (L // c, c, n)).reshape(L, n)
            bprev = jnp.where(row >= c, pltpu.roll(bend, c, 0), 0.0)
            qt = qs * jnp.exp2(b - bprev)
            kt = (kk * jnp.exp2(bend - b)).astype(BF16)
            pair = ((t_idx // c) % 2 == 1) & ((s_idx // c) == (t_idx // c) - 1)
            for h in range(HG_HEADS):
                sc = _dot_t(jnp.where(lane_h == h, qt, 0.0).astype(BF16), kt)
                att[h] = att[h] + jnp.where(pair, sc, 0.0)
            c *= 2
        att_cat = jnp.concatenate([a.astype(BF16) for a in att], axis=1)
        v_stack = jnp.concatenate([jnp.where(lane_h == h, v, 0.0).astype(BF16)
                                   for h in range(HG_HEADS)], axis=0)
        o = o + _dot(att_cat, v_stack)

        state = state_ref[...]
        o = o + _dot_t((qs * jnp.exp2(b)).astype(BF16), state.astype(BF16))
        b_last = b[L - 1:L, :]
        kdec = (kk * jnp.exp2(b_last - b)).astype(BF16)
        upd = lax.dot_general(v.astype(BF16), kdec, (((0,), (0,)), ((), ())),
                              preferred_element_type=F32)
        state_ref[...] = state * jnp.exp2(b_last) + jnp.where(same_head, upd, 0.0)

        ms = _dot((o * o).astype(BF16), ones_bd) * (1.0 / HG_DK)
        y = o * lax.rsqrt(ms + EPS) * nw_ref[...] * (g * _sigmoid(g))
        o_ref[pl.ds(r0, L), :] = y.astype(o_ref.dtype)
        return carry

    lax.fori_loop(0, HG_TILE // L, chunk, 0)


def _hgrn2(z, bsz, seq, lbp, nw, layer):
    nt = seq // HG_TILE
    n = HG_HEADS * HG_DK
    return pl.pallas_call(
        functools.partial(_hgrn2_kernel, layer),
        grid=(bsz, nt),
        in_specs=[pl.BlockSpec((HG_TILE, 4 * n), lambda b, t: (b * nt + t, 0)),
                  pl.BlockSpec(lbp.shape, lambda b, t: (0, 0)),
                  pl.BlockSpec(nw.shape, lambda b, t: (0, 0))],
        out_specs=pl.BlockSpec((HG_TILE, n), lambda b, t: (b * nt + t, 0)),
        out_shape=jax.ShapeDtypeStruct((bsz * seq, n), BF16),
        scratch_shapes=[pltpu.VMEM((n, n), F32)],
        compiler_params=_params("parallel", "arbitrary"),
        name="hgrn2",
    )(z, lbp, nw)


def _moba_kernel(nblk, qt_ref, k_ref, vt_ref, km_ref, kx_ref, o_ref, qa_ref, sa_ref, sb_ref,
                 m_ref, acc_ref):
    j_lo = pl.program_id(2) * MOBA_QBLOCKS
    hp = pl.program_id(1)
    blk = MOBA_BLOCK
    hd = ATT_HD
    pair = 2 * hd
    span = MOBA_GROUP * blk
    scale = 1.0 / math.sqrt(hd)
    qw = MOBA_QBLOCKS * blk
    feat = lax.broadcasted_iota(jnp.int32, (pair, 1), 0)
    n_t = lax.broadcasted_iota(jnp.int32, (nblk, qw), 0)
    b_row = lax.broadcasted_iota(jnp.int32, (SUBLANES, qw), 0)
    j = j_lo + lax.broadcasted_iota(jnp.int32, (1, qw), 1) // blk
    causal = (lax.broadcasted_iota(jnp.int32, (blk, blk), 0)
              <= lax.broadcasted_iota(jnp.int32, (blk, blk), 1))
    qt = qt_ref[...]
    km = km_ref[...]

    def keys(start, size, pr):
        k_pair = k_ref[pl.ds(start, size), pr * pair:(pr + 1) * pair]
        return jnp.concatenate([k_pair, kx_ref[pl.ds(start, size), :]], axis=1)

    pad = jnp.zeros((pair - nblk - SUBLANES, qw), F32)
    q_own = []
    for h in range(MOBA_HEADS):
        pr = h // 2
        slope = jnp.exp2(-8.0 * (MOBA_HEADS * hp + h + 1).astype(F32) / ATT_HEADS)
        qm = jnp.where(feat // hd == h % 2, qt[pr * pair:(pr + 1) * pair], 0.0)
        gate = jnp.dot(km[:, pr * pair:(pr + 1) * pair], qm, precision=HIGHEST,
                       preferred_element_type=F32)
        gate = jnp.where(n_t < j, gate, NEG)
        chosen = jnp.zeros((nblk, qw), F32)
        for r in range(MOBA_TOPK):
            top = jnp.max(gate, axis=0, keepdims=True)
            first = jnp.min(jnp.where(gate == top, n_t, nblk), axis=0, keepdims=True)
            pick = n_t == first
            chosen = jnp.where(pick, jnp.where(j > r, 1.0, 0.0), chosen)
            gate = jnp.where(pick, -jnp.inf, gate)
        bias = jnp.zeros((SUBLANES, qw), F32)
        for i, piece in enumerate(LOG2E_PIECES):
            bias = jnp.where(b_row == i, slope * piece, bias)
            bias = jnp.where(b_row == len(LOG2E_PIECES) + i, slope * (piece * blk), bias)
        qs = (qm * (scale * LOG2E)).astype(BF16)
        past = jnp.concatenate([jnp.where(chosen > 0.0, 0.0, NEG), bias, pad], axis=0)
        own = jnp.concatenate([jnp.where(n_t == j, 0.0, NEG), bias, pad], axis=0)
        qa_ref[h] = jnp.concatenate([qs, past.astype(BF16)], axis=0)
        q_own.append(jnp.concatenate([qs, own.astype(BF16)], axis=0))

    def score(g, s_ref):
        for pr in range(MOBA_HEADS // 2):
            k_g = keys(pl.multiple_of(g * span, span), span, pr)
            for h in (2 * pr, 2 * pr + 1):
                s_ref[h] = _dot(k_g, qa_ref[h])

    def attend(g, s_ref):
        vt_g = vt_ref[:, pl.ds(pl.multiple_of(g * span, span), span)]
        for h in range(MOBA_HEADS):
            s = s_ref[h]
            m_old = m_ref[h]
            m_new = jnp.maximum(m_old, jnp.max(s, axis=0, keepdims=True))
            p = jnp.exp2((s - m_new).astype(BF16))
            m_ref[h] = m_new
            pv = _dot(vt_g[h * V_ROWS:(h + 1) * V_ROWS, :], p)
            acc_ref[h] = jnp.exp2(m_old - m_new) * acc_ref[h] + pv

    last = nblk // MOBA_GROUP - 1

    def chain(g0, count, more):
        for t in range(count):
            cur, nxt = (sa_ref, sb_ref) if t % 2 == 0 else (sb_ref, sa_ref)
            if t + 1 < count or more:
                score(jnp.minimum(g0 + t + 1, last), nxt)
            attend(g0 + t, cur)

    j_hi = j_lo + MOBA_QBLOCKS - 1
    pairs = (j_hi + 2 * MOBA_GROUP - 1) // (2 * MOBA_GROUP)
    nlong = pairs // (MOBA_LONG // 2)
    left = pairs % (MOBA_LONG // 2)
    score(0, sa_ref)

    for qb in range(MOBA_QBLOCKS):
        cols = slice(qb * blk, (qb + 1) * blk)
        j0 = pl.multiple_of((j_lo + qb) * blk, blk)
        vt_own = vt_ref[:, pl.ds(j0, blk)]
        for h in range(MOBA_HEADS):
            s = jnp.where(causal, _dot(keys(j0, blk, h // 2), q_own[h][:, cols]), NEG)
            m = jnp.max(s, axis=0, keepdims=True)
            p = jnp.exp2((s - m).astype(BF16))
            m_ref[h, :, cols] = m
            acc_ref[h, :, cols] = _dot(vt_own[h * V_ROWS:(h + 1) * V_ROWS, :], p)

    def body(i, carry):
        chain(i * MOBA_LONG, MOBA_LONG, True)
        return carry

    lax.fori_loop(0, nlong, body, 0)
    g0 = nlong * MOBA_LONG
    size = MOBA_LONG // 2
    while size >= 2:
        take = (left & (size // 2)) != 0

        @pl.when(take)
        def _(g0=g0, size=size):
            chain(g0, size, size > 2)

        g0 = g0 + jnp.where(take, size, 0)
        size //= 2

    out = []
    for h in range(MOBA_HEADS):
        acc = acc_ref[h]
        out.append(acc[0:hd] / acc[hd:hd + 1])
    o_ref[...] = jnp.concatenate(out, axis=0).T


def _moba(qt, k, vt, km, bsz, seq):
    nblk = seq // MOBA_BLOCK
    blk = MOBA_BLOCK
    pair = 2 * ATT_HD
    width = MOBA_HEADS * ATT_HD
    assert nblk + SUBLANES <= pair and 2 * len(LOG2E_PIECES) <= SUBLANES
    assert MOBA_LONG & (MOBA_LONG - 1) == 0 and MOBA_LONG >= 2
    assert nblk % (2 * MOBA_GROUP) == 0 and nblk % MOBA_QBLOCKS == 0
    nq = nblk // MOBA_QBLOCKS
    qw = MOBA_QBLOCKS * blk
    pos = jnp.arange(seq, dtype=jnp.int32)[:, None]
    col = jnp.arange(pair, dtype=jnp.int32)[None, :]
    extras = [pos % blk] * len(LOG2E_PIECES) + [pos // blk] * len(LOG2E_PIECES)
    kx = (pos // blk == col).astype(F32)
    for i, e in enumerate(extras):
        kx = jnp.where(col == nblk + i, e.astype(F32), kx)
    kx = kx.astype(BF16)
    return pl.pallas_call(
        functools.partial(_moba_kernel, nblk),
        grid=(bsz, ATT_HEADS // MOBA_HEADS, nq),
        in_specs=[pl.BlockSpec((width, qw), lambda b, p, j: (p, b * nq + j)),
                  pl.BlockSpec((seq, width), lambda b, p, j: (b, p)),
                  pl.BlockSpec((MOBA_HEADS * V_ROWS, seq), lambda b, p, j: (p, b)),
                  pl.BlockSpec((nblk, width), lambda b, p, j: (b, p)),
                  pl.BlockSpec((seq, pair), lambda b, p, j: (0, 0))],
        out_specs=pl.BlockSpec((qw, width), lambda b, p, j: (b * nq + j, p)),
        out_shape=jax.ShapeDtypeStruct((bsz * seq, ATT_HEADS * ATT_HD), F32),
        scratch_shapes=[pltpu.VMEM((MOBA_HEADS, 2 * pair, qw), BF16),
                        pltpu.VMEM((MOBA_HEADS, MOBA_GROUP * blk, qw), F32),
                        pltpu.VMEM((MOBA_HEADS, MOBA_GROUP * blk, qw), F32),
                        pltpu.VMEM((MOBA_HEADS, 1, qw), F32),
                        pltpu.VMEM((MOBA_HEADS, V_ROWS, qw), F32)],
        compiler_params=_params("parallel", "parallel", "arbitrary"),
        name="moba",
    )(qt, k, vt, km, kx)


def _out_mlp_kernel(final, x_ref, ya_ref, yb_ref, yc_ref, an_ref, wo_ref, gm_ref, w1_ref, w2_ref,
                    gf_ref, o_ref):
    yc = _rms(yc_ref[...], an_ref[...]).astype(BF16)
    y = jnp.concatenate([ya_ref[...], yb_ref[...], yc], axis=1)
    x = x_ref[...] + _dot(y, wo_ref[...])
    hb = _rms(x, gm_ref[...]).astype(BF16)
    mlp = None
    for c in range(w1_ref.shape[1] // FF_CHUNK):
        cols = slice(c * FF_CHUNK, (c + 1) * FF_CHUNK)
        u = jnp.square(jnp.maximum(_dot(hb, w1_ref[:, cols]), 0.0))
        part = _dot(u.astype(BF16), w2_ref[cols, :])
        mlp = part if mlp is None else mlp + part
    x = x + mlp
    if final:
        x = _rms(x, gf_ref[...])
    o_ref[...] = x


def _out_mlp(x2, ya, yb, yc, an, wo, gm, w1, w2, gf, final):
    t, d = x2.shape
    row = lambda a: pl.BlockSpec((ROW_TILE, a.shape[1]), lambda i: (i, 0))
    full = lambda a: pl.BlockSpec(a.shape, lambda i: (0, 0))
    return pl.pallas_call(
        functools.partial(_out_mlp_kernel, final),
        grid=(t // ROW_TILE,),
        in_specs=[row(x2), row(ya), row(yb), row(yc), full(an), full(wo), full(gm), full(w1),
                  full(w2), full(gf)],
        out_specs=pl.BlockSpec((ROW_TILE, d), lambda i: (i, 0)),
        out_shape=jax.ShapeDtypeStruct((t, d), F32),
        compiler_params=_params("parallel"),
        name="out_mlp",
    )(x2, ya, yb, yc, an, wo, gm, w1, w2, gf)


def _block_diag(w):
    g, n, _ = w.shape
    eye = jnp.eye(g, dtype=w.dtype)
    return (eye[:, None, :, None] * w[:, :, None, :]).reshape(g * n, g * n)


def kernel(x, w_in, w_out, norm_mix, norm_mlp, lru_conv_w, lru_conv_b, lru_wa, lru_ba, lru_wx, lru_bx,
           lru_lambda, hg_lower_bounds, hg_norm_w, lru_out_norm, att_out_norm, w_ff1, w_ff2, norm_final):
    bsz, seq, d = x.shape
    depth = w_in.shape[0]
    x2 = x.reshape(bsz * seq, d)
    r2 = lambda a: a.reshape(1, -1)
    for l in range(depth):
        w = w_in[l].astype(BF16)
        wt = jnp.concatenate([w[:, 1536:2048], w[:, 2560:3072]], axis=1).T
        lru, hg, cqt, ck, cvt, km = _in_proj(x2, r2(norm_mix[l]), w, wt)
        wg = jnp.concatenate([_block_diag(lru_wa[l]), _block_diag(lru_wx[l])], axis=1).astype(BF16)
        bg = jnp.concatenate([lru_ba[l], lru_bx[l]]).reshape(1, -1)
        ya = _rglru(lru, bsz, seq, lru_conv_w[l], r2(lru_conv_b[l]), wg, bg, r2(lru_lambda[l]),
                    r2(lru_out_norm[l]))
        yb = _hgrn2(hg, bsz, seq, hg_lower_bounds, r2(jnp.tile(hg_norm_w[l], HG_HEADS)), l)
        yc = _moba(cqt, ck, cvt, km.reshape(-1, km.shape[-1]), bsz, seq)
        x2 = _out_mlp(x2, ya, yb, yc, r2(att_out_norm[l]), w_out[l].astype(BF16), r2(norm_mlp[l]),
                      w_ff1[l].astype(BF16), w_ff2[l].astype(BF16), r2(norm_final), l == depth - 1)
    return x2.reshape(bsz, seq, d)
```

```python
import functools
import math
import struct

import jax
import jax.numpy as jnp
from jax import lax
from jax.experimental import pallas as pl
from jax.experimental.pallas import tpu as pltpu

F32 = jnp.float32
BF16 = jnp.bfloat16
HIGHEST = lax.Precision.HIGHEST

EPS = 1e-6
NEG = -1e30
TINY = 1e-30

LRU_WIDTH = 256
LRU_BLOCKS = 4
LRU_CONV = 4
LRU_C = 8.0
HG_HEADS = 4
HG_DK = 64
ATT_HEADS = 8
ATT_HD = 64
MOBA_BLOCK = 256
MOBA_TOPK = 3

SUBLANES = 8
VMEM_LIMIT = 56 * 1024 * 1024

ROW_TILE = 512
LRU_TILE = 1024
LRU_SUB = 256
HG_TILE = 512
HG_CHUNK = 256
FF_CHUNK = 1024
MOBA_GROUP = 2
MOBA_LONG = 8
MOBA_QBLOCKS = 1
MOBA_HEADS = 2
ONES_ROWS = 16
V_ROWS = ATT_HD + ONES_ROWS


def _bf16_pieces(x, n):
    pieces = []
    for _ in range(n):
        bits = struct.unpack("<I", struct.pack("<f", x))[0]
        bits = (bits + 0x7FFF + ((bits >> 16) & 1)) & 0xFFFF0000
        piece = struct.unpack("<f", struct.pack("<I", bits))[0]
        pieces.append(piece)
        x -= piece
    return tuple(pieces)


LOG2E = math.log2(math.e)
LOG2E_PIECES = _bf16_pieces(LOG2E, 3)


def _dot(a, b):
    return jnp.dot(a, b, preferred_element_type=F32)


def _dot_t(a, b, precision=None):
    return lax.dot_general(a, b, (((1,), (1,)), ((), ())), precision=precision,
                           preferred_element_type=F32)


def _rms(x, g):
    return x * lax.rsqrt(jnp.mean(x * x, axis=-1, keepdims=True) + EPS) * g


def _sigmoid(x):
    return 1.0 / (1.0 + jnp.exp(-x))


def _softplus(x):
    return jnp.maximum(x, 0.0) + jnp.log(1.0 + jnp.exp(-jnp.abs(x)))


def _params(*sem):
    return pltpu.CompilerParams(dimension_semantics=sem, vmem_limit_bytes=VMEM_LIMIT)


def _in_proj_kernel(x_ref, g_ref, w_ref, wt_ref, lru_ref, hg_ref, qt_ref, k_ref, vt_ref, km_ref):
    hb = _rms(x_ref[...], g_ref[...]).astype(BF16)
    lru_ref[...] = _dot(hb, w_ref[:, 0:512])
    hg_ref[...] = _dot(hb, w_ref[:, 512:1536])
    qt_ref[...] = _dot_t(wt_ref[0:512, :], hb)
    k = _dot(hb, w_ref[:, 2048:2560])
    k_ref[...] = k.astype(BF16)
    for i in range(ROW_TILE // MOBA_BLOCK):
        km_ref[i] = jnp.mean(k[i * MOBA_BLOCK:(i + 1) * MOBA_BLOCK], axis=0, keepdims=True)
    vt = _dot_t(wt_ref[512:1024, :], hb).astype(BF16)
    ones = jnp.ones((ONES_ROWS, ROW_TILE), BF16)
    for h in range(ATT_HEADS):
        vt_ref[h * V_ROWS:h * V_ROWS + ATT_HD, :] = vt[h * ATT_HD:(h + 1) * ATT_HD, :]
        vt_ref[h * V_ROWS + ATT_HD:(h + 1) * V_ROWS, :] = ones


def _in_proj(x2, g, w, wt):
    t, d = x2.shape
    nblk = t // MOBA_BLOCK
    row = lambda width: pl.BlockSpec((ROW_TILE, width), lambda i: (i, 0))
    col = lambda height: pl.BlockSpec((height, ROW_TILE), lambda i: (0, i))
    return pl.pallas_call(
        _in_proj_kernel,
        grid=(t // ROW_TILE,),
        in_specs=[row(d), pl.BlockSpec((1, d), lambda i: (0, 0)),
                  pl.BlockSpec(w.shape, lambda i: (0, 0)), pl.BlockSpec(wt.shape, lambda i: (0, 0))],
        out_specs=[row(512), row(1024), col(512), row(512), col(ATT_HEADS * V_ROWS),
                   pl.BlockSpec((ROW_TILE // MOBA_BLOCK, 1, 512), lambda i: (i, 0, 0))],
        out_shape=[jax.ShapeDtypeStruct((t, 512), F32), jax.ShapeDtypeStruct((t, 1024), F32),
                   jax.ShapeDtypeStruct((512, t), F32), jax.ShapeDtypeStruct((t, 512), BF16),
                   jax.ShapeDtypeStruct((ATT_HEADS * V_ROWS, t), BF16),
                   jax.ShapeDtypeStruct((nblk, 1, 512), F32)],
        compiler_params=_params("parallel"),
        name="in_proj",
    )(x2, g, w, wt)


def _rglru_kernel(xy_ref, cw_ref, cb_ref, wg_ref, bg_ref, lam_ref, nw_ref, o_ref, prev_ref, h_ref):
    @pl.when(pl.program_id(1) == 0)
    def _():
        prev_ref[...] = jnp.zeros_like(prev_ref)
        h_ref[...] = jnp.zeros_like(h_ref)

    w = LRU_WIDTH
    row = lax.broadcasted_iota(jnp.int32, (LRU_SUB, w), 0)
    sp = _softplus(-lam_ref[...])
    for c in range(LRU_TILE // LRU_SUB):
        rows = slice(c * LRU_SUB, (c + 1) * LRU_SUB)
        xb = xy_ref[rows, 0:w]
        yb = xy_ref[rows, w:2 * w]
        prev = prev_ref[...]
        xc = xb * cw_ref[LRU_CONV - 1:LRU_CONV, :] + cb_ref[...]
        for k in range(1, LRU_CONV):
            shifted = jnp.where(row >= k, pltpu.roll(xb, k, 0), pltpu.roll(prev, k, 0))
            xc = xc + shifted * cw_ref[LRU_CONV - 1 - k:LRU_CONV - k, :]
        prev_ref[...] = xb
        gates = _dot(xc.astype(BF16), wg_ref[...]) + bg_ref[...]
        r = _sigmoid(gates[:, 0:w])
        i = _sigmoid(gates[:, w:2 * w])
        log_a = -LRU_C * r * sp
        a = jnp.exp(log_a)
        b = jnp.sqrt(1.0 - jnp.exp(2.0 * log_a)) * (i * xc)
        s = 1
        while s < LRU_SUB:
            keep = row >= s
            a_s = jnp.where(keep, pltpu.roll(a, s, 0), 1.0)
            b_s = jnp.where(keep, pltpu.roll(b, s, 0), 0.0)
            b = a * b_s + b
            a = a * a_s
            s *= 2
        h = b + a * h_ref[...]
        h_ref[...] = h[LRU_SUB - 1:LRU_SUB, :]
        y = h * jax.nn.gelu(yb, approximate=True)
        o_ref[rows, :] = _rms(y, nw_ref[...]).astype(o_ref.dtype)


def _rglru(xy, bsz, seq, cw, cb, wg, bg, lam, nw):
    nt = seq // LRU_TILE
    w = LRU_WIDTH
    vec = lambda a: pl.BlockSpec(a.shape, lambda b, t: (0, 0))
    return pl.pallas_call(
        _rglru_kernel,
        grid=(bsz, nt),
        in_specs=[pl.BlockSpec((LRU_TILE, 2 * w), lambda b, t: (b * nt + t, 0)),
                  vec(cw), vec(cb), vec(wg), vec(bg), vec(lam), vec(nw)],
        out_specs=pl.BlockSpec((LRU_TILE, w), lambda b, t: (b * nt + t, 0)),
        out_shape=jax.ShapeDtypeStruct((bsz * seq, w), BF16),
        scratch_shapes=[pltpu.VMEM((LRU_SUB, w), F32), pltpu.VMEM((1, w), F32)],
        compiler_params=_params("parallel", "arbitrary"),
        name="rglru",
    )(xy, cw, cb, wg, bg, lam, nw)


def _hgrn2_kernel(layer, z_ref, lbp_ref, nw_ref, o_ref, state_ref):
    @pl.when(pl.program_id(1) == 0)
    def _():
        state_ref[...] = jnp.zeros_like(state_ref)

    n = HG_HEADS * HG_DK
    L = HG_CHUNK
    lbp = lbp_ref[...]
    e = jnp.exp(lbp - jnp.max(lbp, axis=0, keepdims=True))
    soft = e / jnp.sum(e, axis=0, keepdims=True)
    lb = jnp.sum(soft[0:layer + 1], axis=0, keepdims=True) - soft[0:1]
    log_lb = jnp.log(jnp.maximum(lb, TINY))
    log_1m = jnp.log1p(-lb)

    lane_h = lax.broadcasted_iota(jnp.int32, (1, n), 1) // HG_DK
    head_of_row = lax.broadcasted_iota(jnp.int32, (n, n), 0) // HG_DK
    head_of_col = lax.broadcasted_iota(jnp.int32, (n, n), 1) // HG_DK
    same_head = head_of_row == head_of_col
    ones_bd = jnp.where(same_head, 1.0, 0.0).astype(BF16)
    row = lax.broadcasted_iota(jnp.int32, (L, n), 0)
    sub = lax.broadcasted_iota(jnp.int32, (1, SUBLANES, 1), 1)
    t_idx = lax.broadcasted_iota(jnp.int32, (L, L), 0)
    s_idx = lax.broadcasted_iota(jnp.int32, (L, L), 1)

    def chunk(ci, carry):
        r0 = pl.multiple_of(ci * L, L)
        q = z_ref[pl.ds(r0, L), 0:n]
        f = z_ref[pl.ds(r0, L), n:2 * n]
        v = z_ref[pl.ds(r0, L), 2 * n:3 * n]
        g = z_ref[pl.ds(r0, L), 3 * n:4 * n]
        lsig = -_softplus(-f)
        t1 = log_1m + lsig
        log_f = jnp.maximum(log_lb, t1) + jnp.log(1.0 + jnp.exp(-jnp.abs(log_lb - t1)))
        kk = (1.0 - lb) * _sigmoid(-f)
        qs = q * _sigmoid(q)
        b = log_f * LOG2E
        s = 1
        while s < L:
            b = b + jnp.where(row >= s, pltpu.roll(b, s, 0), 0.0)
            s *= 2

        g8 = L // SUBLANES
        q3 = qs.reshape(g8, SUBLANES, n)
        k3 = kk.reshape(g8, SUBLANES, n)
        b3 = b.reshape(g8, SUBLANES, n)
        v3 = v.reshape(g8, SUBLANES, n)
        o = jnp.zeros((L, n), F32)
        for j in range(SUBLANES):
            d = jnp.where(sub >= j, b3 - b3[:, j:j + 1, :], NEG)
            xj = q3 * k3[:, j:j + 1, :] * jnp.exp2(d)
            aj = _dot(xj.reshape(L, n).astype(BF16), ones_bd)
            o = o + (aj.reshape(g8, SUBLANES, n) * v3[:, j:j + 1, :]).reshape(L, n)

        att = [jnp.zeros((L, L), F32) for _ in range(HG_HEADS)]
        c = SUBLANES
        while c < L:
            bc = b.reshape(L // c, c, n)
            bend = jnp.broadcast_to(bc[:, c - 1:c, :], (L // c, c, n)).reshape(L, n)
            bprev = jnp.where(row >= c, pltpu.roll(bend, c, 0), 0.0)
            qt = qs * jnp.exp2(b - bprev)
            kt = (kk * jnp.exp2(bend - b)).astype(BF16)
            pair = ((t_idx // c) % 2 == 1) & ((s_idx // c) == (t_idx // c) - 1)
            for h in range(HG_HEADS):
                sc = _dot_t(jnp.where(lane_h == h, qt, 0.0).astype(BF16), kt)
                att[h] = att[h] + jnp.where(pair, sc, 0.0)
            c *= 2
        att_cat = jnp.concatenate([a.astype(BF16) for a in att], axis=1)
        v_stack = jnp.concatenate([jnp.where(lane_h == h, v, 0.0).astype(BF16)
                                   for h in range(HG_HEADS)], axis=0)
        o = o + _dot(att_cat, v_stack)

        state = state_ref[...]
        o = o + _dot_t((qs * jnp.exp2(b)).astype(BF16), state.astype(BF16))
        b_last = b[L - 1:L, :]
        kdec = (kk * jnp.exp2(b_last - b)).astype(BF16)
        upd = lax.dot_general(v.astype(BF16), kdec, (((0,), (0,)), ((), ())),
                              preferred_element_type=F32)
        state_ref[...] = state * jnp.exp2(b_last) + jnp.where(same_head, upd, 0.0)

        ms = _dot((o * o).astype(BF16), ones_bd) * (1.0 / HG_DK)
        y = o * lax.rsqrt(ms + EPS) * nw_ref[...] * (g * _sigmoid(g))
        o_ref[pl.ds(r0, L), :] = y.astype(o_ref.dtype)
        return carry

    lax.fori_loop(0, HG_TILE // L, chunk, 0)


def _hgrn2(z, bsz, seq, lbp, nw, layer):
    nt = seq // HG_TILE
    n = HG_HEADS * HG_DK
    return pl.pallas_call(
        functools.partial(_hgrn2_kernel, layer),
        grid=(bsz, nt),
        in_specs=[pl.BlockSpec((HG_TILE, 4 * n), lambda b, t: (b * nt + t, 0)),
                  pl.BlockSpec(lbp.shape, lambda b, t: (0, 0)),
                  pl.BlockSpec(nw.shape, lambda b, t: (0, 0))],
        out_specs=pl.BlockSpec((HG_TILE, n), lambda b, t: (b * nt + t, 0)),
        out_shape=jax.ShapeDtypeStruct((bsz * seq, n), BF16),
        scratch_shapes=[pltpu.VMEM((n, n), F32)],
        compiler_params=_params("parallel", "arbitrary"),
        name="hgrn2",
    )(z, lbp, nw)


def _moba_kernel(nblk, qt_ref, k_ref, vt_ref, km_ref, kx_ref, o_ref, qa_ref, sa_ref, sb_ref,
                 ta_ref, tb_ref, m_ref, acc_ref):
    j_lo = pl.program_id(2) * MOBA_QBLOCKS
    hp = pl.program_id(1)
    blk = MOBA_BLOCK
    hd = ATT_HD
    pair = 2 * hd
    span = MOBA_GROUP * blk
    scale = 1.0 / math.sqrt(hd)
    qw = MOBA_QBLOCKS * blk
    feat = lax.broadcasted_iota(jnp.int32, (pair, 1), 0)
    n_t = lax.broadcasted_iota(jnp.int32, (nblk, qw), 0)
    b_row = lax.broadcasted_iota(jnp.int32, (SUBLANES, qw), 0)
    j = j_lo + lax.broadcasted_iota(jnp.int32, (1, qw), 1) // blk
    causal = (lax.broadcasted_iota(jnp.int32, (blk, blk), 0)
              <= lax.broadcasted_iota(jnp.int32, (blk, blk), 1))
    qt = qt_ref[...]
    km = km_ref[...]

    def keys(start, size, pr):
        k_pair = k_ref[pl.ds(start, size), pr * pair:(pr + 1) * pair]
        return jnp.concatenate([k_pair, kx_ref[pl.ds(start, size), :]], axis=1)

    pad = jnp.zeros((pair - nblk - SUBLANES, qw), F32)
    q_own = []
    for h in range(MOBA_HEADS):
        pr = h // 2
        slope = jnp.exp2(-8.0 * (MOBA_HEADS * hp + h + 1).astype(F32) / ATT_HEADS)
        qm = jnp.where(feat // hd == h % 2, qt[pr * pair:(pr + 1) * pair], 0.0)
        gate = jnp.dot(km[:, pr * pair:(pr + 1) * pair], qm, precision=HIGHEST,
                       preferred_element_type=F32)
        gate = jnp.where(n_t < j, gate, NEG)
        chosen = jnp.zeros((nblk, qw), F32)
        for r in range(MOBA_TOPK):
            top = jnp.max(gate, axis=0, keepdims=True)
            first = jnp.min(jnp.where(gate == top, n_t, nblk), axis=0, keepdims=True)
            pick = n_t == first
            chosen = jnp.where(pick, jnp.where(j > r, 1.0, 0.0), chosen)
            gate = jnp.where(pick, -jnp.inf, gate)
        bias = jnp.zeros((SUBLANES, qw), F32)
        for i, piece in enumerate(LOG2E_PIECES):
            bias = jnp.where(b_row == i, slope * piece, bias)
            bias = jnp.where(b_row == len(LOG2E_PIECES) + i, slope * (piece * blk), bias)
        qs = (qm * (scale * LOG2E)).astype(BF16)
        past = jnp.concatenate([jnp.where(chosen > 0.0, 0.0, NEG), bias, pad], axis=0)
        own = jnp.concatenate([jnp.where(n_t == j, 0.0, NEG), bias, pad], axis=0)
        qa_ref[h] = jnp.concatenate([qs, past.astype(BF16)], axis=0)
        q_own.append(jnp.concatenate([qs, own.astype(BF16)], axis=0))

    def score(g, buf):
        s_ref, top_ref = buf
        for pr in range(MOBA_HEADS // 2):
            k_g = keys(pl.multiple_of(g * span, span), span, pr)
            for h in (2 * pr, 2 * pr + 1):
                s = _dot(k_g, qa_ref[h])
                s_ref[h] = s
                top_ref[h] = jnp.max(s, axis=0, keepdims=True)

    def attend(g, buf):
        s_ref, top_ref = buf
        vt_g = vt_ref[:, pl.ds(pl.multiple_of(g * span, span), span)]
        for h in range(MOBA_HEADS):
            m_old = m_ref[h]
            m_new = jnp.maximum(m_old, top_ref[h])
            p = jnp.exp2((s_ref[h] - m_new).astype(BF16))
            m_ref[h] = m_new
            pv = _dot(vt_g[h * V_ROWS:(h + 1) * V_ROWS, :], p)
            acc_ref[h] = jnp.exp2(m_old - m_new) * acc_ref[h] + pv

    last = nblk // MOBA_GROUP - 1
    buf_a = (sa_ref, ta_ref)
    buf_b = (sb_ref, tb_ref)

    def chain(g0, count, more):
        for t in range(count):
            cur, nxt = (buf_a, buf_b) if t % 2 == 0 else (buf_b, buf_a)
            if t + 1 < count or more:
                score(jnp.minimum(g0 + t + 1, last), nxt)
            attend(g0 + t, cur)

    j_hi = j_lo + MOBA_QBLOCKS - 1
    pairs = (j_hi + 2 * MOBA_GROUP - 1) // (2 * MOBA_GROUP)
    nlong = pairs // (MOBA_LONG // 2)
    left = pairs % (MOBA_LONG // 2)
    score(0, buf_a)

    for qb in range(MOBA_QBLOCKS):
        cols = slice(qb * blk, (qb + 1) * blk)
        j0 = pl.multiple_of((j_lo + qb) * blk, blk)
        vt_own = vt_ref[:, pl.ds(j0, blk)]
        for h in range(MOBA_HEADS):
            s = jnp.where(causal, _dot(keys(j0, blk, h // 2), q_own[h][:, cols]), NEG)
            m = jnp.max(s, axis=0, keepdims=True)
            p = jnp.exp2((s - m).astype(BF16))
            m_ref[h, :, cols] = m
            acc_ref[h, :, cols] = _dot(vt_own[h * V_ROWS:(h + 1) * V_ROWS, :], p)

    def body(i, carry):
        chain(i * MOBA_LONG, MOBA_LONG, True)
        return carry

    lax.fori_loop(0, nlong, body, 0)
    g0 = nlong * MOBA_LONG
    size = MOBA_LONG // 2
    while size >= 2:
        take = (left & (size // 2)) != 0

        @pl.when(take)
        def _(g0=g0, size=size):
            chain(g0, size, size > 2)

        g0 = g0 + jnp.where(take, size, 0)
        size //= 2

    out = []
    for h in range(MOBA_HEADS):
        acc = acc_ref[h]
        out.append(acc[0:hd] / acc[hd:hd + 1])
    o_ref[...] = jnp.concatenate(out, axis=0).T


def _moba(qt, k, vt, km, bsz, seq):
    nblk = seq // MOBA_BLOCK
    blk = MOBA_BLOCK
    pair = 2 * ATT_HD
    width = MOBA_HEADS * ATT_HD
    assert nblk + SUBLANES <= pair and 2 * len(LOG2E_PIECES) <= SUBLANES
    assert MOBA_LONG & (MOBA_LONG - 1) == 0 and MOBA_LONG >= 2
    assert nblk % (2 * MOBA_GROUP) == 0 and nblk % MOBA_QBLOCKS == 0
    nq = nblk // MOBA_QBLOCKS
    qw = MOBA_QBLOCKS * blk
    pos = jnp.arange(seq, dtype=jnp.int32)[:, None]
    col = jnp.arange(pair, dtype=jnp.int32)[None, :]
    extras = [pos % blk] * len(LOG2E_PIECES) + [pos // blk] * len(LOG2E_PIECES)
    kx = (pos // blk == col).astype(F32)
    for i, e in enumerate(extras):
        kx = jnp.where(col == nblk + i, e.astype(F32), kx)
    kx = kx.astype(BF16)
    return pl.pallas_call(
        functools.partial(_moba_kernel, nblk),
        grid=(bsz, ATT_HEADS // MOBA_HEADS, nq),
        in_specs=[pl.BlockSpec((width, qw), lambda b, p, j: (p, b * nq + j)),
                  pl.BlockSpec((seq, width), lambda b, p, j: (b, p)),
                  pl.BlockSpec((MOBA_HEADS * V_ROWS, seq), lambda b, p, j: (p, b)),
                  pl.BlockSpec((nblk, width), lambda b, p, j: (b, p)),
                  pl.BlockSpec((seq, pair), lambda b, p, j: (0, 0))],
        out_specs=pl.BlockSpec((qw, width), lambda b, p, j: (b * nq + j, p)),
        out_shape=jax.ShapeDtypeStruct((bsz * seq, ATT_HEADS * ATT_HD), F32),
        scratch_shapes=[pltpu.VMEM((MOBA_HEADS, 2 * pair, qw), BF16),
                        pltpu.VMEM((MOBA_HEADS, MOBA_GROUP * blk, qw), F32),
                        pltpu.VMEM((MOBA_HEADS, MOBA_GROUP * blk, qw), F32),
                        pltpu.VMEM((MOBA_HEADS, 1, qw), F32),
                        pltpu.VMEM((MOBA_HEADS, 1, qw), F32),
                        pltpu.VMEM((MOBA_HEADS, 1, qw), F32),
                        pltpu.VMEM((MOBA_HEADS, V_ROWS, qw), F32)],
        compiler_params=_params("parallel", "parallel", "arbitrary"),
        name="moba",
    )(qt, k, vt, km, kx)


def _out_mlp_kernel(final, x_ref, ya_ref, yb_ref, yc_ref, an_ref, wo_ref, gm_ref, w1_ref, w2_ref,
                    gf_ref, o_ref):
    yc = _rms(yc_ref[...], an_ref[...]).astype(BF16)
    y = jnp.concatenate([ya_ref[...], yb_ref[...], yc], axis=1)
    x = x_ref[...] + _dot(y, wo_ref[...])
    hb = _rms(x, gm_ref[...]).astype(BF16)
    mlp = None
    for c in range(w1_ref.shape[1] // FF_CHUNK):
        cols = slice(c * FF_CHUNK, (c + 1) * FF_CHUNK)
        u = jnp.square(jnp.maximum(_dot(hb, w1_ref[:, cols]), 0.0))
        part = _dot(u.astype(BF16), w2_ref[cols, :])
        mlp = part if mlp is None else mlp + part
    x = x + mlp
    if final:
        x = _rms(x, gf_ref[...])
    o_ref[...] = x


def _out_mlp(x2, ya, yb, yc, an, wo, gm, w1, w2, gf, final):
    t, d = x2.shape
    row = lambda a: pl.BlockSpec((ROW_TILE, a.shape[1]), lambda i: (i, 0))
    full = lambda a: pl.BlockSpec(a.shape, lambda i: (0, 0))
    return pl.pallas_call(
        functools.partial(_out_mlp_kernel, final),
        grid=(t // ROW_TILE,),
        in_specs=[row(x2), row(ya), row(yb), row(yc), full(an), full(wo), full(gm), full(w1),
                  full(w2), full(gf)],
        out_specs=pl.BlockSpec((ROW_TILE, d), lambda i: (i, 0)),
        out_shape=jax.ShapeDtypeStruct((t, d), F32),
        compiler_params=_params("parallel"),
        name="out_mlp",
    )(x2, ya, yb, yc, an, wo, gm, w1, w2, gf)


def _block_diag(w):
    g, n, _ = w.shape
    eye = jnp.eye(g, dtype=w.dtype)
    return (eye[:, None, :, None] * w[:, :, None, :]).reshape(g * n, g * n)


def kernel(x, w_in, w_out, norm_mix, norm_mlp, lru_conv_w, lru_conv_b, lru_wa, lru_ba, lru_wx, lru_bx,
           lru_lambda, hg_lower_bounds, hg_norm_w, lru_out_norm, att_out_norm, w_ff1, w_ff2, norm_final):
    bsz, seq, d = x.shape
    depth = w_in.shape[0]
    x2 = x.reshape(bsz * seq, d)
    r2 = lambda a: a.reshape(1, -1)
    for l in range(depth):
        w = w_in[l].astype(BF16)
        wt = jnp.concatenate([w[:, 1536:2048], w[:, 2560:3072]], axis=1).T
        lru, hg, cqt, ck, cvt, km = _in_proj(x2, r2(norm_mix[l]), w, wt)
        wg = jnp.concatenate([_block_diag(lru_wa[l]), _block_diag(lru_wx[l])], axis=1).astype(BF16)
        bg = jnp.concatenate([lru_ba[l], lru_bx[l]]).reshape(1, -1)
        ya = _rglru(lru, bsz, seq, lru_conv_w[l], r2(lru_conv_b[l]), wg, bg, r2(lru_lambda[l]),
                    r2(lru_out_norm[l]))
        yb = _hgrn2(hg, bsz, seq, hg_lower_bounds, r2(jnp.tile(hg_norm_w[l], HG_HEADS)), l)
        yc = _moba(cqt, ck, cvt, km.reshape(-1, km.shape[-1]), bsz, seq)
        x2 = _out_mlp(x2, ya, yb, yc, r2(att_out_norm[l]), w_out[l].astype(BF16), r2(norm_mlp[l]),
                      w_ff1[l].astype(BF16), w_ff2[l].astype(BF16), r2(norm_final), l == depth - 1)
    return x2.reshape(bsz, seq, d)
```

```python
import functools
import math
import struct

import jax
import jax.numpy as jnp
from jax import lax
from jax.experimental import pallas as pl
from jax.experimental.pallas import tpu as pltpu

F32 = jnp.float32
BF16 = jnp.bfloat16
HIGHEST = lax.Precision.HIGHEST

EPS = 1e-6
NEG = -1e30
TINY = 1e-30

LRU_WIDTH = 256
LRU_BLOCKS = 4
LRU_CONV = 4
LRU_C = 8.0
HG_HEADS = 4
HG_DK = 64
ATT_HEADS = 8
ATT_HD = 64
MOBA_BLOCK = 256
MOBA_TOPK = 3

SUBLANES = 8
VMEM_LIMIT = 56 * 1024 * 1024

ROW_TILE = 512
LRU_TILE = 1024
LRU_SUB = 256
HG_TILE = 512
HG_CHUNK = 256
FF_CHUNK = 1024
MOBA_GROUP = 2
MOBA_LONG = 8
MOBA_QBLOCKS = 1
MOBA_HEADS = 2
ONES_ROWS = 16
V_ROWS = ATT_HD + ONES_ROWS


def _bf16_pieces(x, n):
    pieces = []
    for _ in range(n):
        bits = struct.unpack("<I", struct.pack("<f", x))[0]
        bits = (bits + 0x7FFF + ((bits >> 16) & 1)) & 0xFFFF0000
        piece = struct.unpack("<f", struct.pack("<I", bits))[0]
        pieces.append(piece)
        x -= piece
    return tuple(pieces)


LOG2E = math.log2(math.e)
LOG2E_PIECES = _bf16_pieces(LOG2E, 3)


def _dot(a, b):
    return jnp.dot(a, b, preferred_element_type=F32)


def _dot_t(a, b, precision=None):
    return lax.dot_general(a, b, (((1,), (1,)), ((), ())), precision=precision,
                           preferred_element_type=F32)


def _rms(x, g):
    return x * lax.rsqrt(jnp.mean(x * x, axis=-1, keepdims=True) + EPS) * g


def _sigmoid(x):
    return 1.0 / (1.0 + jnp.exp(-x))


def _softplus(x):
    return jnp.maximum(x, 0.0) + jnp.log(1.0 + jnp.exp(-jnp.abs(x)))


def _params(*sem):
    return pltpu.CompilerParams(dimension_semantics=sem, vmem_limit_bytes=VMEM_LIMIT)


def _in_proj_kernel(x_ref, g_ref, w_ref, wt_ref, lru_ref, hg_ref, qt_ref, k_ref, vt_ref, km_ref):
    hb = _rms(x_ref[...], g_ref[...]).astype(BF16)
    lru_ref[...] = _dot(hb, w_ref[:, 0:512])
    hg_ref[...] = _dot(hb, w_ref[:, 512:1536])
    qt_ref[...] = _dot_t(wt_ref[0:512, :], hb)
    k = _dot(hb, w_ref[:, 2048:2560])
    k_ref[...] = k.astype(BF16)
    for i in range(ROW_TILE // MOBA_BLOCK):
        km_ref[i] = jnp.mean(k[i * MOBA_BLOCK:(i + 1) * MOBA_BLOCK], axis=0, keepdims=True)
    vt = _dot_t(wt_ref[512:1024, :], hb).astype(BF16)
    ones = jnp.ones((ONES_ROWS, ROW_TILE), BF16)
    for h in range(ATT_HEADS):
        vt_ref[h * V_ROWS:h * V_ROWS + ATT_HD, :] = vt[h * ATT_HD:(h + 1) * ATT_HD, :]
        vt_ref[h * V_ROWS + ATT_HD:(h + 1) * V_ROWS, :] = ones


def _in_proj(x2, g, w, wt, layer):
    t, d = x2.shape
    stacked = lambda a: pl.BlockSpec((None,) + a.shape[1:], lambda i: (layer, 0, 0))
    nblk = t // MOBA_BLOCK
    row = lambda width: pl.BlockSpec((ROW_TILE, width), lambda i: (i, 0))
    col = lambda height: pl.BlockSpec((height, ROW_TILE), lambda i: (0, i))
    return pl.pallas_call(
        _in_proj_kernel,
        grid=(t // ROW_TILE,),
        in_specs=[row(d), pl.BlockSpec((1, d), lambda i: (0, 0)),
                  stacked(w), stacked(wt)],
        out_specs=[row(512), row(1024), col(512), row(512), col(ATT_HEADS * V_ROWS),
                   pl.BlockSpec((ROW_TILE // MOBA_BLOCK, 1, 512), lambda i: (i, 0, 0))],
        out_shape=[jax.ShapeDtypeStruct((t, 512), F32), jax.ShapeDtypeStruct((t, 1024), F32),
                   jax.ShapeDtypeStruct((512, t), F32), jax.ShapeDtypeStruct((t, 512), BF16),
                   jax.ShapeDtypeStruct((ATT_HEADS * V_ROWS, t), BF16),
                   jax.ShapeDtypeStruct((nblk, 1, 512), F32)],
        compiler_params=_params("parallel"),
        name="in_proj",
    )(x2, g, w, wt)


def _rglru_kernel(xy_ref, cw_ref, cb_ref, wg_ref, bg_ref, lam_ref, nw_ref, o_ref, prev_ref, h_ref):
    @pl.when(pl.program_id(1) == 0)
    def _():
        prev_ref[...] = jnp.zeros_like(prev_ref)
        h_ref[...] = jnp.zeros_like(h_ref)

    w = LRU_WIDTH
    row = lax.broadcasted_iota(jnp.int32, (LRU_SUB, w), 0)
    sp = _softplus(-lam_ref[...])
    for c in range(LRU_TILE // LRU_SUB):
        rows = slice(c * LRU_SUB, (c + 1) * LRU_SUB)
        xb = xy_ref[rows, 0:w]
        yb = xy_ref[rows, w:2 * w]
        prev = prev_ref[...]
        xc = xb * cw_ref[LRU_CONV - 1:LRU_CONV, :] + cb_ref[...]
        for k in range(1, LRU_CONV):
            shifted = jnp.where(row >= k, pltpu.roll(xb, k, 0), pltpu.roll(prev, k, 0))
            xc = xc + shifted * cw_ref[LRU_CONV - 1 - k:LRU_CONV - k, :]
        prev_ref[...] = xb
        gates = _dot(xc.astype(BF16), wg_ref[...]) + bg_ref[...]
        r = _sigmoid(gates[:, 0:w])
        i = _sigmoid(gates[:, w:2 * w])
        log_a = -LRU_C * r * sp
        a = jnp.exp(log_a)
        b = jnp.sqrt(jnp.maximum(1.0 - a * a, 0.0)) * (i * xc)
        s = 1
        while s < LRU_SUB:
            keep = row >= s
            a_s = jnp.where(keep, pltpu.roll(a, s, 0), 1.0)
            b_s = jnp.where(keep, pltpu.roll(b, s, 0), 0.0)
            b = a * b_s + b
            a = a * a_s
            s *= 2
        h = b + a * h_ref[...]
        h_ref[...] = h[LRU_SUB - 1:LRU_SUB, :]
        y = h * jax.nn.gelu(yb, approximate=True)
        o_ref[rows, :] = _rms(y, nw_ref[...]).astype(o_ref.dtype)


def _rglru(xy, bsz, seq, cw, cb, wg, bg, lam, nw):
    nt = seq // LRU_TILE
    w = LRU_WIDTH
    vec = lambda a: pl.BlockSpec(a.shape, lambda b, t: (0, 0))
    return pl.pallas_call(
        _rglru_kernel,
        grid=(bsz, nt),
        in_specs=[pl.BlockSpec((LRU_TILE, 2 * w), lambda b, t: (b * nt + t, 0)),
                  vec(cw), vec(cb), vec(wg), vec(bg), vec(lam), vec(nw)],
        out_specs=pl.BlockSpec((LRU_TILE, w), lambda b, t: (b * nt + t, 0)),
        out_shape=jax.ShapeDtypeStruct((bsz * seq, w), BF16),
        scratch_shapes=[pltpu.VMEM((LRU_SUB, w), F32), pltpu.VMEM((1, w), F32)],
        compiler_params=_params("parallel", "arbitrary"),
        name="rglru",
    )(xy, cw, cb, wg, bg, lam, nw)


def _hgrn2_kernel(layer, z_ref, lbp_ref, nw_ref, o_ref, state_ref):
    @pl.when(pl.program_id(1) == 0)
    def _():
        state_ref[...] = jnp.zeros_like(state_ref)

    n = HG_HEADS * HG_DK
    L = HG_CHUNK
    lbp = lbp_ref[...]
    e = jnp.exp(lbp - jnp.max(lbp, axis=0, keepdims=True))
    soft = e / jnp.sum(e, axis=0, keepdims=True)
    lb = jnp.sum(soft[0:layer + 1], axis=0, keepdims=True) - soft[0:1]
    log_lb = jnp.log(jnp.maximum(lb, TINY))
    log_1m = jnp.log1p(-lb)

    lane_h = lax.broadcasted_iota(jnp.int32, (1, n), 1) // HG_DK
    head_of_row = lax.broadcasted_iota(jnp.int32, (n, n), 0) // HG_DK
    head_of_col = lax.broadcasted_iota(jnp.int32, (n, n), 1) // HG_DK
    same_head = head_of_row == head_of_col
    ones_bd = jnp.where(same_head, 1.0, 0.0).astype(BF16)
    row = lax.broadcasted_iota(jnp.int32, (L, n), 0)
    sub = lax.broadcasted_iota(jnp.int32, (1, SUBLANES, 1), 1)
    t_idx = lax.broadcasted_iota(jnp.int32, (L, L), 0)
    s_idx = lax.broadcasted_iota(jnp.int32, (L, L), 1)

    def chunk(ci, carry):
        r0 = pl.multiple_of(ci * L, L)
        q = z_ref[pl.ds(r0, L), 0:n]
        f = z_ref[pl.ds(r0, L), n:2 * n]
        v = z_ref[pl.ds(r0, L), 2 * n:3 * n]
        g = z_ref[pl.ds(r0, L), 3 * n:4 * n]
        lsig = -_softplus(-f)
        t1 = log_1m + lsig
        log_f = jnp.maximum(log_lb, t1) + jnp.log(1.0 + jnp.exp(-jnp.abs(log_lb - t1)))
        kk = (1.0 - lb) * _sigmoid(-f)
        qs = q * _sigmoid(q)
        b = log_f * LOG2E
        s = 1
        while s < L:
            b = b + jnp.where(row >= s, pltpu.roll(b, s, 0), 0.0)
            s *= 2

        g8 = L // SUBLANES
        q3 = qs.reshape(g8, SUBLANES, n)
        k3 = kk.reshape(g8, SUBLANES, n)
        b3 = b.reshape(g8, SUBLANES, n)
        v3 = v.reshape(g8, SUBLANES, n)
        o = jnp.zeros((L, n), F32)
        for j in range(SUBLANES):
            d = jnp.where(sub >= j, b3 - b3[:, j:j + 1, :], NEG)
            xj = q3 * k3[:, j:j + 1, :] * jnp.exp2(d)
            aj = _dot(xj.reshape(L, n).astype(BF16), ones_bd)
            o = o + (aj.reshape(g8, SUBLANES, n) * v3[:, j:j + 1, :]).reshape(L, n)

        att = [jnp.zeros((L, L), F32) for _ in range(HG_HEADS)]
        c = SUBLANES
        while c < L:
            bc = b.reshape(L // c, c, n)
            bend = jnp.broadcast_to(bc[:, c - 1:c, :], (L // c, c, n)).reshape(L, n)
            bprev = jnp.where(row >= c, pltpu.roll(bend, c, 0), 0.0)
            qt = qs * jnp.exp2(b - bprev)
            kt = (kk * jnp.exp2(bend - b)).astype(BF16)
            pair = ((t_idx // c) % 2 == 1) & ((s_idx // c) == (t_idx // c) - 1)
            for h in range(HG_HEADS):
                sc = _dot_t(jnp.where(lane_h == h, qt, 0.0).astype(BF16), kt)
                att[h] = att[h] + jnp.where(pair, sc, 0.0)
            c *= 2
        att_cat = jnp.concatenate([a.astype(BF16) for a in att], axis=1)
        v_stack = jnp.concatenate([jnp.where(lane_h == h, v, 0.0).astype(BF16)
                                   for h in range(HG_HEADS)], axis=0)
        o = o + _dot(att_cat, v_stack)

        state = state_ref[...]
        o = o + _dot_t((qs * jnp.exp2(b)).astype(BF16), state.astype(BF16))
        b_last = b[L - 1:L, :]
        kdec = (kk * jnp.exp2(b_last - b)).astype(BF16)
        upd = lax.dot_general(v.astype(BF16), kdec, (((0,), (0,)), ((), ())),
                              preferred_element_type=F32)
        state_ref[...] = state * jnp.exp2(b_last) + jnp.where(same_head, upd, 0.0)

        ms = _dot((o * o).astype(BF16), ones_bd) * (1.0 / HG_DK)
        y = o * lax.rsqrt(ms + EPS) * nw_ref[...] * (g * _sigmoid(g))
        o_ref[pl.ds(r0, L), :] = y.astype(o_ref.dtype)
        return carry

    lax.fori_loop(0, HG_TILE // L, chunk, 0)


def _hgrn2(z, bsz, seq, lbp, nw, layer):
    nt = seq // HG_TILE
    n = HG_HEADS * HG_DK
    return pl.pallas_call(
        functools.partial(_hgrn2_kernel, layer),
        grid=(bsz, nt),
        in_specs=[pl.BlockSpec((HG_TILE, 4 * n), lambda b, t: (b * nt + t, 0)),
                  pl.BlockSpec(lbp.shape, lambda b, t: (0, 0)),
                  pl.BlockSpec(nw.shape, lambda b, t: (0, 0))],
        out_specs=pl.BlockSpec((HG_TILE, n), lambda b, t: (b * nt + t, 0)),
        out_shape=jax.ShapeDtypeStruct((bsz * seq, n), BF16),
        scratch_shapes=[pltpu.VMEM((n, n), F32)],
        compiler_params=_params("parallel", "arbitrary"),
        name="hgrn2",
    )(z, lbp, nw)


def _moba_kernel(nblk, qt_ref, k_ref, vt_ref, km_ref, kx_ref, o_ref, qa_ref, sa_ref, sb_ref,
                 ta_ref, tb_ref, m_ref, acc_ref):
    j_lo = pl.program_id(2) * MOBA_QBLOCKS
    hp = pl.program_id(1)
    blk = MOBA_BLOCK
    hd = ATT_HD
    pair = 2 * hd
    span = MOBA_GROUP * blk
    scale = 1.0 / math.sqrt(hd)
    qw = MOBA_QBLOCKS * blk
    feat = lax.broadcasted_iota(jnp.int32, (pair, 1), 0)
    n_t = lax.broadcasted_iota(jnp.int32, (nblk, qw), 0)
    b_row = lax.broadcasted_iota(jnp.int32, (SUBLANES, qw), 0)
    j = j_lo + lax.broadcasted_iota(jnp.int32, (1, qw), 1) // blk
    causal = (lax.broadcasted_iota(jnp.int32, (blk, blk), 0)
              <= lax.broadcasted_iota(jnp.int32, (blk, blk), 1))
    qt = qt_ref[...]
    km = km_ref[...]

    def keys(start, size, pr):
        k_pair = k_ref[pl.ds(start, size), pr * pair:(pr + 1) * pair]
        return jnp.concatenate([k_pair, kx_ref[pl.ds(start, size), :]], axis=1)

    pad = jnp.zeros((pair - nblk - SUBLANES, qw), F32)
    q_own = []
    for h in range(MOBA_HEADS):
        pr = h // 2
        slope = jnp.exp2(-8.0 * (MOBA_HEADS * hp + h + 1).astype(F32) / ATT_HEADS)
        qm = jnp.where(feat // hd == h % 2, qt[pr * pair:(pr + 1) * pair], 0.0)
        gate = jnp.dot(km[:, pr * pair:(pr + 1) * pair], qm, precision=HIGHEST,
                       preferred_element_type=F32)
        gate = jnp.where(n_t < j, gate, NEG)
        chosen = jnp.zeros((nblk, qw), F32)
        for r in range(MOBA_TOPK):
            top = jnp.max(gate, axis=0, keepdims=True)
            first = jnp.min(jnp.where(gate == top, n_t, nblk), axis=0, keepdims=True)
            pick = n_t == first
            chosen = jnp.where(pick, jnp.where(j > r, 1.0, 0.0), chosen)
            gate = jnp.where(pick, -jnp.inf, gate)
        bias = jnp.zeros((SUBLANES, qw), F32)
        for i, piece in enumerate(LOG2E_PIECES):
            bias = jnp.where(b_row == i, slope * piece, bias)
            bias = jnp.where(b_row == len(LOG2E_PIECES) + i, slope * (piece * blk), bias)
        qs = (qm * (scale * LOG2E)).astype(BF16)
        past = jnp.concatenate([jnp.where(chosen > 0.0, 0.0, NEG), bias, pad], axis=0)
        own = jnp.concatenate([jnp.where(n_t == j, 0.0, NEG), bias, pad], axis=0)
        qa_ref[h] = jnp.concatenate([qs, past.astype(BF16)], axis=0)
        q_own.append(jnp.concatenate([qs, own.astype(BF16)], axis=0))

    def score(g, buf):
        s_ref, top_ref = buf
        for pr in range(MOBA_HEADS // 2):
            k_g = keys(pl.multiple_of(g * span, span), span, pr)
            for h in (2 * pr, 2 * pr + 1):
                s = _dot(k_g, qa_ref[h])
                s_ref[h] = s
                top_ref[h] = jnp.max(s, axis=0, keepdims=True)

    def attend(g, buf):
        s_ref, top_ref = buf
        vt_g = vt_ref[:, pl.ds(pl.multiple_of(g * span, span), span)]
        for h in range(MOBA_HEADS):
            m_old = m_ref[h]
            m_new = jnp.maximum(m_old, top_ref[h])
            p = jnp.exp2((s_ref[h] - m_new).astype(BF16))
            m_ref[h] = m_new
            pv = _dot(vt_g[h * V_ROWS:(h + 1) * V_ROWS, :], p)
            acc_ref[h] = jnp.exp2(m_old - m_new) * acc_ref[h] + pv

    last = nblk // MOBA_GROUP - 1
    buf_a = (sa_ref, ta_ref)
    buf_b = (sb_ref, tb_ref)

    def chain(g0, count, more):
        for t in range(count):
            cur, nxt = (buf_a, buf_b) if t % 2 == 0 else (buf_b, buf_a)
            if t + 1 < count or more:
                score(jnp.minimum(g0 + t + 1, last), nxt)
            attend(g0 + t, cur)

    j_hi = j_lo + MOBA_QBLOCKS - 1
    pairs = (j_hi + 2 * MOBA_GROUP - 1) // (2 * MOBA_GROUP)
    nlong = pairs // (MOBA_LONG // 2)
    left = pairs % (MOBA_LONG // 2)
    score(0, buf_a)

    for qb in range(MOBA_QBLOCKS):
        cols = slice(qb * blk, (qb + 1) * blk)
        j0 = pl.multiple_of((j_lo + qb) * blk, blk)
        vt_own = vt_ref[:, pl.ds(j0, blk)]
        for h in range(MOBA_HEADS):
            s = jnp.where(causal, _dot(keys(j0, blk, h // 2), q_own[h][:, cols]), NEG)
            m = jnp.max(s, axis=0, keepdims=True)
            p = jnp.exp2((s - m).astype(BF16))
            m_ref[h, :, cols] = m
            acc_ref[h, :, cols] = _dot(vt_own[h * V_ROWS:(h + 1) * V_ROWS, :], p)

    def body(i, carry):
        chain(i * MOBA_LONG, MOBA_LONG, True)
        return carry

    lax.fori_loop(0, nlong, body, 0)
    g0 = nlong * MOBA_LONG
    size = MOBA_LONG // 2
    while size >= 2:
        take = (left & (size // 2)) != 0

        @pl.when(take)
        def _(g0=g0, size=size):
            chain(g0, size, size > 2)

        g0 = g0 + jnp.where(take, size, 0)
        size //= 2

    out = []
    for h in range(MOBA_HEADS):
        acc = acc_ref[h]
        out.append(acc[0:hd] / acc[hd:hd + 1])
    o_ref[...] = jnp.concatenate(out, axis=0).T


def _moba(qt, k, vt, km, bsz, seq):
    nblk = seq // MOBA_BLOCK
    blk = MOBA_BLOCK
    pair = 2 * ATT_HD
    width = MOBA_HEADS * ATT_HD
    assert nblk + SUBLANES <= pair and 2 * len(LOG2E_PIECES) <= SUBLANES
    assert MOBA_LONG & (MOBA_LONG - 1) == 0 and MOBA_LONG >= 2
    assert nblk % (2 * MOBA_GROUP) == 0 and nblk % MOBA_QBLOCKS == 0
    nq = nblk // MOBA_QBLOCKS
    qw = MOBA_QBLOCKS * blk
    pos = jnp.arange(seq, dtype=jnp.int32)[:, None]
    col = jnp.arange(pair, dtype=jnp.int32)[None, :]
    extras = [pos % blk] * len(LOG2E_PIECES) + [pos // blk] * len(LOG2E_PIECES)
    kx = (pos // blk == col).astype(F32)
    for i, e in enumerate(extras):
        kx = jnp.where(col == nblk + i, e.astype(F32), kx)
    kx = kx.astype(BF16)
    return pl.pallas_call(
        functools.partial(_moba_kernel, nblk),
        grid=(bsz, ATT_HEADS // MOBA_HEADS, nq),
        in_specs=[pl.BlockSpec((width, qw), lambda b, p, j: (p, b * nq + j)),
                  pl.BlockSpec((seq, width), lambda b, p, j: (b, p)),
                  pl.BlockSpec((MOBA_HEADS * V_ROWS, seq), lambda b, p, j: (p, b)),
                  pl.BlockSpec((nblk, width), lambda b, p, j: (b, p)),
                  pl.BlockSpec((seq, pair), lambda b, p, j: (0, 0))],
        out_specs=pl.BlockSpec((qw, width), lambda b, p, j: (b * nq + j, p)),
        out_shape=jax.ShapeDtypeStruct((bsz * seq, ATT_HEADS * ATT_HD), F32),
        scratch_shapes=[pltpu.VMEM((MOBA_HEADS, 2 * pair, qw), BF16),
                        pltpu.VMEM((MOBA_HEADS, MOBA_GROUP * blk, qw), F32),
                        pltpu.VMEM((MOBA_HEADS, MOBA_GROUP * blk, qw), F32),
                        pltpu.VMEM((MOBA_HEADS, 1, qw), F32),
                        pltpu.VMEM((MOBA_HEADS, 1, qw), F32),
                        pltpu.VMEM((MOBA_HEADS, 1, qw), F32),
                        pltpu.VMEM((MOBA_HEADS, V_ROWS, qw), F32)],
        compiler_params=_params("parallel", "parallel", "arbitrary"),
        name="moba",
    )(qt, k, vt, km, kx)


def _out_mlp_kernel(final, x_ref, ya_ref, yb_ref, yc_ref, an_ref, wo_ref, gm_ref, w1_ref, w2_ref,
                    gf_ref, o_ref):
    yc = _rms(yc_ref[...], an_ref[...]).astype(BF16)
    y = jnp.concatenate([ya_ref[...], yb_ref[...], yc], axis=1)
    x = x_ref[...] + _dot(y, wo_ref[...])
    hb = _rms(x, gm_ref[...]).astype(BF16)
    mlp = None
    for c in range(w1_ref.shape[1] // FF_CHUNK):
        cols = slice(c * FF_CHUNK, (c + 1) * FF_CHUNK)
        u = jnp.square(jnp.maximum(_dot(hb, w1_ref[:, cols]), 0.0))
        part = _dot(u.astype(BF16), w2_ref[cols, :])
        mlp = part if mlp is None else mlp + part
    x = x + mlp
    if final:
        x = _rms(x, gf_ref[...])
    o_ref[...] = x


def _out_mlp(x2, ya, yb, yc, an, wo, gm, w1, w2, gf, layer, final):
    t, d = x2.shape
    stacked = lambda a: pl.BlockSpec((None,) + a.shape[1:], lambda i: (layer, 0, 0))
    row = lambda a: pl.BlockSpec((ROW_TILE, a.shape[1]), lambda i: (i, 0))
    full = lambda a: pl.BlockSpec(a.shape, lambda i: (0, 0))
    return pl.pallas_call(
        functools.partial(_out_mlp_kernel, final),
        grid=(t // ROW_TILE,),
        in_specs=[row(x2), row(ya), row(yb), row(yc), full(an), stacked(wo), full(gm), stacked(w1),
                  stacked(w2), full(gf)],
        out_specs=pl.BlockSpec((ROW_TILE, d), lambda i: (i, 0)),
        out_shape=jax.ShapeDtypeStruct((t, d), F32),
        compiler_params=_params("parallel"),
        name="out_mlp",
    )(x2, ya, yb, yc, an, wo, gm, w1, w2, gf)


def _block_diag(w):
    g, n, _ = w.shape
    eye = jnp.eye(g, dtype=w.dtype)
    return (eye[:, None, :, None] * w[:, :, None, :]).reshape(g * n, g * n)


def kernel(x, w_in, w_out, norm_mix, norm_mlp, lru_conv_w, lru_conv_b, lru_wa, lru_ba, lru_wx, lru_bx,
           lru_lambda, hg_lower_bounds, hg_norm_w, lru_out_norm, att_out_norm, w_ff1, w_ff2, norm_final):
    bsz, seq, d = x.shape
    depth = w_in.shape[0]
    x2 = x.reshape(bsz * seq, d)
    r2 = lambda a: a.reshape(1, -1)
    w_in_b = w_in.astype(BF16)
    wt_b = jnp.concatenate([w_in[:, :, 1536:2048], w_in[:, :, 2560:3072]], axis=2)
    wt_b = jnp.swapaxes(wt_b, 1, 2).astype(BF16)
    w_out_b, w_ff1_b, w_ff2_b = w_out.astype(BF16), w_ff1.astype(BF16), w_ff2.astype(BF16)
    for l in range(depth):
        lru, hg, cqt, ck, cvt, km = _in_proj(x2, r2(norm_mix[l]), w_in_b, wt_b, l)
        wg = jnp.concatenate([_block_diag(lru_wa[l]), _block_diag(lru_wx[l])], axis=1).astype(BF16)
        bg = jnp.concatenate([lru_ba[l], lru_bx[l]]).reshape(1, -1)
        ya = _rglru(lru, bsz, seq, lru_conv_w[l], r2(lru_conv_b[l]), wg, bg, r2(lru_lambda[l]),
                    r2(lru_out_norm[l]))
        yb = _hgrn2(hg, bsz, seq, hg_lower_bounds, r2(jnp.tile(hg_norm_w[l], HG_HEADS)), l)
        yc = _moba(cqt, ck, cvt, km.reshape(-1, km.shape[-1]), bsz, seq)
        x2 = _out_mlp(x2, ya, yb, yc, r2(att_out_norm[l]), w_out_b, r2(norm_mlp[l]), w_ff1_b, w_ff2_b,
                      r2(norm_final), l, l == depth - 1)
    return x2.reshape(bsz, seq, d)
```

```python
import functools
import math
import struct

import jax
import jax.numpy as jnp
from jax import lax
from jax.experimental import pallas as pl
from jax.experimental.pallas import tpu as pltpu

F32 = jnp.float32
BF16 = jnp.bfloat16
HIGHEST = lax.Precision.HIGHEST

EPS = 1e-6
NEG = -1e30
TINY = 1e-30

LRU_WIDTH = 256
LRU_BLOCKS = 4
LRU_CONV = 4
LRU_C = 8.0
HG_HEADS = 4
HG_DK = 64
ATT_HEADS = 8
ATT_HD = 64
MOBA_BLOCK = 256
MOBA_TOPK = 3

SUBLANES = 8
VMEM_LIMIT = 56 * 1024 * 1024

ROW_TILE = 512
LRU_TILE = 1024
LRU_SUB = 256
HG_TILE = 512
HG_CHUNK = 256
FF_CHUNK = 1024
MOBA_GROUP = 2
MOBA_LONG = 8
MOBA_QBLOCKS = 1
MOBA_HEADS = 2
ONES_ROWS = 16
V_ROWS = ATT_HD + ONES_ROWS


def _bf16_pieces(x, n):
    pieces = []
    for _ in range(n):
        bits = struct.unpack("<I", struct.pack("<f", x))[0]
        bits = (bits + 0x7FFF + ((bits >> 16) & 1)) & 0xFFFF0000
        piece = struct.unpack("<f", struct.pack("<I", bits))[0]
        pieces.append(piece)
        x -= piece
    return tuple(pieces)


LOG2E = math.log2(math.e)
LOG2E_PIECES = _bf16_pieces(LOG2E, 3)


def _dot(a, b):
    return jnp.dot(a, b, preferred_element_type=F32)


def _dot_t(a, b, precision=None):
    return lax.dot_general(a, b, (((1,), (1,)), ((), ())), precision=precision,
                           preferred_element_type=F32)


def _rms(x, g):
    return x * lax.rsqrt(jnp.mean(x * x, axis=-1, keepdims=True) + EPS) * g


def _sigmoid(x):
    return 1.0 / (1.0 + jnp.exp(-x))


def _softplus(x):
    return jnp.maximum(x, 0.0) + jnp.log(1.0 + jnp.exp(-jnp.abs(x)))


def _params(*sem):
    return pltpu.CompilerParams(dimension_semantics=sem, vmem_limit_bytes=VMEM_LIMIT)


def _in_proj_kernel(x_ref, g_ref, w_ref, wt_ref, lru_ref, hg_ref, qt_ref, k_ref, vt_ref, km_ref):
    hb = _rms(x_ref[...], g_ref[...]).astype(BF16)
    lru_ref[...] = _dot(hb, w_ref[:, 0:512])
    hg_ref[...] = _dot(hb, w_ref[:, 512:1536])
    qt_ref[...] = _dot_t(wt_ref[0:512, :], hb)
    k = _dot(hb, w_ref[:, 2048:2560])
    k_ref[...] = k.astype(BF16)
    for i in range(ROW_TILE // MOBA_BLOCK):
        km_ref[i] = jnp.mean(k[i * MOBA_BLOCK:(i + 1) * MOBA_BLOCK], axis=0, keepdims=True)
    vt = _dot_t(wt_ref[512:1024, :], hb).astype(BF16)
    ones = jnp.ones((ONES_ROWS, ROW_TILE), BF16)
    for h in range(ATT_HEADS):
        vt_ref[h * V_ROWS:h * V_ROWS + ATT_HD, :] = vt[h * ATT_HD:(h + 1) * ATT_HD, :]
        vt_ref[h * V_ROWS + ATT_HD:(h + 1) * V_ROWS, :] = ones


def _in_proj(x2, g, w, wt, layer):
    t, d = x2.shape
    stacked = lambda a: pl.BlockSpec((None,) + a.shape[1:], lambda i: (layer, 0, 0))
    nblk = t // MOBA_BLOCK
    row = lambda width: pl.BlockSpec((ROW_TILE, width), lambda i: (i, 0))
    col = lambda height: pl.BlockSpec((height, ROW_TILE), lambda i: (0, i))
    return pl.pallas_call(
        _in_proj_kernel,
        grid=(t // ROW_TILE,),
        in_specs=[row(d), pl.BlockSpec((1, d), lambda i: (0, 0)),
                  stacked(w), stacked(wt)],
        out_specs=[row(512), row(1024), col(512), row(512), col(ATT_HEADS * V_ROWS),
                   pl.BlockSpec((ROW_TILE // MOBA_BLOCK, 1, 512), lambda i: (i, 0, 0))],
        out_shape=[jax.ShapeDtypeStruct((t, 512), F32), jax.ShapeDtypeStruct((t, 1024), F32),
                   jax.ShapeDtypeStruct((512, t), F32), jax.ShapeDtypeStruct((t, 512), BF16),
                   jax.ShapeDtypeStruct((ATT_HEADS * V_ROWS, t), BF16),
                   jax.ShapeDtypeStruct((nblk, 1, 512), F32)],
        compiler_params=_params("parallel"),
        name="in_proj",
    )(x2, g, w, wt)


def _rglru_kernel(xy_ref, cw_ref, cb_ref, wg_ref, bg_ref, lam_ref, nw_ref, o_ref, prev_ref, h_ref):
    @pl.when(pl.program_id(1) == 0)
    def _():
        prev_ref[...] = jnp.zeros_like(prev_ref)
        h_ref[...] = jnp.zeros_like(h_ref)

    w = LRU_WIDTH
    row = lax.broadcasted_iota(jnp.int32, (LRU_SUB, w), 0)
    sp = _softplus(-lam_ref[...])
    for c in range(LRU_TILE // LRU_SUB):
        rows = slice(c * LRU_SUB, (c + 1) * LRU_SUB)
        xb = xy_ref[rows, 0:w]
        yb = xy_ref[rows, w:2 * w]
        prev = prev_ref[...]
        xc = xb * cw_ref[LRU_CONV - 1:LRU_CONV, :] + cb_ref[...]
        for k in range(1, LRU_CONV):
            shifted = jnp.where(row >= k, pltpu.roll(xb, k, 0), pltpu.roll(prev, k, 0))
            xc = xc + shifted * cw_ref[LRU_CONV - 1 - k:LRU_CONV - k, :]
        prev_ref[...] = xb
        gates = _dot(xc.astype(BF16), wg_ref[...]) + bg_ref[...]
        r = _sigmoid(gates[:, 0:w])
        i = _sigmoid(gates[:, w:2 * w])
        log_a = -LRU_C * r * sp
        a = jnp.exp(log_a)
        b = jnp.sqrt(jnp.maximum(1.0 - a * a, 0.0)) * (i * xc)
        s = 1
        while s < LRU_SUB:
            keep = row >= s
            a_s = jnp.where(keep, pltpu.roll(a, s, 0), 1.0)
            b_s = jnp.where(keep, pltpu.roll(b, s, 0), 0.0)
            b = a * b_s + b
            a = a * a_s
            s *= 2
        h = b + a * h_ref[...]
        h_ref[...] = h[LRU_SUB - 1:LRU_SUB, :]
        y = h * jax.nn.gelu(yb, approximate=True)
        o_ref[rows, :] = _rms(y, nw_ref[...]).astype(o_ref.dtype)


def _rglru(xy, bsz, seq, cw, cb, wg, bg, lam, nw):
    nt = seq // LRU_TILE
    w = LRU_WIDTH
    vec = lambda a: pl.BlockSpec(a.shape, lambda b, t: (0, 0))
    return pl.pallas_call(
        _rglru_kernel,
        grid=(bsz, nt),
        in_specs=[pl.BlockSpec((LRU_TILE, 2 * w), lambda b, t: (b * nt + t, 0)),
                  vec(cw), vec(cb), vec(wg), vec(bg), vec(lam), vec(nw)],
        out_specs=pl.BlockSpec((LRU_TILE, w), lambda b, t: (b * nt + t, 0)),
        out_shape=jax.ShapeDtypeStruct((bsz * seq, w), BF16),
        scratch_shapes=[pltpu.VMEM((LRU_SUB, w), F32), pltpu.VMEM((1, w), F32)],
        compiler_params=_params("parallel", "arbitrary"),
        name="rglru",
    )(xy, cw, cb, wg, bg, lam, nw)


def _hgrn2_kernel(layer, z_ref, lbp_ref, nw_ref, o_ref, state_ref):
    @pl.when(pl.program_id(1) == 0)
    def _():
        state_ref[...] = jnp.zeros_like(state_ref)

    n = HG_HEADS * HG_DK
    L = HG_CHUNK
    lbp = lbp_ref[...]
    e = jnp.exp(lbp - jnp.max(lbp, axis=0, keepdims=True))
    soft = e / jnp.sum(e, axis=0, keepdims=True)
    lb = jnp.sum(soft[0:layer + 1], axis=0, keepdims=True) - soft[0:1]
    log_lb = jnp.log(jnp.maximum(lb, TINY))
    log_1m = jnp.log1p(-lb)

    lane_h = lax.broadcasted_iota(jnp.int32, (1, n), 1) // HG_DK
    head_of_row = lax.broadcasted_iota(jnp.int32, (n, n), 0) // HG_DK
    head_of_col = lax.broadcasted_iota(jnp.int32, (n, n), 1) // HG_DK
    same_head = head_of_row == head_of_col
    ones_bd = jnp.where(same_head, 1.0, 0.0).astype(BF16)
    row = lax.broadcasted_iota(jnp.int32, (L, n), 0)
    sub = lax.broadcasted_iota(jnp.int32, (1, SUBLANES, 1), 1)
    t_idx = lax.broadcasted_iota(jnp.int32, (L, L), 0)
    s_idx = lax.broadcasted_iota(jnp.int32, (L, L), 1)

    def chunk(ci, carry):
        r0 = pl.multiple_of(ci * L, L)
        q = z_ref[pl.ds(r0, L), 0:n]
        f = z_ref[pl.ds(r0, L), n:2 * n]
        v = z_ref[pl.ds(r0, L), 2 * n:3 * n]
        g = z_ref[pl.ds(r0, L), 3 * n:4 * n]
        lsig = -_softplus(-f)
        t1 = log_1m + lsig
        log_f = jnp.maximum(log_lb, t1) + jnp.log(1.0 + jnp.exp(-jnp.abs(log_lb - t1)))
        kk = (1.0 - lb) * _sigmoid(-f)
        qs = q * _sigmoid(q)
        b = log_f * LOG2E
        s = 1
        while s < L:
            b = b + jnp.where(row >= s, pltpu.roll(b, s, 0), 0.0)
            s *= 2

        g8 = L // SUBLANES
        q3 = qs.reshape(g8, SUBLANES, n)
        k3 = kk.reshape(g8, SUBLANES, n)
        b3 = b.reshape(g8, SUBLANES, n)
        v3 = v.reshape(g8, SUBLANES, n)
        o = jnp.zeros((L, n), F32)
        for j in range(SUBLANES):
            d = jnp.where(sub >= j, b3 - b3[:, j:j + 1, :], NEG)
            xj = q3 * k3[:, j:j + 1, :] * jnp.exp2(d)
            aj = _dot(xj.reshape(L, n).astype(BF16), ones_bd)
            o = o + (aj.reshape(g8, SUBLANES, n) * v3[:, j:j + 1, :]).reshape(L, n)

        att = [jnp.zeros((L, L), F32) for _ in range(HG_HEADS)]
        c = SUBLANES
        while c < L:
            bc = b.reshape(L // c, c, n)
            bend = jnp.broadcast_to(bc[:, c - 1:c, :], (L // c, c, n)).reshape(L, n)
            bprev = jnp.where(row >= c, pltpu.roll(bend, c, 0), 0.0)
            qt = qs * jnp.exp2(b - bprev)
            kt = (kk * jnp.exp2(bend - b)).astype(BF16)
            pair = ((t_idx // c) % 2 == 1) & ((s_idx // c) == (t_idx // c) - 1)
            for h in range(HG_HEADS):
                sc = _dot_t(jnp.where(lane_h == h, qt, 0.0).astype(BF16), kt)
                att[h] = att[h] + jnp.where(pair, sc, 0.0)
            c *= 2
        att_cat = jnp.concatenate([a.astype(BF16) for a in att], axis=1)
        v_stack = jnp.concatenate([jnp.where(lane_h == h, v, 0.0).astype(BF16)
                                   for h in range(HG_HEADS)], axis=0)
        o = o + _dot(att_cat, v_stack)

        state = state_ref[...]
        o = o + _dot_t((qs * jnp.exp2(b)).astype(BF16), state.astype(BF16))
        b_last = b[L - 1:L, :]
        kdec = (kk * jnp.exp2(b_last - b)).astype(BF16)
        upd = lax.dot_general(v.astype(BF16), kdec, (((0,), (0,)), ((), ())),
                              preferred_element_type=F32)
        state_ref[...] = state * jnp.exp2(b_last) + jnp.where(same_head, upd, 0.0)

        ms = _dot((o * o).astype(BF16), ones_bd) * (1.0 / HG_DK)
        y = o * lax.rsqrt(ms + EPS) * nw_ref[...] * (g * _sigmoid(g))
        o_ref[pl.ds(r0, L), :] = y.astype(o_ref.dtype)
        return carry

    lax.fori_loop(0, HG_TILE // L, chunk, 0)


def _hgrn2(z, bsz, seq, lbp, nw, layer):
    nt = seq // HG_TILE
    n = HG_HEADS * HG_DK
    return pl.pallas_call(
        functools.partial(_hgrn2_kernel, layer),
        grid=(bsz, nt),
        in_specs=[pl.BlockSpec((HG_TILE, 4 * n), lambda b, t: (b * nt + t, 0)),
                  pl.BlockSpec(lbp.shape, lambda b, t: (0, 0)),
                  pl.BlockSpec(nw.shape, lambda b, t: (0, 0))],
        out_specs=pl.BlockSpec((HG_TILE, n), lambda b, t: (b * nt + t, 0)),
        out_shape=jax.ShapeDtypeStruct((bsz * seq, n), BF16),
        scratch_shapes=[pltpu.VMEM((n, n), F32)],
        compiler_params=_params("parallel", "arbitrary"),
        name="hgrn2",
    )(z, lbp, nw)


def _moba_kernel(nblk, qt_ref, k_ref, vt_ref, km_ref, kx_ref, o_ref, qa_ref, sa_ref, sb_ref,
                 ta_ref, tb_ref, m_ref, acc_ref):
    j_lo = pl.program_id(2) * MOBA_QBLOCKS
    hp = pl.program_id(1)
    blk = MOBA_BLOCK
    hd = ATT_HD
    pair = 2 * hd
    span = MOBA_GROUP * blk
    scale = 1.0 / math.sqrt(hd)
    qw = MOBA_QBLOCKS * blk
    feat = lax.broadcasted_iota(jnp.int32, (pair, 1), 0)
    n_t = lax.broadcasted_iota(jnp.int32, (nblk, qw), 0)
    b_row = lax.broadcasted_iota(jnp.int32, (SUBLANES, qw), 0)
    j = j_lo + lax.broadcasted_iota(jnp.int32, (1, qw), 1) // blk
    causal = (lax.broadcasted_iota(jnp.int32, (blk, blk), 0)
              <= lax.broadcasted_iota(jnp.int32, (blk, blk), 1))
    qt = qt_ref[...]
    km = km_ref[...]

    def keys(start, size, pr):
        k_pair = k_ref[pl.ds(start, size), pr * pair:(pr + 1) * pair]
        return jnp.concatenate([k_pair, kx_ref[pl.ds(start, size), :]], axis=1)

    pad = jnp.zeros((pair - nblk - SUBLANES, qw), F32)
    q_own = []
    for h in range(MOBA_HEADS):
        pr = h // 2
        slope = jnp.exp2(-8.0 * (MOBA_HEADS * hp + h + 1).astype(F32) / ATT_HEADS)
        qm = jnp.where(feat // hd == h % 2, qt[pr * pair:(pr + 1) * pair], 0.0)
        gate = jnp.dot(km[:, pr * pair:(pr + 1) * pair], qm, precision=HIGHEST,
                       preferred_element_type=F32)
        gate = jnp.where(n_t < j, gate, NEG)
        chosen = jnp.zeros((nblk, qw), F32)
        for r in range(MOBA_TOPK):
            top = jnp.max(gate, axis=0, keepdims=True)
            first = jnp.min(jnp.where(gate == top, n_t, nblk), axis=0, keepdims=True)
            pick = n_t == first
            chosen = jnp.where(pick, jnp.where(j > r, 1.0, 0.0), chosen)
            gate = jnp.where(pick, -jnp.inf, gate)
        bias = jnp.zeros((SUBLANES, qw), F32)
        for i, piece in enumerate(LOG2E_PIECES):
            bias = jnp.where(b_row == i, slope * piece, bias)
            bias = jnp.where(b_row == len(LOG2E_PIECES) + i, slope * (piece * blk), bias)
        qs = (qm * (scale * LOG2E)).astype(BF16)
        past = jnp.concatenate([jnp.where(chosen > 0.0, 0.0, NEG), bias, pad], axis=0)
        own = jnp.concatenate([jnp.where(n_t == j, 0.0, NEG), bias, pad], axis=0)
        qa_ref[h] = jnp.concatenate([qs, past.astype(BF16)], axis=0)
        q_own.append(jnp.concatenate([qs, own.astype(BF16)], axis=0))

    def score(g, buf):
        s_ref, top_ref = buf
        for pr in range(MOBA_HEADS // 2):
            k_g = keys(pl.multiple_of(g * span, span), span, pr)
            for h in (2 * pr, 2 * pr + 1):
                s = _dot(k_g, qa_ref[h])
                s_ref[h] = s
                top_ref[h] = jnp.max(s, axis=0, keepdims=True)

    def attend(g, buf):
        s_ref, top_ref = buf
        vt_g = vt_ref[:, pl.ds(pl.multiple_of(g * span, span), span)]
        for h in range(MOBA_HEADS):
            m_old = m_ref[h]
            m_new = jnp.maximum(m_old, top_ref[h])
            p = jnp.exp2((s_ref[h] - m_new).astype(BF16))
            m_ref[h] = m_new
            pv = _dot(vt_g[h * V_ROWS:(h + 1) * V_ROWS, :], p)
            acc_ref[h] = jnp.exp2(m_old - m_new) * acc_ref[h] + pv

    last = nblk // MOBA_GROUP - 1
    buf_a = (sa_ref, ta_ref)
    buf_b = (sb_ref, tb_ref)

    def chain(g0, count, more):
        for t in range(count):
            cur, nxt = (buf_a, buf_b) if t % 2 == 0 else (buf_b, buf_a)
            if t + 1 < count or more:
                score(jnp.minimum(g0 + t + 1, last), nxt)
            attend(g0 + t, cur)

    j_hi = j_lo + MOBA_QBLOCKS - 1
    groups = (j_hi + MOBA_GROUP - 1) // MOBA_GROUP
    nlong = groups // MOBA_LONG
    left = groups % MOBA_LONG
    score(0, buf_a)

    for qb in range(MOBA_QBLOCKS):
        cols = slice(qb * blk, (qb + 1) * blk)
        j0 = pl.multiple_of((j_lo + qb) * blk, blk)
        vt_own = vt_ref[:, pl.ds(j0, blk)]
        for h in range(MOBA_HEADS):
            s = jnp.where(causal, _dot(keys(j0, blk, h // 2), q_own[h][:, cols]), NEG)
            m = jnp.max(s, axis=0, keepdims=True)
            p = jnp.exp2((s - m).astype(BF16))
            m_ref[h, :, cols] = m
            acc_ref[h, :, cols] = _dot(vt_own[h * V_ROWS:(h + 1) * V_ROWS, :], p)

    def body(i, carry):
        chain(i * MOBA_LONG, MOBA_LONG, True)
        return carry

    lax.fori_loop(0, nlong, body, 0)
    g0 = nlong * MOBA_LONG
    size = MOBA_LONG // 2
    while size >= 1:
        take = (left & size) != 0

        @pl.when(take)
        def _(g0=g0, size=size):
            chain(g0, size, size > 1)

        g0 = g0 + jnp.where(take, size, 0)
        size //= 2

    out = []
    for h in range(MOBA_HEADS):
        acc = acc_ref[h]
        out.append(acc[0:hd] / acc[hd:hd + 1])
    o_ref[...] = jnp.concatenate(out, axis=0).T


def _moba(qt, k, vt, km, bsz, seq):
    nblk = seq // MOBA_BLOCK
    blk = MOBA_BLOCK
    pair = 2 * ATT_HD
    width = MOBA_HEADS * ATT_HD
    assert nblk + SUBLANES <= pair and 2 * len(LOG2E_PIECES) <= SUBLANES
    assert MOBA_LONG & (MOBA_LONG - 1) == 0 and MOBA_LONG >= 2
    assert nblk % MOBA_GROUP == 0 and nblk % MOBA_QBLOCKS == 0
    nq = nblk // MOBA_QBLOCKS
    qw = MOBA_QBLOCKS * blk
    pos = jnp.arange(seq, dtype=jnp.int32)[:, None]
    col = jnp.arange(pair, dtype=jnp.int32)[None, :]
    extras = [pos % blk] * len(LOG2E_PIECES) + [pos // blk] * len(LOG2E_PIECES)
    kx = (pos // blk == col).astype(F32)
    for i, e in enumerate(extras):
        kx = jnp.where(col == nblk + i, e.astype(F32), kx)
    kx = kx.astype(BF16)
    return pl.pallas_call(
        functools.partial(_moba_kernel, nblk),
        grid=(bsz, ATT_HEADS // MOBA_HEADS, nq),
        in_specs=[pl.BlockSpec((width, qw), lambda b, p, j: (p, b * nq + j)),
                  pl.BlockSpec((seq, width), lambda b, p, j: (b, p)),
                  pl.BlockSpec((MOBA_HEADS * V_ROWS, seq), lambda b, p, j: (p, b)),
                  pl.BlockSpec((nblk, width), lambda b, p, j: (b, p)),
                  pl.BlockSpec((seq, pair), lambda b, p, j: (0, 0))],
        out_specs=pl.BlockSpec((qw, width), lambda b, p, j: (b * nq + j, p)),
        out_shape=jax.ShapeDtypeStruct((bsz * seq, ATT_HEADS * ATT_HD), F32),
        scratch_shapes=[pltpu.VMEM((MOBA_HEADS, 2 * pair, qw), BF16),
                        pltpu.VMEM((MOBA_HEADS, MOBA_GROUP * blk, qw), F32),
                        pltpu.VMEM((MOBA_HEADS, MOBA_GROUP * blk, qw), F32),
                        pltpu.VMEM((MOBA_HEADS, 1, qw), F32),
                        pltpu.VMEM((MOBA_HEADS, 1, qw), F32),
                        pltpu.VMEM((MOBA_HEADS, 1, qw), F32),
                        pltpu.VMEM((MOBA_HEADS, V_ROWS, qw), F32)],
        compiler_params=_params("parallel", "parallel", "arbitrary"),
        name="moba",
    )(qt, k, vt, km, kx)


def _out_mlp_kernel(final, x_ref, ya_ref, yb_ref, yc_ref, an_ref, wo_ref, gm_ref, w1_ref, w2_ref,
                    gf_ref, o_ref):
    yc = _rms(yc_ref[...], an_ref[...]).astype(BF16)
    y = jnp.concatenate([ya_ref[...], yb_ref[...], yc], axis=1)
    x = x_ref[...] + _dot(y, wo_ref[...])
    hb = _rms(x, gm_ref[...]).astype(BF16)
    mlp = None
    for c in range(w1_ref.shape[1] // FF_CHUNK):
        cols = slice(c * FF_CHUNK, (c + 1) * FF_CHUNK)
        u = jnp.square(jnp.maximum(_dot(hb, w1_ref[:, cols]), 0.0))
        part = _dot(u.astype(BF16), w2_ref[cols, :])
        mlp = part if mlp is None else mlp + part
    x = x + mlp
    if final:
        x = _rms(x, gf_ref[...])
    o_ref[...] = x


def _out_mlp(x2, ya, yb, yc, an, wo, gm, w1, w2, gf, layer, final):
    t, d = x2.shape
    stacked = lambda a: pl.BlockSpec((None,) + a.shape[1:], lambda i: (layer, 0, 0))
    row = lambda a: pl.BlockSpec((ROW_TILE, a.shape[1]), lambda i: (i, 0))
    full = lambda a: pl.BlockSpec(a.shape, lambda i: (0, 0))
    return pl.pallas_call(
        functools.partial(_out_mlp_kernel, final),
        grid=(t // ROW_TILE,),
        in_specs=[row(x2), row(ya), row(yb), row(yc), full(an), stacked(wo), full(gm), stacked(w1),
                  stacked(w2), full(gf)],
        out_specs=pl.BlockSpec((ROW_TILE, d), lambda i: (i, 0)),
        out_shape=jax.ShapeDtypeStruct((t, d), F32),
        compiler_params=_params("parallel"),
        name="out_mlp",
    )(x2, ya, yb, yc, an, wo, gm, w1, w2, gf)


def _block_diag(w):
    g, n, _ = w.shape
    eye = jnp.eye(g, dtype=w.dtype)
    return (eye[:, None, :, None] * w[:, :, None, :]).reshape(g * n, g * n)


def kernel(x, w_in, w_out, norm_mix, norm_mlp, lru_conv_w, lru_conv_b, lru_wa, lru_ba, lru_wx, lru_bx,
           lru_lambda, hg_lower_bounds, hg_norm_w, lru_out_norm, att_out_norm, w_ff1, w_ff2, norm_final):
    bsz, seq, d = x.shape
    depth = w_in.shape[0]
    x2 = x.reshape(bsz * seq, d)
    r2 = lambda a: a.reshape(1, -1)
    w_in_b = w_in.astype(BF16)
    w_src = lax.optimization_barrier(w_in)
    wt_b = jnp.concatenate([w_src[:, :, 1536:2048], w_src[:, :, 2560:3072]], axis=2)
    wt_b = jnp.swapaxes(wt_b, 1, 2).astype(BF16)
    w_out_b, w_ff1_b, w_ff2_b = w_out.astype(BF16), w_ff1.astype(BF16), w_ff2.astype(BF16)
    for l in range(depth):
        lru, hg, cqt, ck, cvt, km = _in_proj(x2, r2(norm_mix[l]), w_in_b, wt_b, l)
        wg = jnp.concatenate([_block_diag(lru_wa[l]), _block_diag(lru_wx[l])], axis=1).astype(BF16)
        bg = jnp.concatenate([lru_ba[l], lru_bx[l]]).reshape(1, -1)
        ya = _rglru(lru, bsz, seq, lru_conv_w[l], r2(lru_conv_b[l]), wg, bg, r2(lru_lambda[l]),
                    r2(lru_out_norm[l]))
        yb = _hgrn2(hg, bsz, seq, hg_lower_bounds, r2(jnp.tile(hg_norm_w[l], HG_HEADS)), l)
        yc = _moba(cqt, ck, cvt, km.reshape(-1, km.shape[-1]), bsz, seq)
        x2 = _out_mlp(x2, ya, yb, yc, r2(att_out_norm[l]), w_out_b, r2(norm_mlp[l]), w_ff1_b, w_ff2_b,
                      r2(norm_final), l, l == depth - 1)
    return x2.reshape(bsz, seq, d)
```

```python
import functools
import math
import struct

import jax
import jax.numpy as jnp
from jax import lax
from jax.experimental import pallas as pl
from jax.experimental.pallas import tpu as pltpu

F32 = jnp.float32
BF16 = jnp.bfloat16
HIGHEST = lax.Precision.HIGHEST

EPS = 1e-6
NEG = -1e30
TINY = 1e-30

LRU_WIDTH = 256
LRU_BLOCKS = 4
LRU_CONV = 4
LRU_C = 8.0
HG_HEADS = 4
HG_DK = 64
ATT_HEADS = 8
ATT_HD = 64
MOBA_BLOCK = 256
MOBA_TOPK = 3

SUBLANES = 8
VMEM_LIMIT = 56 * 1024 * 1024

ROW_TILE = 512
LRU_SUB = 256
HG_TILE = 512
HG_CHUNK = 256
FF_CHUNK = 1024
MOBA_GROUP = 2
MOBA_LONG = 8
MOBA_QBLOCKS = 1
MOBA_HEADS = 2
ONES_ROWS = 16
V_ROWS = ATT_HD + ONES_ROWS


def _bf16_pieces(x, n):
    pieces = []
    for _ in range(n):
        bits = struct.unpack("<I", struct.pack("<f", x))[0]
        bits = (bits + 0x7FFF + ((bits >> 16) & 1)) & 0xFFFF0000
        piece = struct.unpack("<f", struct.pack("<I", bits))[0]
        pieces.append(piece)
        x -= piece
    return tuple(pieces)


LOG2E = math.log2(math.e)
LOG2E_PIECES = _bf16_pieces(LOG2E, 3)


def _dot(a, b):
    return jnp.dot(a, b, preferred_element_type=F32)


def _dot_t(a, b, precision=None):
    return lax.dot_general(a, b, (((1,), (1,)), ((), ())), precision=precision,
                           preferred_element_type=F32)


def _rms(x, g):
    return x * lax.rsqrt(jnp.mean(x * x, axis=-1, keepdims=True) + EPS) * g


def _sigmoid(x):
    return 1.0 / (1.0 + jnp.exp(-x))


def _softplus(x):
    return jnp.maximum(x, 0.0) + jnp.log(1.0 + jnp.exp(-jnp.abs(x)))


def _params(*sem):
    return pltpu.CompilerParams(dimension_semantics=sem, vmem_limit_bytes=VMEM_LIMIT)


def _in_proj_kernel(seq, x_ref, g_ref, w_ref, wt_ref, cw_ref, cb_ref, wg_ref, bg_ref, lam_ref, nw_ref,
                    ya_ref, hg_ref, qt_ref, k_ref, vt_ref, km_ref, prev_ref, h_ref):
    @pl.when((pl.program_id(0) * ROW_TILE) % seq == 0)
    def _():
        prev_ref[...] = jnp.zeros_like(prev_ref)
        h_ref[...] = jnp.zeros_like(h_ref)

    hb = _rms(x_ref[...], g_ref[...]).astype(BF16)
    _rglru_tile(_dot(hb, w_ref[:, 0:512]), cw_ref, cb_ref, wg_ref, bg_ref, lam_ref, nw_ref, ya_ref,
                prev_ref, h_ref)
    hg_ref[...] = _dot(hb, w_ref[:, 512:1536])
    qt_ref[...] = _dot_t(wt_ref[0:512, :], hb)
    k = _dot(hb, w_ref[:, 2048:2560])
    k_ref[...] = k.astype(BF16)
    for i in range(ROW_TILE // MOBA_BLOCK):
        km_ref[i] = jnp.mean(k[i * MOBA_BLOCK:(i + 1) * MOBA_BLOCK], axis=0, keepdims=True)
    vt = _dot_t(wt_ref[512:1024, :], hb).astype(BF16)
    ones = jnp.ones((ONES_ROWS, ROW_TILE), BF16)
    for h in range(ATT_HEADS):
        vt_ref[h * V_ROWS:h * V_ROWS + ATT_HD, :] = vt[h * ATT_HD:(h + 1) * ATT_HD, :]
        vt_ref[h * V_ROWS + ATT_HD:(h + 1) * V_ROWS, :] = ones


def _in_proj(x2, g, w, wt, layer, seq, lru_params):
    t, d = x2.shape
    nblk = t // MOBA_BLOCK
    stacked = lambda a: pl.BlockSpec((None,) + a.shape[1:], lambda i: (layer, 0, 0))
    vec = lambda a: pl.BlockSpec(a.shape, lambda i: (0, 0))
    row = lambda width: pl.BlockSpec((ROW_TILE, width), lambda i: (i, 0))
    col = lambda height: pl.BlockSpec((height, ROW_TILE), lambda i: (0, i))
    assert seq % ROW_TILE == 0 and ROW_TILE % LRU_SUB == 0
    return pl.pallas_call(
        functools.partial(_in_proj_kernel, seq),
        grid=(t // ROW_TILE,),
        in_specs=[row(d), vec(g), stacked(w), stacked(wt)] + [vec(a) for a in lru_params],
        out_specs=[row(LRU_WIDTH), row(1024), col(512), row(512), col(ATT_HEADS * V_ROWS),
                   pl.BlockSpec((ROW_TILE // MOBA_BLOCK, 1, 512), lambda i: (i, 0, 0))],
        out_shape=[jax.ShapeDtypeStruct((t, LRU_WIDTH), BF16), jax.ShapeDtypeStruct((t, 1024), F32),
                   jax.ShapeDtypeStruct((512, t), F32), jax.ShapeDtypeStruct((t, 512), BF16),
                   jax.ShapeDtypeStruct((ATT_HEADS * V_ROWS, t), BF16),
                   jax.ShapeDtypeStruct((nblk, 1, 512), F32)],
        scratch_shapes=[pltpu.VMEM((LRU_SUB, LRU_WIDTH), F32), pltpu.VMEM((1, LRU_WIDTH), F32)],
        compiler_params=_params("arbitrary"),
        name="in_proj",
    )(x2, g, w, wt, *lru_params)


def _rglru_tile(xy, cw_ref, cb_ref, wg_ref, bg_ref, lam_ref, nw_ref, o_ref, prev_ref, h_ref):
    w = LRU_WIDTH
    row = lax.broadcasted_iota(jnp.int32, (LRU_SUB, w), 0)
    sp = _softplus(-lam_ref[...])
    for c in range(ROW_TILE // LRU_SUB):
        rows = slice(c * LRU_SUB, (c + 1) * LRU_SUB)
        xb = xy[rows, 0:w]
        yb = xy[rows, w:2 * w]
        prev = prev_ref[...]
        xc = xb * cw_ref[LRU_CONV - 1:LRU_CONV, :] + cb_ref[...]
        for k in range(1, LRU_CONV):
            shifted = jnp.where(row >= k, pltpu.roll(xb, k, 0), pltpu.roll(prev, k, 0))
            xc = xc + shifted * cw_ref[LRU_CONV - 1 - k:LRU_CONV - k, :]
        prev_ref[...] = xb
        gates = _dot(xc.astype(BF16), wg_ref[...]) + bg_ref[...]
        r = _sigmoid(gates[:, 0:w])
        i = _sigmoid(gates[:, w:2 * w])
        log_a = -LRU_C * r * sp
        a = jnp.exp(log_a)
        b = jnp.sqrt(jnp.maximum(1.0 - a * a, 0.0)) * (i * xc)
        s = 1
        while s < LRU_SUB:
            keep = row >= s
            a_s = jnp.where(keep, pltpu.roll(a, s, 0), 1.0)
            b_s = jnp.where(keep, pltpu.roll(b, s, 0), 0.0)
            b = a * b_s + b
            a = a * a_s
            s *= 2
        h = b + a * h_ref[...]
        h_ref[...] = h[LRU_SUB - 1:LRU_SUB, :]
        y = h * jax.nn.gelu(yb, approximate=True)
        o_ref[rows, :] = _rms(y, nw_ref[...]).astype(o_ref.dtype)


def _hgrn2_kernel(layer, z_ref, lbp_ref, nw_ref, o_ref, state_ref):
    @pl.when(pl.program_id(1) == 0)
    def _():
        state_ref[...] = jnp.zeros_like(state_ref)

    n = HG_HEADS * HG_DK
    L = HG_CHUNK
    lbp = lbp_ref[...]
    e = jnp.exp(lbp - jnp.max(lbp, axis=0, keepdims=True))
    soft = e / jnp.sum(e, axis=0, keepdims=True)
    lb = jnp.sum(soft[0:layer + 1], axis=0, keepdims=True) - soft[0:1]
    log_lb = jnp.log(jnp.maximum(lb, TINY))
    log_1m = jnp.log1p(-lb)

    lane_h = lax.broadcasted_iota(jnp.int32, (1, n), 1) // HG_DK
    head_of_row = lax.broadcasted_iota(jnp.int32, (n, n), 0) // HG_DK
    head_of_col = lax.broadcasted_iota(jnp.int32, (n, n), 1) // HG_DK
    same_head = head_of_row == head_of_col
    ones_bd = jnp.where(same_head, 1.0, 0.0).astype(BF16)
    row = lax.broadcasted_iota(jnp.int32, (L, n), 0)
    sub = lax.broadcasted_iota(jnp.int32, (1, SUBLANES, 1), 1)
    t_idx = lax.broadcasted_iota(jnp.int32, (L, L), 0)
    s_idx = lax.broadcasted_iota(jnp.int32, (L, L), 1)

    def chunk(ci, carry):
        r0 = pl.multiple_of(ci * L, L)
        q = z_ref[pl.ds(r0, L), 0:n]
        f = z_ref[pl.ds(r0, L), n:2 * n]
        v = z_ref[pl.ds(r0, L), 2 * n:3 * n]
        g = z_ref[pl.ds(r0, L), 3 * n:4 * n]
        lsig = -_softplus(-f)
        t1 = log_1m + lsig
        log_f = jnp.maximum(log_lb, t1) + jnp.log(1.0 + jnp.exp(-jnp.abs(log_lb - t1)))
        kk = (1.0 - lb) * _sigmoid(-f)
        qs = q * _sigmoid(q)
        b = log_f * LOG2E
        s = 1
        while s < L:
            b = b + jnp.where(row >= s, pltpu.roll(b, s, 0), 0.0)
            s *= 2

        g8 = L // SUBLANES
        q3 = qs.reshape(g8, SUBLANES, n)
        k3 = kk.reshape(g8, SUBLANES, n)
        b3 = b.reshape(g8, SUBLANES, n)
        v3 = v.reshape(g8, SUBLANES, n)
        o = jnp.zeros((L, n), F32)
        for j in range(SUBLANES):
            d = jnp.where(sub >= j, b3 - b3[:, j:j + 1, :], NEG)
            xj = q3 * k3[:, j:j + 1, :] * jnp.exp2(d)
            aj = _dot(xj.reshape(L, n).astype(BF16), ones_bd)
            o = o + (aj.reshape(g8, SUBLANES, n) * v3[:, j:j + 1, :]).reshape(L, n)

        att = [jnp.zeros((L, L), F32) for _ in range(HG_HEADS)]
        c = SUBLANES
        while c < L:
            bc = b.reshape(L // c, c, n)
            bend = jnp.broadcast_to(bc[:, c - 1:c, :], (L // c, c, n)).reshape(L, n)
            bprev = jnp.where(row >= c, pltpu.roll(bend, c, 0), 0.0)
            qt = qs * jnp.exp2(b - bprev)
            kt = (kk * jnp.exp2(bend - b)).astype(BF16)
            pair = ((t_idx // c) % 2 == 1) & ((s_idx // c) == (t_idx // c) - 1)
            for h in range(HG_HEADS):
                sc = _dot_t(jnp.where(lane_h == h, qt, 0.0).astype(BF16), kt)
                att[h] = att[h] + jnp.where(pair, sc, 0.0)
            c *= 2
        att_cat = jnp.concatenate([a.astype(BF16) for a in att], axis=1)
        v_stack = jnp.concatenate([jnp.where(lane_h == h, v, 0.0).astype(BF16)
                                   for h in range(HG_HEADS)], axis=0)
        o = o + _dot(att_cat, v_stack)

        state = state_ref[...]
        o = o + _dot_t((qs * jnp.exp2(b)).astype(BF16), state.astype(BF16))
        b_last = b[L - 1:L, :]
        kdec = (kk * jnp.exp2(b_last - b)).astype(BF16)
        upd = lax.dot_general(v.astype(BF16), kdec, (((0,), (0,)), ((), ())),
                              preferred_element_type=F32)
        state_ref[...] = state * jnp.exp2(b_last) + jnp.where(same_head, upd, 0.0)

        ms = _dot((o * o).astype(BF16), ones_bd) * (1.0 / HG_DK)
        y = o * lax.rsqrt(ms + EPS) * nw_ref[...] * (g * _sigmoid(g))
        o_ref[pl.ds(r0, L), :] = y.astype(o_ref.dtype)
        return carry

    lax.fori_loop(0, HG_TILE // L, chunk, 0)


def _hgrn2(z, bsz, seq, lbp, nw, layer):
    nt = seq // HG_TILE
    n = HG_HEADS * HG_DK
    return pl.pallas_call(
        functools.partial(_hgrn2_kernel, layer),
        grid=(bsz, nt),
        in_specs=[pl.BlockSpec((HG_TILE, 4 * n), lambda b, t: (b * nt + t, 0)),
                  pl.BlockSpec(lbp.shape, lambda b, t: (0, 0)),
                  pl.BlockSpec(nw.shape, lambda b, t: (0, 0))],
        out_specs=pl.BlockSpec((HG_TILE, n), lambda b, t: (b * nt + t, 0)),
        out_shape=jax.ShapeDtypeStruct((bsz * seq, n), BF16),
        scratch_shapes=[pltpu.VMEM((n, n), F32)],
        compiler_params=_params("parallel", "arbitrary"),
        name="hgrn2",
    )(z, lbp, nw)


def _moba_kernel(nblk, qt_ref, k_ref, vt_ref, km_ref, kx_ref, o_ref, qa_ref, sa_ref, sb_ref,
                 ta_ref, tb_ref, m_ref, acc_ref):
    j_lo = pl.program_id(2) * MOBA_QBLOCKS
    hp = pl.program_id(1)
    blk = MOBA_BLOCK
    hd = ATT_HD
    pair = 2 * hd
    span = MOBA_GROUP * blk
    scale = 1.0 / math.sqrt(hd)
    qw = MOBA_QBLOCKS * blk
    feat = lax.broadcasted_iota(jnp.int32, (pair, 1), 0)
    n_t = lax.broadcasted_iota(jnp.int32, (nblk, qw), 0)
    b_row = lax.broadcasted_iota(jnp.int32, (SUBLANES, qw), 0)
    j = j_lo + lax.broadcasted_iota(jnp.int32, (1, qw), 1) // blk
    causal = (lax.broadcasted_iota(jnp.int32, (blk, blk), 0)
              <= lax.broadcasted_iota(jnp.int32, (blk, blk), 1))
    qt = qt_ref[...]
    km = km_ref[...]

    def keys(start, size, pr):
        k_pair = k_ref[pl.ds(start, size), pr * pair:(pr + 1) * pair]
        return jnp.concatenate([k_pair, kx_ref[pl.ds(start, size), :]], axis=1)

    pad = jnp.zeros((pair - nblk - SUBLANES, qw), F32)
    q_own = []
    for h in range(MOBA_HEADS):
        pr = h // 2
        slope = jnp.exp2(-8.0 * (MOBA_HEADS * hp + h + 1).astype(F32) / ATT_HEADS)
        qm = jnp.where(feat // hd == h % 2, qt[pr * pair:(pr + 1) * pair], 0.0)
        gate = jnp.dot(km[:, pr * pair:(pr + 1) * pair], qm, precision=HIGHEST,
                       preferred_element_type=F32)
        gate = jnp.where(n_t < j, gate, NEG)
        chosen = jnp.zeros((nblk, qw), F32)
        for r in range(MOBA_TOPK):
            top = jnp.max(gate, axis=0, keepdims=True)
            first = jnp.min(jnp.where(gate == top, n_t, nblk), axis=0, keepdims=True)
            pick = n_t == first
            chosen = jnp.where(pick, jnp.where(j > r, 1.0, 0.0), chosen)
            gate = jnp.where(pick, -jnp.inf, gate)
        bias = jnp.zeros((SUBLANES, qw), F32)
        for i, piece in enumerate(LOG2E_PIECES):
            bias = jnp.where(b_row == i, slope * piece, bias)
            bias = jnp.where(b_row == len(LOG2E_PIECES) + i, slope * (piece * blk), bias)
        qs = (qm * (scale * LOG2E)).astype(BF16)
        past = jnp.concatenate([jnp.where(chosen > 0.0, 0.0, NEG), bias, pad], axis=0)
        own = jnp.concatenate([jnp.where(n_t == j, 0.0, NEG), bias, pad], axis=0)
        qa_ref[h] = jnp.concatenate([qs, past.astype(BF16)], axis=0)
        q_own.append(jnp.concatenate([qs, own.astype(BF16)], axis=0))

    def score(g, buf):
        s_ref, top_ref = buf
        for pr in range(MOBA_HEADS // 2):
            k_g = keys(pl.multiple_of(g * span, span), span, pr)
            for h in (2 * pr, 2 * pr + 1):
                s = _dot(k_g, qa_ref[h])
                s_ref[h] = s
                top_ref[h] = jnp.max(s, axis=0, keepdims=True)

    def attend(g, buf):
        s_ref, top_ref = buf
        vt_g = vt_ref[:, pl.ds(pl.multiple_of(g * span, span), span)]
        for h in range(MOBA_HEADS):
            m_old = m_ref[h]
            m_new = jnp.maximum(m_old, top_ref[h])
            p = jnp.exp2((s_ref[h] - m_new).astype(BF16))
            m_ref[h] = m_new
            pv = _dot(vt_g[h * V_ROWS:(h + 1) * V_ROWS, :], p)
            acc_ref[h] = jnp.exp2(m_old - m_new) * acc_ref[h] + pv

    last = nblk // MOBA_GROUP - 1
    buf_a = (sa_ref, ta_ref)
    buf_b = (sb_ref, tb_ref)

    def chain(g0, count, more):
        for t in range(count):
            cur, nxt = (buf_a, buf_b) if t % 2 == 0 else (buf_b, buf_a)
            if t + 1 < count or more:
                score(jnp.minimum(g0 + t + 1, last), nxt)
            attend(g0 + t, cur)

    j_hi = j_lo + MOBA_QBLOCKS - 1
    groups = (j_hi + MOBA_GROUP - 1) // MOBA_GROUP
    nlong = groups // MOBA_LONG
    left = groups % MOBA_LONG
    score(0, buf_a)

    for qb in range(MOBA_QBLOCKS):
        cols = slice(qb * blk, (qb + 1) * blk)
        j0 = pl.multiple_of((j_lo + qb) * blk, blk)
        vt_own = vt_ref[:, pl.ds(j0, blk)]
        for h in range(MOBA_HEADS):
            s = jnp.where(causal, _dot(keys(j0, blk, h // 2), q_own[h][:, cols]), NEG)
            m = jnp.max(s, axis=0, keepdims=True)
            p = jnp.exp2((s - m).astype(BF16))
            m_ref[h, :, cols] = m
            acc_ref[h, :, cols] = _dot(vt_own[h * V_ROWS:(h + 1) * V_ROWS, :], p)

    def body(i, carry):
        chain(i * MOBA_LONG, MOBA_LONG, True)
        return carry

    lax.fori_loop(0, nlong, body, 0)
    g0 = nlong * MOBA_LONG
    size = MOBA_LONG // 2
    while size >= 1:
        take = (left & size) != 0

        @pl.when(take)
        def _(g0=g0, size=size):
            chain(g0, size, size > 1)

        g0 = g0 + jnp.where(take, size, 0)
        size //= 2

    out = []
    for h in range(MOBA_HEADS):
        acc = acc_ref[h]
        out.append(acc[0:hd] / acc[hd:hd + 1])
    o_ref[...] = jnp.concatenate(out, axis=0).T


def _moba(qt, k, vt, km, bsz, seq):
    nblk = seq // MOBA_BLOCK
    blk = MOBA_BLOCK
    pair = 2 * ATT_HD
    width = MOBA_HEADS * ATT_HD
    assert nblk + SUBLANES <= pair and 2 * len(LOG2E_PIECES) <= SUBLANES
    assert MOBA_LONG & (MOBA_LONG - 1) == 0 and MOBA_LONG >= 2
    assert nblk % MOBA_GROUP == 0 and nblk % MOBA_QBLOCKS == 0
    nq = nblk // MOBA_QBLOCKS
    qw = MOBA_QBLOCKS * blk
    pos = jnp.arange(seq, dtype=jnp.int32)[:, None]
    col = jnp.arange(pair, dtype=jnp.int32)[None, :]
    extras = [pos % blk] * len(LOG2E_PIECES) + [pos // blk] * len(LOG2E_PIECES)
    kx = (pos // blk == col).astype(F32)
    for i, e in enumerate(extras):
        kx = jnp.where(col == nblk + i, e.astype(F32), kx)
    kx = kx.astype(BF16)
    return pl.pallas_call(
        functools.partial(_moba_kernel, nblk),
        grid=(bsz, ATT_HEADS // MOBA_HEADS, nq),
        in_specs=[pl.BlockSpec((width, qw), lambda b, p, j: (p, b * nq + j)),
                  pl.BlockSpec((seq, width), lambda b, p, j: (b, p)),
                  pl.BlockSpec((MOBA_HEADS * V_ROWS, seq), lambda b, p, j: (p, b)),
                  pl.BlockSpec((nblk, width), lambda b, p, j: (b, p)),
                  pl.BlockSpec((seq, pair), lambda b, p, j: (0, 0))],
        out_specs=pl.BlockSpec((qw, width), lambda b, p, j: (b * nq + j, p)),
        out_shape=jax.ShapeDtypeStruct((bsz * seq, ATT_HEADS * ATT_HD), F32),
        scratch_shapes=[pltpu.VMEM((MOBA_HEADS, 2 * pair, qw), BF16),
                        pltpu.VMEM((MOBA_HEADS, MOBA_GROUP * blk, qw), F32),
                        pltpu.VMEM((MOBA_HEADS, MOBA_GROUP * blk, qw), F32),
                        pltpu.VMEM((MOBA_HEADS, 1, qw), F32),
                        pltpu.VMEM((MOBA_HEADS, 1, qw), F32),
                        pltpu.VMEM((MOBA_HEADS, 1, qw), F32),
                        pltpu.VMEM((MOBA_HEADS, V_ROWS, qw), F32)],
        compiler_params=_params("parallel", "parallel", "arbitrary"),
        name="moba",
    )(qt, k, vt, km, kx)


def _out_mlp_kernel(final, x_ref, ya_ref, yb_ref, yc_ref, an_ref, wo_ref, gm_ref, w1_ref, w2_ref,
                    gf_ref, o_ref):
    yc = _rms(yc_ref[...], an_ref[...]).astype(BF16)
    y = jnp.concatenate([ya_ref[...], yb_ref[...], yc], axis=1)
    x = x_ref[...] + _dot(y, wo_ref[...])
    hb = _rms(x, gm_ref[...]).astype(BF16)
    mlp = None
    for c in range(w1_ref.shape[1] // FF_CHUNK):
        cols = slice(c * FF_CHUNK, (c + 1) * FF_CHUNK)
        u = jnp.square(jnp.maximum(_dot(hb, w1_ref[:, cols]), 0.0))
        part = _dot(u.astype(BF16), w2_ref[cols, :])
        mlp = part if mlp is None else mlp + part
    x = x + mlp
    if final:
        x = _rms(x, gf_ref[...])
    o_ref[...] = x


def _out_mlp(x2, ya, yb, yc, an, wo, gm, w1, w2, gf, layer, final):
    t, d = x2.shape
    stacked = lambda a: pl.BlockSpec((None,) + a.shape[1:], lambda i: (layer, 0, 0))
    row = lambda a: pl.BlockSpec((ROW_TILE, a.shape[1]), lambda i: (i, 0))
    full = lambda a: pl.BlockSpec(a.shape, lambda i: (0, 0))
    return pl.pallas_call(
        functools.partial(_out_mlp_kernel, final),
        grid=(t // ROW_TILE,),
        in_specs=[row(x2), row(ya), row(yb), row(yc), full(an), stacked(wo), full(gm), stacked(w1),
                  stacked(w2), full(gf)],
        out_specs=pl.BlockSpec((ROW_TILE, d), lambda i: (i, 0)),
        out_shape=jax.ShapeDtypeStruct((t, d), F32),
        compiler_params=_params("parallel"),
        name="out_mlp",
    )(x2, ya, yb, yc, an, wo, gm, w1, w2, gf)


def _block_diag(w):
    g, n, _ = w.shape
    eye = jnp.eye(g, dtype=w.dtype)
    return (eye[:, None, :, None] * w[:, :, None, :]).reshape(g * n, g * n)


def kernel(x, w_in, w_out, norm_mix, norm_mlp, lru_conv_w, lru_conv_b, lru_wa, lru_ba, lru_wx, lru_bx,
           lru_lambda, hg_lower_bounds, hg_norm_w, lru_out_norm, att_out_norm, w_ff1, w_ff2, norm_final):
    bsz, seq, d = x.shape
    depth = w_in.shape[0]
    x2 = x.reshape(bsz * seq, d)
    r2 = lambda a: a.reshape(1, -1)
    w_in_b = w_in.astype(BF16)
    w_qv = jnp.concatenate([w_in[:, :, 1536:2048], w_in[:, :, 2560:3072]], axis=2)
    wt_b = jnp.swapaxes(lax.optimization_barrier(w_qv), 1, 2).astype(BF16)
    w_out_b, w_ff1_b, w_ff2_b = w_out.astype(BF16), w_ff1.astype(BF16), w_ff2.astype(BF16)
    for l in range(depth):
        wg = jnp.concatenate([_block_diag(lru_wa[l]), _block_diag(lru_wx[l])], axis=1).astype(BF16)
        bg = jnp.concatenate([lru_ba[l], lru_bx[l]]).reshape(1, -1)
        lru_params = (lru_conv_w[l], r2(lru_conv_b[l]), wg, bg, r2(lru_lambda[l]), r2(lru_out_norm[l]))
        ya, hg, cqt, ck, cvt, km = _in_proj(x2, r2(norm_mix[l]), w_in_b, wt_b, l, seq, lru_params)
        yb = _hgrn2(hg, bsz, seq, hg_lower_bounds, r2(jnp.tile(hg_norm_w[l], HG_HEADS)), l)
        yc = _moba(cqt, ck, cvt, km.reshape(-1, km.shape[-1]), bsz, seq)
        x2 = _out_mlp(x2, ya, yb, yc, r2(att_out_norm[l]), w_out_b, r2(norm_mlp[l]), w_ff1_b, w_ff2_b,
                      r2(norm_final), l, l == depth - 1)
    return x2.reshape(bsz, seq, d)
```

```python
import functools
import math
import struct

import jax
import jax.numpy as jnp
from jax import lax
from jax.experimental import pallas as pl
from jax.experimental.pallas import tpu as pltpu

F32 = jnp.float32
BF16 = jnp.bfloat16
HIGHEST = lax.Precision.HIGHEST

EPS = 1e-6
NEG = -1e30
TINY = 1e-30

LRU_WIDTH = 256
LRU_CONV = 4
LRU_C = 8.0
HG_HEADS = 4
HG_DK = 64
ATT_HEADS = 8
ATT_HD = 64
MOBA_BLOCK = 256
MOBA_TOPK = 3

HG_WIDTH = HG_HEADS * HG_DK
ATT_WIDTH = ATT_HEADS * ATT_HD
COLS_LRU = (0, 2 * LRU_WIDTH)
COLS_HG = (COLS_LRU[1], COLS_LRU[1] + 4 * HG_WIDTH)
COLS_Q = (COLS_HG[1], COLS_HG[1] + ATT_WIDTH)
COLS_K = (COLS_Q[1], COLS_Q[1] + ATT_WIDTH)
COLS_V = (COLS_K[1], COLS_K[1] + ATT_WIDTH)

SUBLANES = 8
VMEM_LIMIT = 56 * 1024 * 1024

ROW_TILE = 512
LRU_SUB = 256
HG_TILE = 512
HG_CHUNK = 256
FF_CHUNK = 1024
MOBA_GROUP = 2
MOBA_LONG = 8
ONES_ROWS = 16
V_ROWS = ATT_HD + ONES_ROWS


def _bf16_pieces(x, n):
    pieces = []
    for _ in range(n):
        bits = struct.unpack("<I", struct.pack("<f", x))[0]
        bits = (bits + 0x7FFF + ((bits >> 16) & 1)) & 0xFFFF0000
        piece = struct.unpack("<f", struct.pack("<I", bits))[0]
        pieces.append(piece)
        x -= piece
    return tuple(pieces)


LOG2E = math.log2(math.e)
LOG2E_PIECES = _bf16_pieces(LOG2E, 3)


def _dot(a, b):
    return jnp.dot(a, b, preferred_element_type=F32)


def _dot_t(a, b):
    return lax.dot_general(a, b, (((1,), (1,)), ((), ())), preferred_element_type=F32)


def _rms(x, g):
    return x * lax.rsqrt(jnp.mean(x * x, axis=-1, keepdims=True) + EPS) * g


def _sigmoid(x):
    return 1.0 / (1.0 + jnp.exp(-x))


def _softplus(x):
    return jnp.maximum(x, 0.0) + jnp.log(1.0 + jnp.exp(-jnp.abs(x)))


def _params(*sem):
    return pltpu.CompilerParams(dimension_semantics=sem, vmem_limit_bytes=VMEM_LIMIT)


def _in_proj_kernel(seq, x_ref, g_ref, w_ref, wt_ref, cw_ref, cb_ref, wg_ref, bg_ref, lam_ref, nw_ref,
                    ya_ref, hg_ref, qt_ref, k_ref, vt_ref, km_ref, prev_ref, h_ref):
    @pl.when((pl.program_id(0) * ROW_TILE) % seq == 0)
    def _():
        prev_ref[...] = jnp.zeros_like(prev_ref)
        h_ref[...] = jnp.zeros_like(h_ref)

    hb = _rms(x_ref[...], g_ref[...]).astype(BF16)
    _rglru_tile(_dot(hb, w_ref[:, COLS_LRU[0]:COLS_LRU[1]]), cw_ref, cb_ref, wg_ref, bg_ref, lam_ref,
                nw_ref, ya_ref, prev_ref, h_ref)
    hg_ref[...] = _dot(hb, w_ref[:, COLS_HG[0]:COLS_HG[1]])
    qt_ref[...] = _dot_t(wt_ref[0:ATT_WIDTH, :], hb)
    k = _dot(hb, w_ref[:, COLS_K[0]:COLS_K[1]])
    k_ref[...] = k.astype(BF16)
    for i in range(ROW_TILE // MOBA_BLOCK):
        km_ref[i] = jnp.mean(k[i * MOBA_BLOCK:(i + 1) * MOBA_BLOCK], axis=0, keepdims=True)
    vt = _dot_t(wt_ref[ATT_WIDTH:2 * ATT_WIDTH, :], hb).astype(BF16)
    ones = jnp.ones((ONES_ROWS, ROW_TILE), BF16)
    for h in range(ATT_HEADS):
        vt_ref[h * V_ROWS:h * V_ROWS + ATT_HD, :] = vt[h * ATT_HD:(h + 1) * ATT_HD, :]
        vt_ref[h * V_ROWS + ATT_HD:(h + 1) * V_ROWS, :] = ones


def _in_proj(x2, g, w, wt, layer, seq, lru_params):
    t, d = x2.shape
    nblk = t // MOBA_BLOCK
    stacked = lambda a: pl.BlockSpec((None,) + a.shape[1:], lambda i: (layer, 0, 0))
    vec = lambda a: pl.BlockSpec(a.shape, lambda i: (0, 0))
    row = lambda width: pl.BlockSpec((ROW_TILE, width), lambda i: (i, 0))
    col = lambda height: pl.BlockSpec((height, ROW_TILE), lambda i: (0, i))
    assert seq % ROW_TILE == 0 and ROW_TILE % LRU_SUB == 0 and ROW_TILE % MOBA_BLOCK == 0
    return pl.pallas_call(
        functools.partial(_in_proj_kernel, seq),
        grid=(t // ROW_TILE,),
        in_specs=[row(d), vec(g), stacked(w), stacked(wt)] + [vec(a) for a in lru_params],
        out_specs=[row(LRU_WIDTH), row(4 * HG_WIDTH), col(ATT_WIDTH), row(ATT_WIDTH),
                   col(ATT_HEADS * V_ROWS),
                   pl.BlockSpec((ROW_TILE // MOBA_BLOCK, 1, ATT_WIDTH), lambda i: (i, 0, 0))],
        out_shape=[jax.ShapeDtypeStruct((t, LRU_WIDTH), BF16),
                   jax.ShapeDtypeStruct((t, 4 * HG_WIDTH), F32),
                   jax.ShapeDtypeStruct((ATT_WIDTH, t), F32),
                   jax.ShapeDtypeStruct((t, ATT_WIDTH), BF16),
                   jax.ShapeDtypeStruct((ATT_HEADS * V_ROWS, t), BF16),
                   jax.ShapeDtypeStruct((nblk, 1, ATT_WIDTH), F32)],
        scratch_shapes=[pltpu.VMEM((LRU_SUB, LRU_WIDTH), F32), pltpu.VMEM((1, LRU_WIDTH), F32)],
        compiler_params=_params("arbitrary"),
        name="in_proj",
    )(x2, g, w, wt, *lru_params)


def _rglru_tile(xy, cw_ref, cb_ref, wg_ref, bg_ref, lam_ref, nw_ref, o_ref, prev_ref, h_ref):
    w = LRU_WIDTH
    row = lax.broadcasted_iota(jnp.int32, (LRU_SUB, w), 0)
    sp = _softplus(-lam_ref[...])
    for c in range(ROW_TILE // LRU_SUB):
        rows = slice(c * LRU_SUB, (c + 1) * LRU_SUB)
        xb = xy[rows, 0:w]
        yb = xy[rows, w:2 * w]
        prev = prev_ref[...]
        xc = xb * cw_ref[LRU_CONV - 1:LRU_CONV, :] + cb_ref[...]
        for k in range(1, LRU_CONV):
            shifted = jnp.where(row >= k, pltpu.roll(xb, k, 0), pltpu.roll(prev, k, 0))
            xc = xc + shifted * cw_ref[LRU_CONV - 1 - k:LRU_CONV - k, :]
        prev_ref[...] = xb
        gates = _dot(xc.astype(BF16), wg_ref[...]) + bg_ref[...]
        r = _sigmoid(gates[:, 0:w])
        i = _sigmoid(gates[:, w:2 * w])
        log_a = -LRU_C * r * sp
        a = jnp.exp(log_a)
        b = jnp.sqrt(jnp.maximum(1.0 - a * a, 0.0)) * (i * xc)
        s = 1
        while s < LRU_SUB:
            keep = row >= s
            a_s = jnp.where(keep, pltpu.roll(a, s, 0), 1.0)
            b_s = jnp.where(keep, pltpu.roll(b, s, 0), 0.0)
            b = a * b_s + b
            a = a * a_s
            s *= 2
        h = b + a * h_ref[...]
        h_ref[...] = h[LRU_SUB - 1:LRU_SUB, :]
        y = h * jax.nn.gelu(yb, approximate=True)
        o_ref[rows, :] = _rms(y, nw_ref[...]).astype(o_ref.dtype)


def _hgrn2_kernel(layer, z_ref, lbp_ref, nw_ref, o_ref, state_ref):
    @pl.when(pl.program_id(1) == 0)
    def _():
        state_ref[...] = jnp.zeros_like(state_ref)

    n = HG_WIDTH
    L = HG_CHUNK
    lbp = lbp_ref[...]
    e = jnp.exp(lbp - jnp.max(lbp, axis=0, keepdims=True))
    soft = e / jnp.sum(e, axis=0, keepdims=True)
    lb = jnp.sum(soft[0:layer + 1], axis=0, keepdims=True) - soft[0:1]
    log_lb = jnp.log(jnp.maximum(lb, TINY))
    log_1m = jnp.log1p(-lb)

    lane_h = lax.broadcasted_iota(jnp.int32, (1, n), 1) // HG_DK
    head_of_row = lax.broadcasted_iota(jnp.int32, (n, n), 0) // HG_DK
    head_of_col = lax.broadcasted_iota(jnp.int32, (n, n), 1) // HG_DK
    same_head = head_of_row == head_of_col
    ones_bd = jnp.where(same_head, 1.0, 0.0).astype(BF16)
    row = lax.broadcasted_iota(jnp.int32, (L, n), 0)
    sub = lax.broadcasted_iota(jnp.int32, (1, SUBLANES, 1), 1)
    t_idx = lax.broadcasted_iota(jnp.int32, (L, L), 0)
    s_idx = lax.broadcasted_iota(jnp.int32, (L, L), 1)

    def chunk(ci, carry):
        r0 = pl.multiple_of(ci * L, L)
        q = z_ref[pl.ds(r0, L), 0:n]
        f = z_ref[pl.ds(r0, L), n:2 * n]
        v = z_ref[pl.ds(r0, L), 2 * n:3 * n]
        g = z_ref[pl.ds(r0, L), 3 * n:4 * n]
        lsig = -_softplus(-f)
        t1 = log_1m + lsig
        log_f = jnp.maximum(log_lb, t1) + jnp.log(1.0 + jnp.exp(-jnp.abs(log_lb - t1)))
        kk = (1.0 - lb) * _sigmoid(-f)
        qs = q * _sigmoid(q)
        b = log_f * LOG2E
        s = 1
        while s < L:
            b = b + jnp.where(row >= s, pltpu.roll(b, s, 0), 0.0)
            s *= 2

        g8 = L // SUBLANES
        q3 = qs.reshape(g8, SUBLANES, n)
        k3 = kk.reshape(g8, SUBLANES, n)
        b3 = b.reshape(g8, SUBLANES, n)
        v3 = v.reshape(g8, SUBLANES, n)
        o = jnp.zeros((L, n), F32)
        for j in range(SUBLANES):
            d = jnp.where(sub >= j, b3 - b3[:, j:j + 1, :], NEG)
            xj = q3 * k3[:, j:j + 1, :] * jnp.exp2(d)
            aj = _dot(xj.reshape(L, n).astype(BF16), ones_bd)
            o = o + (aj.reshape(g8, SUBLANES, n) * v3[:, j:j + 1, :]).reshape(L, n)

        att = [jnp.zeros((L, L), F32) for _ in range(HG_HEADS)]
        c = SUBLANES
        while c < L:
            bc = b.reshape(L // c, c, n)
            bend = jnp.broadcast_to(bc[:, c - 1:c, :], (L // c, c, n)).reshape(L, n)
            bprev = jnp.where(row >= c, pltpu.roll(bend, c, 0), 0.0)
            qt = qs * jnp.exp2(b - bprev)
            kt = (kk * jnp.exp2(bend - b)).astype(BF16)
            pair = ((t_idx // c) % 2 == 1) & ((s_idx // c) == (t_idx // c) - 1)
            for h in range(HG_HEADS):
                sc = _dot_t(jnp.where(lane_h == h, qt, 0.0).astype(BF16), kt)
                att[h] = att[h] + jnp.where(pair, sc, 0.0)
            c *= 2
        att_cat = jnp.concatenate([a.astype(BF16) for a in att], axis=1)
        v_stack = jnp.concatenate([jnp.where(lane_h == h, v, 0.0).astype(BF16)
                                   for h in range(HG_HEADS)], axis=0)
        o = o + _dot(att_cat, v_stack)

        state = state_ref[...]
        o = o + _dot_t((qs * jnp.exp2(b)).astype(BF16), state.astype(BF16))
        b_last = b[L - 1:L, :]
        kdec = (kk * jnp.exp2(b_last - b)).astype(BF16)
        upd = lax.dot_general(v.astype(BF16), kdec, (((0,), (0,)), ((), ())),
                              preferred_element_type=F32)
        state_ref[...] = state * jnp.exp2(b_last) + jnp.where(same_head, upd, 0.0)

        ms = _dot((o * o).astype(BF16), ones_bd) * (1.0 / HG_DK)
        y = o * lax.rsqrt(ms + EPS) * nw_ref[...] * (g * _sigmoid(g))
        o_ref[pl.ds(r0, L), :] = y.astype(o_ref.dtype)
        return carry

    lax.fori_loop(0, HG_TILE // L, chunk, 0)


def _hgrn2(z, bsz, seq, lbp, nw, layer):
    nt = seq // HG_TILE
    n = HG_WIDTH
    assert seq % HG_TILE == 0 and HG_TILE % HG_CHUNK == 0
    return pl.pallas_call(
        functools.partial(_hgrn2_kernel, layer),
        grid=(bsz, nt),
        in_specs=[pl.BlockSpec((HG_TILE, 4 * n), lambda b, t: (b * nt + t, 0)),
                  pl.BlockSpec(lbp.shape, lambda b, t: (0, 0)),
                  pl.BlockSpec(nw.shape, lambda b, t: (0, 0))],
        out_specs=pl.BlockSpec((HG_TILE, n), lambda b, t: (b * nt + t, 0)),
        out_shape=jax.ShapeDtypeStruct((bsz * seq, n), BF16),
        scratch_shapes=[pltpu.VMEM((n, n), F32)],
        compiler_params=_params("parallel", "arbitrary"),
        name="hgrn2",
    )(z, lbp, nw)


def _moba_kernel(nblk, qt_ref, k_ref, vt_ref, km_ref, kx_ref, o_ref, qa_ref, sa_ref, sb_ref,
                 ta_ref, tb_ref, m_ref, acc_ref):
    j = pl.program_id(2)
    hp = pl.program_id(1)
    blk = MOBA_BLOCK
    hd = ATT_HD
    pair = 2 * hd
    span = MOBA_GROUP * blk
    scale = 1.0 / math.sqrt(hd)
    feat = lax.broadcasted_iota(jnp.int32, (pair, 1), 0)
    n_t = lax.broadcasted_iota(jnp.int32, (nblk, blk), 0)
    b_row = lax.broadcasted_iota(jnp.int32, (SUBLANES, blk), 0)
    causal = (lax.broadcasted_iota(jnp.int32, (blk, blk), 0)
              <= lax.broadcasted_iota(jnp.int32, (blk, blk), 1))
    qt = qt_ref[...]
    km = km_ref[...]

    def keys(start, size):
        return jnp.concatenate([k_ref[pl.ds(start, size), :], kx_ref[pl.ds(start, size), :]], axis=1)

    pad = jnp.zeros((pair - nblk - SUBLANES, blk), F32)
    q_own = []
    for h in range(2):
        slope = jnp.exp2(-8.0 * (2 * hp + h + 1).astype(F32) / ATT_HEADS)
        qm = jnp.where(feat // hd == h, qt, 0.0)
        gate = jnp.dot(km, qm, precision=HIGHEST, preferred_element_type=F32)
        gate = jnp.where(n_t < j, gate, NEG)
        chosen = jnp.zeros((nblk, blk), F32)
        for r in range(MOBA_TOPK):
            top = jnp.max(gate, axis=0, keepdims=True)
            first = jnp.min(jnp.where(gate == top, n_t, nblk), axis=0, keepdims=True)
            pick = n_t == first
            chosen = jnp.where(pick, jnp.where(j > r, 1.0, 0.0), chosen)
            gate = jnp.where(pick, -jnp.inf, gate)
        bias = jnp.zeros((SUBLANES, blk), F32)
        for i, piece in enumerate(LOG2E_PIECES):
            bias = jnp.where(b_row == i, slope * piece, bias)
            bias = jnp.where(b_row == len(LOG2E_PIECES) + i, slope * (piece * blk), bias)
        qs = (qm * (scale * LOG2E)).astype(BF16)
        past = jnp.concatenate([jnp.where(chosen > 0.0, 0.0, NEG), bias, pad], axis=0)
        own = jnp.concatenate([jnp.where(n_t == j, 0.0, NEG), bias, pad], axis=0)
        qa_ref[h] = jnp.concatenate([qs, past.astype(BF16)], axis=0)
        q_own.append(jnp.concatenate([qs, own.astype(BF16)], axis=0))

    def score(g, buf):
        s_ref, top_ref = buf
        k_g = keys(pl.multiple_of(g * span, span), span)
        for h in range(2):
            s = _dot(k_g, qa_ref[h])
            s_ref[h] = s
            top_ref[h] = jnp.max(s, axis=0, keepdims=True)

    def attend(g, buf):
        s_ref, top_ref = buf
        vt_g = vt_ref[:, pl.ds(pl.multiple_of(g * span, span), span)]
        for h in range(2):
            m_old = m_ref[h]
            m_new = jnp.maximum(m_old, top_ref[h])
            p = jnp.exp2((s_ref[h] - m_new).astype(BF16))
            m_ref[h] = m_new
            pv = _dot(vt_g[h * V_ROWS:(h + 1) * V_ROWS, :], p)
            acc_ref[h] = jnp.exp2(m_old - m_new) * acc_ref[h] + pv

    last = nblk // MOBA_GROUP - 1
    buf_a = (sa_ref, ta_ref)
    buf_b = (sb_ref, tb_ref)

    def chain(g0, count, more):
        for t in range(count):
            cur, nxt = (buf_a, buf_b) if t % 2 == 0 else (buf_b, buf_a)
            if t + 1 < count or more:
                score(jnp.minimum(g0 + t + 1, last), nxt)
            attend(g0 + t, cur)

    groups = (j + MOBA_GROUP - 1) // MOBA_GROUP
    nlong = groups // MOBA_LONG
    left = groups % MOBA_LONG
    score(0, buf_a)

    j0 = pl.multiple_of(j * blk, blk)
    k_own = keys(j0, blk)
    vt_own = vt_ref[:, pl.ds(j0, blk)]
    for h in range(2):
        s = jnp.where(causal, _dot(k_own, q_own[h]), NEG)
        m = jnp.max(s, axis=0, keepdims=True)
        p = jnp.exp2((s - m).astype(BF16))
        m_ref[h] = m
        acc_ref[h] = _dot(vt_own[h * V_ROWS:(h + 1) * V_ROWS, :], p)

    def body(i, carry):
        chain(i * MOBA_LONG, MOBA_LONG, True)
        return carry

    lax.fori_loop(0, nlong, body, 0)
    g0 = nlong * MOBA_LONG
    size = MOBA_LONG // 2
    while size >= 1:
        take = (left & size) != 0

        @pl.when(take)
        def _(g0=g0, size=size):
            chain(g0, size, size > 1)

        g0 = g0 + jnp.where(take, size, 0)
        size //= 2

    out = []
    for h in range(2):
        acc = acc_ref[h]
        out.append(acc[0:hd] / acc[hd:hd + 1])
    o_ref[...] = jnp.concatenate(out, axis=0).T


def _moba(qt, k, vt, km, bsz, seq):
    nblk = seq // MOBA_BLOCK
    blk = MOBA_BLOCK
    pair = 2 * ATT_HD
    assert nblk + SUBLANES <= pair and 2 * len(LOG2E_PIECES) <= SUBLANES
    assert MOBA_LONG & (MOBA_LONG - 1) == 0 and MOBA_LONG >= 2
    assert nblk % MOBA_GROUP == 0
    pos = jnp.arange(seq, dtype=jnp.int32)[:, None]
    col = jnp.arange(pair, dtype=jnp.int32)[None, :]
    extras = [pos % blk] * len(LOG2E_PIECES) + [pos // blk] * len(LOG2E_PIECES)
    kx = (pos // blk == col).astype(F32)
    for i, e in enumerate(extras):
        kx = jnp.where(col == nblk + i, e.astype(F32), kx)
    kx = kx.astype(BF16)
    return pl.pallas_call(
        functools.partial(_moba_kernel, nblk),
        grid=(bsz, ATT_HEADS // 2, nblk),
        in_specs=[pl.BlockSpec((pair, blk), lambda b, p, j: (p, b * nblk + j)),
                  pl.BlockSpec((seq, pair), lambda b, p, j: (b, p)),
                  pl.BlockSpec((2 * V_ROWS, seq), lambda b, p, j: (p, b)),
                  pl.BlockSpec((nblk, pair), lambda b, p, j: (b, p)),
                  pl.BlockSpec((seq, pair), lambda b, p, j: (0, 0))],
        out_specs=pl.BlockSpec((blk, pair), lambda b, p, j: (b * nblk + j, p)),
        out_shape=jax.ShapeDtypeStruct((bsz * seq, ATT_WIDTH), F32),
        scratch_shapes=[pltpu.VMEM((2, 2 * pair, blk), BF16),
                        pltpu.VMEM((2, MOBA_GROUP * blk, blk), F32),
                        pltpu.VMEM((2, MOBA_GROUP * blk, blk), F32),
                        pltpu.VMEM((2, 1, blk), F32),
                        pltpu.VMEM((2, 1, blk), F32),
                        pltpu.VMEM((2, 1, blk), F32),
                        pltpu.VMEM((2, V_ROWS, blk), F32)],
        compiler_params=_params("parallel", "parallel", "arbitrary"),
        name="moba",
    )(qt, k, vt, km, kx)


def _out_mlp_kernel(final, x_ref, ya_ref, yb_ref, yc_ref, an_ref, wo_ref, gm_ref, w1_ref, w2_ref,
                    gf_ref, o_ref):
    yc = _rms(yc_ref[...], an_ref[...]).astype(BF16)
    y = jnp.concatenate([ya_ref[...], yb_ref[...], yc], axis=1)
    x = x_ref[...] + _dot(y, wo_ref[...])
    hb = _rms(x, gm_ref[...]).astype(BF16)
    mlp = None
    for c in range(w1_ref.shape[1] // FF_CHUNK):
        cols = slice(c * FF_CHUNK, (c + 1) * FF_CHUNK)
        u = jnp.square(jnp.maximum(_dot(hb, w1_ref[:, cols]), 0.0))
        part = _dot(u.astype(BF16), w2_ref[cols, :])
        mlp = part if mlp is None else mlp + part
    x = x + mlp
    if final:
        x = _rms(x, gf_ref[...])
    o_ref[...] = x


def _out_mlp(x2, ya, yb, yc, an, wo, gm, w1, w2, gf, layer, final):
    t, d = x2.shape
    stacked = lambda a: pl.BlockSpec((None,) + a.shape[1:], lambda i: (layer, 0, 0))
    row = lambda a: pl.BlockSpec((ROW_TILE, a.shape[1]), lambda i: (i, 0))
    full = lambda a: pl.BlockSpec(a.shape, lambda i: (0, 0))
    assert t % ROW_TILE == 0 and w1.shape[2] % FF_CHUNK == 0
    return pl.pallas_call(
        functools.partial(_out_mlp_kernel, final),
        grid=(t // ROW_TILE,),
        in_specs=[row(x2), row(ya), row(yb), row(yc), full(an), stacked(wo), full(gm), stacked(w1),
                  stacked(w2), full(gf)],
        out_specs=pl.BlockSpec((ROW_TILE, d), lambda i: (i, 0)),
        out_shape=jax.ShapeDtypeStruct((t, d), F32),
        compiler_params=_params("parallel"),
        name="out_mlp",
    )(x2, ya, yb, yc, an, wo, gm, w1, w2, gf)


def _block_diag(w):
    g, n, _ = w.shape
    eye = jnp.eye(g, dtype=w.dtype)
    return (eye[:, None, :, None] * w[:, :, None, :]).reshape(g * n, g * n)


def kernel(x, w_in, w_out, norm_mix, norm_mlp, lru_conv_w, lru_conv_b, lru_wa, lru_ba, lru_wx, lru_bx,
           lru_lambda, hg_lower_bounds, hg_norm_w, lru_out_norm, att_out_norm, w_ff1, w_ff2, norm_final):
    bsz, seq, d = x.shape
    depth = w_in.shape[0]
    x2 = x.reshape(bsz * seq, d)
    r2 = lambda a: a.reshape(1, -1)
    w_in_b = w_in.astype(BF16)
    w_qv = jnp.concatenate([w_in[:, :, COLS_Q[0]:COLS_Q[1]], w_in[:, :, COLS_V[0]:COLS_V[1]]], axis=2)
    wt_b = jnp.swapaxes(lax.optimization_barrier(w_qv), 1, 2).astype(BF16)
    w_out_b, w_ff1_b, w_ff2_b = w_out.astype(BF16), w_ff1.astype(BF16), w_ff2.astype(BF16)
    for l in range(depth):
        wg = jnp.concatenate([_block_diag(lru_wa[l]), _block_diag(lru_wx[l])], axis=1).astype(BF16)
        bg = jnp.concatenate([lru_ba[l], lru_bx[l]]).reshape(1, -1)
        lru_params = (lru_conv_w[l], r2(lru_conv_b[l]), wg, bg, r2(lru_lambda[l]), r2(lru_out_norm[l]))
        ya, hg, cqt, ck, cvt, km = _in_proj(x2, r2(norm_mix[l]), w_in_b, wt_b, l, seq, lru_params)
        yb = _hgrn2(hg, bsz, seq, hg_lower_bounds, r2(jnp.tile(hg_norm_w[l], HG_HEADS)), l)
        yc = _moba(cqt, ck, cvt, km.reshape(-1, km.shape[-1]), bsz, seq)
        x2 = _out_mlp(x2, ya, yb, yc, r2(att_out_norm[l]), w_out_b, r2(norm_mlp[l]), w_ff1_b, w_ff2_b,
                      r2(norm_final), l, l == depth - 1)
    return x2.reshape(bsz, seq, d)
```

```python
import functools
import math
import struct

import jax
import jax.numpy as jnp
from jax import lax
from jax.experimental import pallas as pl
from jax.experimental.pallas import tpu as pltpu

F32 = jnp.float32
BF16 = jnp.bfloat16
HIGHEST = lax.Precision.HIGHEST

EPS = 1e-6
NEG = -1e30
TINY = 1e-30

LRU_WIDTH = 256
LRU_CONV = 4
LRU_C = 8.0
HG_HEADS = 4
HG_DK = 64
ATT_HEADS = 8
ATT_HD = 64
MOBA_BLOCK = 256
MOBA_TOPK = 3

HG_WIDTH = HG_HEADS * HG_DK
ATT_WIDTH = ATT_HEADS * ATT_HD
COLS_LRU = (0, 2 * LRU_WIDTH)
COLS_HG = (COLS_LRU[1], COLS_LRU[1] + 4 * HG_WIDTH)
COLS_Q = (COLS_HG[1], COLS_HG[1] + ATT_WIDTH)
COLS_K = (COLS_Q[1], COLS_Q[1] + ATT_WIDTH)
COLS_V = (COLS_K[1], COLS_K[1] + ATT_WIDTH)

SUBLANES = 8
VMEM_LIMIT = 56 * 1024 * 1024

ROW_TILE = 512
LRU_SUB = 256
HG_TILE = 512
HG_CHUNK = 256
FF_CHUNK = 1024
MOBA_GROUP = 2
MOBA_LONG = 8
ONES_ROWS = 16
V_ROWS = ATT_HD + ONES_ROWS


def _bf16_pieces(x, n):
    pieces = []
    for _ in range(n):
        bits = struct.unpack("<I", struct.pack("<f", x))[0]
        bits = (bits + 0x7FFF + ((bits >> 16) & 1)) & 0xFFFF0000
        piece = struct.unpack("<f", struct.pack("<I", bits))[0]
        pieces.append(piece)
        x -= piece
    return tuple(pieces)


LOG2E = math.log2(math.e)
LOG2E_PIECES = _bf16_pieces(LOG2E, 3)


def _dot(a, b):
    return jnp.dot(a, b, preferred_element_type=F32)


def _dot_t(a, b):
    return lax.dot_general(a, b, (((1,), (1,)), ((), ())), preferred_element_type=F32)


def _rms(x, g):
    return x * lax.rsqrt(jnp.mean(x * x, axis=-1, keepdims=True) + EPS) * g


def _sigmoid(x):
    return 1.0 / (1.0 + jnp.exp(-x))


def _softplus(x):
    return jnp.maximum(x, 0.0) + jnp.log(1.0 + jnp.exp(-jnp.abs(x)))


def _params(*sem):
    return pltpu.CompilerParams(dimension_semantics=sem, vmem_limit_bytes=VMEM_LIMIT)


def _in_proj_kernel(seq, x_ref, g_ref, w_ref, wt_ref, cw_ref, cb_ref, wg_ref, bg_ref, lam_ref, nw_ref,
                    ya_ref, hg_ref, qt_ref, k_ref, vt_ref, km_ref, prev_ref, h_ref):
    @pl.when((pl.program_id(0) * ROW_TILE) % seq == 0)
    def _():
        prev_ref[...] = jnp.zeros_like(prev_ref)
        h_ref[...] = jnp.zeros_like(h_ref)

    hb = _rms(x_ref[...], g_ref[...]).astype(BF16)
    _rglru_tile(_dot(hb, w_ref[:, COLS_LRU[0]:COLS_LRU[1]]), cw_ref, cb_ref, wg_ref, bg_ref, lam_ref,
                nw_ref, ya_ref, prev_ref, h_ref)
    hg_ref[...] = _dot(hb, w_ref[:, COLS_HG[0]:COLS_HG[1]])
    qt_ref[...] = _dot_t(wt_ref[0:ATT_WIDTH, :], hb)
    k = _dot(hb, w_ref[:, COLS_K[0]:COLS_K[1]])
    k_ref[...] = k.astype(BF16)
    for i in range(ROW_TILE // MOBA_BLOCK):
        km_ref[i] = jnp.mean(k[i * MOBA_BLOCK:(i + 1) * MOBA_BLOCK], axis=0, keepdims=True)
    vt = _dot_t(wt_ref[ATT_WIDTH:2 * ATT_WIDTH, :], hb).astype(BF16)
    ones = jnp.ones((ONES_ROWS, ROW_TILE), BF16)
    for h in range(ATT_HEADS):
        vt_ref[h * V_ROWS:h * V_ROWS + ATT_HD, :] = vt[h * ATT_HD:(h + 1) * ATT_HD, :]
        vt_ref[h * V_ROWS + ATT_HD:(h + 1) * V_ROWS, :] = ones


def _in_proj(x2, g, w, wt, layer, seq, lru_params):
    t, d = x2.shape
    nblk = t // MOBA_BLOCK
    stacked = lambda a: pl.BlockSpec((None,) + a.shape[1:], lambda i: (layer, 0, 0))
    vec = lambda a: pl.BlockSpec(a.shape, lambda i: (0, 0))
    row = lambda width: pl.BlockSpec((ROW_TILE, width), lambda i: (i, 0))
    col = lambda height: pl.BlockSpec((height, ROW_TILE), lambda i: (0, i))
    assert seq % ROW_TILE == 0 and ROW_TILE % LRU_SUB == 0 and ROW_TILE % MOBA_BLOCK == 0
    return pl.pallas_call(
        functools.partial(_in_proj_kernel, seq),
        grid=(t // ROW_TILE,),
        in_specs=[row(d), vec(g), stacked(w), stacked(wt)] + [vec(a) for a in lru_params],
        out_specs=[row(LRU_WIDTH), row(4 * HG_WIDTH), col(ATT_WIDTH), row(ATT_WIDTH),
                   col(ATT_HEADS * V_ROWS),
                   pl.BlockSpec((ROW_TILE // MOBA_BLOCK, 1, ATT_WIDTH), lambda i: (i, 0, 0))],
        out_shape=[jax.ShapeDtypeStruct((t, LRU_WIDTH), BF16),
                   jax.ShapeDtypeStruct((t, 4 * HG_WIDTH), F32),
                   jax.ShapeDtypeStruct((ATT_WIDTH, t), F32),
                   jax.ShapeDtypeStruct((t, ATT_WIDTH), BF16),
                   jax.ShapeDtypeStruct((ATT_HEADS * V_ROWS, t), BF16),
                   jax.ShapeDtypeStruct((nblk, 1, ATT_WIDTH), F32)],
        scratch_shapes=[pltpu.VMEM((LRU_SUB, LRU_WIDTH), F32), pltpu.VMEM((1, LRU_WIDTH), F32)],
        compiler_params=_params("arbitrary"),
        name="in_proj",
    )(x2, g, w, wt, *lru_params)


def _rglru_tile(xy, cw_ref, cb_ref, wg_ref, bg_ref, lam_ref, nw_ref, o_ref, prev_ref, h_ref):
    w = LRU_WIDTH
    row = lax.broadcasted_iota(jnp.int32, (LRU_SUB, w), 0)
    sp = _softplus(-lam_ref[...])
    for c in range(ROW_TILE // LRU_SUB):
        rows = slice(c * LRU_SUB, (c + 1) * LRU_SUB)
        xb = xy[rows, 0:w]
        yb = xy[rows, w:2 * w]
        prev = prev_ref[...]
        xc = xb * cw_ref[LRU_CONV - 1:LRU_CONV, :] + cb_ref[...]
        for k in range(1, LRU_CONV):
            shifted = jnp.where(row >= k, pltpu.roll(xb, k, 0), pltpu.roll(prev, k, 0))
            xc = xc + shifted * cw_ref[LRU_CONV - 1 - k:LRU_CONV - k, :]
        prev_ref[...] = xb
        gates = _dot(xc.astype(BF16), wg_ref[...]) + bg_ref[...]
        r = _sigmoid(gates[:, 0:w])
        i = _sigmoid(gates[:, w:2 * w])
        log_a = -LRU_C * r * sp
        a = jnp.exp(log_a)
        b = jnp.sqrt(jnp.maximum(1.0 - a * a, 0.0)) * (i * xc)
        s = 1
        while s < LRU_SUB:
            keep = row >= s
            a_s = jnp.where(keep, pltpu.roll(a, s, 0), 1.0)
            b_s = jnp.where(keep, pltpu.roll(b, s, 0), 0.0)
            b = a * b_s + b
            a = a * a_s
            s *= 2
        h = b + a * h_ref[...]
        h_ref[...] = h[LRU_SUB - 1:LRU_SUB, :]
        y = h * jax.nn.gelu(yb, approximate=True)
        o_ref[rows, :] = _rms(y, nw_ref[...]).astype(o_ref.dtype)


def _hgrn2_kernel(layer, z_ref, lbp_ref, nw_ref, o_ref, state_ref):
    @pl.when(pl.program_id(1) == 0)
    def _():
        state_ref[...] = jnp.zeros_like(state_ref)

    n = HG_WIDTH
    L = HG_CHUNK
    lbp = lbp_ref[...]
    e = jnp.exp(lbp - jnp.max(lbp, axis=0, keepdims=True))
    soft = e / jnp.sum(e, axis=0, keepdims=True)
    lb = jnp.sum(soft[0:layer + 1], axis=0, keepdims=True) - soft[0:1]
    log_lb = jnp.log(jnp.maximum(lb, TINY))
    log_1m = jnp.log1p(-lb)

    lane_h = lax.broadcasted_iota(jnp.int32, (1, n), 1) // HG_DK
    head_of_row = lax.broadcasted_iota(jnp.int32, (n, n), 0) // HG_DK
    head_of_col = lax.broadcasted_iota(jnp.int32, (n, n), 1) // HG_DK
    same_head = head_of_row == head_of_col
    ones_bd = jnp.where(same_head, 1.0, 0.0).astype(BF16)
    row = lax.broadcasted_iota(jnp.int32, (L, n), 0)
    sub = lax.broadcasted_iota(jnp.int32, (1, SUBLANES, 1), 1)
    t_idx = lax.broadcasted_iota(jnp.int32, (L, L), 0)
    s_idx = lax.broadcasted_iota(jnp.int32, (L, L), 1)

    def chunk(ci, carry):
        r0 = pl.multiple_of(ci * L, L)
        q = z_ref[pl.ds(r0, L), 0:n]
        f = z_ref[pl.ds(r0, L), n:2 * n]
        v = z_ref[pl.ds(r0, L), 2 * n:3 * n]
        g = z_ref[pl.ds(r0, L), 3 * n:4 * n]
        lsig = -_softplus(-f)
        t1 = log_1m + lsig
        log_f = jnp.maximum(log_lb, t1) + jnp.log(1.0 + jnp.exp(-jnp.abs(log_lb - t1)))
        kk = (1.0 - lb) * _sigmoid(-f)
        qs = q * _sigmoid(q)
        b = log_f * LOG2E
        s = 1
        while s < L:
            b = b + jnp.where(row >= s, pltpu.roll(b, s, 0), 0.0)
            s *= 2

        g8 = L // SUBLANES
        q3 = qs.reshape(g8, SUBLANES, n)
        k3 = kk.reshape(g8, SUBLANES, n)
        b3 = b.reshape(g8, SUBLANES, n)
        v3 = v.reshape(g8, SUBLANES, n)
        o = jnp.zeros((L, n), F32)
        for j in range(SUBLANES):
            d = jnp.where(sub >= j, b3 - b3[:, j:j + 1, :], NEG)
            xj = q3 * k3[:, j:j + 1, :] * jnp.exp2(d)
            aj = _dot(xj.reshape(L, n).astype(BF16), ones_bd)
            o = o + (aj.reshape(g8, SUBLANES, n) * v3[:, j:j + 1, :]).reshape(L, n)

        att = [jnp.zeros((L, L), F32) for _ in range(HG_HEADS)]
        c = SUBLANES
        while c < L:
            bc = b.reshape(L // c, c, n)
            bend = jnp.broadcast_to(bc[:, c - 1:c, :], (L // c, c, n)).reshape(L, n)
            bprev = jnp.where(row >= c, pltpu.roll(bend, c, 0), 0.0)
            qt = qs * jnp.exp2(b - bprev)
            kt = (kk * jnp.exp2(bend - b)).astype(BF16)
            pair = ((t_idx // c) % 2 == 1) & ((s_idx // c) == (t_idx // c) - 1)
            for h in range(HG_HEADS):
                sc = _dot_t(jnp.where(lane_h == h, qt, 0.0).astype(BF16), kt)
                att[h] = att[h] + jnp.where(pair, sc, 0.0)
            c *= 2
        att_cat = jnp.concatenate([a.astype(BF16) for a in att], axis=1)
        v_stack = jnp.concatenate([jnp.where(lane_h == h, v, 0.0).astype(BF16)
                                   for h in range(HG_HEADS)], axis=0)
        o = o + _dot(att_cat, v_stack)

        state = state_ref[...]
        o = o + _dot_t((qs * jnp.exp2(b)).astype(BF16), state.astype(BF16))
        b_last = b[L - 1:L, :]
        kdec = (kk * jnp.exp2(b_last - b)).astype(BF16)
        upd = lax.dot_general(v.astype(BF16), kdec, (((0,), (0,)), ((), ())),
                              preferred_element_type=F32)
        state_ref[...] = state * jnp.exp2(b_last) + jnp.where(same_head, upd, 0.0)

        ms = _dot((o * o).astype(BF16), ones_bd) * (1.0 / HG_DK)
        y = o * lax.rsqrt(ms + EPS) * nw_ref[...] * (g * _sigmoid(g))
        o_ref[pl.ds(r0, L), :] = y.astype(o_ref.dtype)
        return carry

    lax.fori_loop(0, HG_TILE // L, chunk, 0)


def _hgrn2(z, bsz, seq, lbp, nw, layer):
    nt = seq // HG_TILE
    n = HG_WIDTH
    assert seq % HG_TILE == 0 and HG_TILE % HG_CHUNK == 0
    return pl.pallas_call(
        functools.partial(_hgrn2_kernel, layer),
        grid=(bsz, nt),
        in_specs=[pl.BlockSpec((HG_TILE, 4 * n), lambda b, t: (b * nt + t, 0)),
                  pl.BlockSpec(lbp.shape, lambda b, t: (0, 0)),
                  pl.BlockSpec(nw.shape, lambda b, t: (0, 0))],
        out_specs=pl.BlockSpec((HG_TILE, n), lambda b, t: (b * nt + t, 0)),
        out_shape=jax.ShapeDtypeStruct((bsz * seq, n), BF16),
        scratch_shapes=[pltpu.VMEM((n, n), F32)],
        compiler_params=_params("parallel", "arbitrary"),
        name="hgrn2",
    )(z, lbp, nw)


def _moba_kernel(nblk, qt_ref, k_ref, vt_ref, km_ref, kx_ref, o_ref, qa_ref, sa_ref, sb_ref,
                 ta_ref, tb_ref, m_ref, acc_ref):
    j = pl.program_id(2)
    hp = pl.program_id(1)
    blk = MOBA_BLOCK
    hd = ATT_HD
    pair = 2 * hd
    span = MOBA_GROUP * blk
    scale = 1.0 / math.sqrt(hd)
    feat = lax.broadcasted_iota(jnp.int32, (pair, 1), 0)
    n_t = lax.broadcasted_iota(jnp.int32, (nblk, blk), 0)
    b_row = lax.broadcasted_iota(jnp.int32, (SUBLANES, blk), 0)
    causal = (lax.broadcasted_iota(jnp.int32, (blk, blk), 0)
              <= lax.broadcasted_iota(jnp.int32, (blk, blk), 1))
    qt = qt_ref[...]
    km = km_ref[...]

    def keys(start, size):
        return jnp.concatenate([k_ref[pl.ds(start, size), :], kx_ref[pl.ds(start, size), :]], axis=1)

    pad = jnp.zeros((pair - nblk - SUBLANES, blk), F32)
    qs, bias, gate = [], [], []
    for h in range(2):
        slope = jnp.exp2(-8.0 * (2 * hp + h + 1).astype(F32) / ATT_HEADS)
        rows = jnp.zeros((SUBLANES, blk), F32)
        for i, piece in enumerate(LOG2E_PIECES):
            rows = jnp.where(b_row == i, slope * piece, rows)
            rows = jnp.where(b_row == len(LOG2E_PIECES) + i, slope * (piece * blk), rows)
        bias.append(rows)
        qm = jnp.where(feat // hd == h, qt, 0.0)
        qs.append((qm * (scale * LOG2E)).astype(BF16))
        gate.append(jnp.dot(km, qm, precision=HIGHEST, preferred_element_type=F32))

    j0 = pl.multiple_of(j * blk, blk)
    k_own = keys(j0, blk)
    vt_own = vt_ref[:, pl.ds(j0, blk)]
    for h in range(2):
        own = jnp.concatenate([jnp.where(n_t == j, 0.0, NEG), bias[h], pad], axis=0)
        s = _dot(k_own, jnp.concatenate([qs[h], own.astype(BF16)], axis=0))
        s = jnp.where(causal, s, NEG)
        m = jnp.max(s, axis=0, keepdims=True)
        p = jnp.exp2((s - m).astype(BF16))
        m_ref[h] = m
        acc_ref[h] = _dot(vt_own[h * V_ROWS:(h + 1) * V_ROWS, :], p)

    def score(g, buf, heads=(0, 1)):
        s_ref, top_ref = buf
        k_g = keys(pl.multiple_of(g * span, span), span)
        for h in heads:
            s = _dot(k_g, qa_ref[h])
            s_ref[h] = s
            top_ref[h] = jnp.max(s, axis=0, keepdims=True)

    def attend(g, buf):
        s_ref, top_ref = buf
        vt_g = vt_ref[:, pl.ds(pl.multiple_of(g * span, span), span)]
        for h in range(2):
            m_old = m_ref[h]
            m_new = jnp.maximum(m_old, top_ref[h])
            p = jnp.exp2((s_ref[h] - m_new).astype(BF16))
            m_ref[h] = m_new
            pv = _dot(vt_g[h * V_ROWS:(h + 1) * V_ROWS, :], p)
            acc_ref[h] = jnp.exp2(m_old - m_new) * acc_ref[h] + pv

    last = nblk // MOBA_GROUP - 1
    buf_a = (sa_ref, ta_ref)
    buf_b = (sb_ref, tb_ref)

    for h in range(2):
        g = jnp.where(n_t < j, gate[h], NEG)
        chosen = jnp.zeros((nblk, blk), F32)
        for r in range(MOBA_TOPK):
            top = jnp.max(g, axis=0, keepdims=True)
            first = jnp.min(jnp.where(g == top, n_t, nblk), axis=0, keepdims=True)
            pick = n_t == first
            chosen = jnp.where(pick, jnp.where(j > r, 1.0, 0.0), chosen)
            g = jnp.where(pick, -jnp.inf, g)
        past = jnp.concatenate([jnp.where(chosen > 0.0, 0.0, NEG), bias[h], pad], axis=0)
        qa_ref[h] = jnp.concatenate([qs[h], past.astype(BF16)], axis=0)
        score(0, buf_a, heads=(h,))

    def chain(g0, count, more):
        for t in range(count):
            cur, nxt = (buf_a, buf_b) if t % 2 == 0 else (buf_b, buf_a)
            if t + 1 < count or more:
                score(jnp.minimum(g0 + t + 1, last), nxt)
            attend(g0 + t, cur)

    groups = (j + MOBA_GROUP - 1) // MOBA_GROUP
    nlong = groups // MOBA_LONG
    left = groups % MOBA_LONG

    def body(i, carry):
        chain(i * MOBA_LONG, MOBA_LONG, True)
        return carry

    lax.fori_loop(0, nlong, body, 0)
    g0 = nlong * MOBA_LONG
    size = MOBA_LONG // 2
    while size >= 1:
        take = (left & size) != 0

        @pl.when(take)
        def _(g0=g0, size=size):
            chain(g0, size, size > 1)

        g0 = g0 + jnp.where(take, size, 0)
        size //= 2

    out = []
    for h in range(2):
        acc = acc_ref[h]
        out.append(acc[0:hd] / acc[hd:hd + 1])
    o_ref[...] = jnp.concatenate(out, axis=0).T


def _moba(qt, k, vt, km, bsz, seq):
    nblk = seq // MOBA_BLOCK
    blk = MOBA_BLOCK
    pair = 2 * ATT_HD
    assert nblk + SUBLANES <= pair and 2 * len(LOG2E_PIECES) <= SUBLANES
    assert MOBA_LONG & (MOBA_LONG - 1) == 0 and MOBA_LONG >= 2
    assert nblk % MOBA_GROUP == 0
    pos = jnp.arange(seq, dtype=jnp.int32)[:, None]
    col = jnp.arange(pair, dtype=jnp.int32)[None, :]
    extras = [pos % blk] * len(LOG2E_PIECES) + [pos // blk] * len(LOG2E_PIECES)
    kx = (pos // blk == col).astype(F32)
    for i, e in enumerate(extras):
        kx = jnp.where(col == nblk + i, e.astype(F32), kx)
    kx = kx.astype(BF16)
    return pl.pallas_call(
        functools.partial(_moba_kernel, nblk),
        grid=(bsz, ATT_HEADS // 2, nblk),
        in_specs=[pl.BlockSpec((pair, blk), lambda b, p, j: (p, b * nblk + j)),
                  pl.BlockSpec((seq, pair), lambda b, p, j: (b, p)),
                  pl.BlockSpec((2 * V_ROWS, seq), lambda b, p, j: (p, b)),
                  pl.BlockSpec((nblk, pair), lambda b, p, j: (b, p)),
                  pl.BlockSpec((seq, pair), lambda b, p, j: (0, 0))],
        out_specs=pl.BlockSpec((blk, pair), lambda b, p, j: (b * nblk + j, p)),
        out_shape=jax.ShapeDtypeStruct((bsz * seq, ATT_WIDTH), F32),
        scratch_shapes=[pltpu.VMEM((2, 2 * pair, blk), BF16),
                        pltpu.VMEM((2, MOBA_GROUP * blk, blk), F32),
                        pltpu.VMEM((2, MOBA_GROUP * blk, blk), F32),
                        pltpu.VMEM((2, 1, blk), F32),
                        pltpu.VMEM((2, 1, blk), F32),
                        pltpu.VMEM((2, 1, blk), F32),
                        pltpu.VMEM((2, V_ROWS, blk), F32)],
        compiler_params=_params("parallel", "parallel", "arbitrary"),
        name="moba",
    )(qt, k, vt, km, kx)


def _out_mlp_kernel(final, x_ref, ya_ref, yb_ref, yc_ref, an_ref, wo_ref, gm_ref, w1_ref, w2_ref,
                    gf_ref, o_ref):
    yc = _rms(yc_ref[...], an_ref[...]).astype(BF16)
    y = jnp.concatenate([ya_ref[...], yb_ref[...], yc], axis=1)
    x = x_ref[...] + _dot(y, wo_ref[...])
    hb = _rms(x, gm_ref[...]).astype(BF16)
    mlp = None
    for c in range(w1_ref.shape[1] // FF_CHUNK):
        cols = slice(c * FF_CHUNK, (c + 1) * FF_CHUNK)
        u = jnp.square(jnp.maximum(_dot(hb, w1_ref[:, cols]), 0.0))
        part = _dot(u.astype(BF16), w2_ref[cols, :])
        mlp = part if mlp is None else mlp + part
    x = x + mlp
    if final:
        x = _rms(x, gf_ref[...])
    o_ref[...] = x


def _out_mlp(x2, ya, yb, yc, an, wo, gm, w1, w2, gf, layer, final):
    t, d = x2.shape
    stacked = lambda a: pl.BlockSpec((None,) + a.shape[1:], lambda i: (layer, 0, 0))
    row = lambda a: pl.BlockSpec((ROW_TILE, a.shape[1]), lambda i: (i, 0))
    full = lambda a: pl.BlockSpec(a.shape, lambda i: (0, 0))
    assert t % ROW_TILE == 0 and w1.shape[2] % FF_CHUNK == 0
    return pl.pallas_call(
        functools.partial(_out_mlp_kernel, final),
        grid=(t // ROW_TILE,),
        in_specs=[row(x2), row(ya), row(yb), row(yc), full(an), stacked(wo), full(gm), stacked(w1),
                  stacked(w2), full(gf)],
        out_specs=pl.BlockSpec((ROW_TILE, d), lambda i: (i, 0)),
        out_shape=jax.ShapeDtypeStruct((t, d), F32),
        compiler_params=_params("parallel"),
        name="out_mlp",
    )(x2, ya, yb, yc, an, wo, gm, w1, w2, gf)


def _block_diag(w):
    g, n, _ = w.shape
    eye = jnp.eye(g, dtype=w.dtype)
    return (eye[:, None, :, None] * w[:, :, None, :]).reshape(g * n, g * n)


def kernel(x, w_in, w_out, norm_mix, norm_mlp, lru_conv_w, lru_conv_b, lru_wa, lru_ba, lru_wx, lru_bx,
           lru_lambda, hg_lower_bounds, hg_norm_w, lru_out_norm, att_out_norm, w_ff1, w_ff2, norm_final):
    bsz, seq, d = x.shape
    depth = w_in.shape[0]
    x2 = x.reshape(bsz * seq, d)
    r2 = lambda a: a.reshape(1, -1)
    w_in_b = w_in.astype(BF16)
    w_qv = jnp.concatenate([w_in[:, :, COLS_Q[0]:COLS_Q[1]], w_in[:, :, COLS_V[0]:COLS_V[1]]], axis=2)
    wt_b = jnp.swapaxes(lax.optimization_barrier(w_qv), 1, 2).astype(BF16)
    w_out_b, w_ff1_b, w_ff2_b = w_out.astype(BF16), w_ff1.astype(BF16), w_ff2.astype(BF16)
    for l in range(depth):
        wg = jnp.concatenate([_block_diag(lru_wa[l]), _block_diag(lru_wx[l])], axis=1).astype(BF16)
        bg = jnp.concatenate([lru_ba[l], lru_bx[l]]).reshape(1, -1)
        lru_params = (lru_conv_w[l], r2(lru_conv_b[l]), wg, bg, r2(lru_lambda[l]), r2(lru_out_norm[l]))
        ya, hg, cqt, ck, cvt, km = _in_proj(x2, r2(norm_mix[l]), w_in_b, wt_b, l, seq, lru_params)
        yb = _hgrn2(hg, bsz, seq, hg_lower_bounds, r2(jnp.tile(hg_norm_w[l], HG_HEADS)), l)
        yc = _moba(cqt, ck, cvt, km.reshape(-1, km.shape[-1]), bsz, seq)
        x2 = _out_mlp(x2, ya, yb, yc, r2(att_out_norm[l]), w_out_b, r2(norm_mlp[l]), w_ff1_b, w_ff2_b,
                      r2(norm_final), l, l == depth - 1)
    return x2.reshape(bsz, seq, d)
```

```python
import functools
import math
import struct

import jax
import jax.numpy as jnp
from jax import lax
from jax.experimental import pallas as pl
from jax.experimental.pallas import tpu as pltpu

F32 = jnp.float32
BF16 = jnp.bfloat16
HIGHEST = lax.Precision.HIGHEST

EPS = 1e-6
NEG = -1e30
TINY = 1e-30

LRU_WIDTH = 256
LRU_CONV = 4
LRU_C = 8.0
HG_HEADS = 4
HG_DK = 64
ATT_HEADS = 8
ATT_HD = 64
MOBA_BLOCK = 256
MOBA_TOPK = 3

HG_WIDTH = HG_HEADS * HG_DK
ATT_WIDTH = ATT_HEADS * ATT_HD
COLS_LRU = (0, 2 * LRU_WIDTH)
COLS_HG = (COLS_LRU[1], COLS_LRU[1] + 4 * HG_WIDTH)
COLS_Q = (COLS_HG[1], COLS_HG[1] + ATT_WIDTH)
COLS_K = (COLS_Q[1], COLS_Q[1] + ATT_WIDTH)
COLS_V = (COLS_K[1], COLS_K[1] + ATT_WIDTH)

SUBLANES = 8
VMEM_LIMIT = 56 * 1024 * 1024

ROW_TILE = 512
LRU_SUB = 256
HG_TILE = 512
HG_CHUNK = 256
FF_CHUNK = 1024
MOBA_GROUP = 2
MOBA_LONG = 8
ONES_ROWS = 16
V_ROWS = ATT_HD + ONES_ROWS


def _bf16_pieces(x, n):
    pieces = []
    for _ in range(n):
        bits = struct.unpack("<I", struct.pack("<f", x))[0]
        bits = (bits + 0x7FFF + ((bits >> 16) & 1)) & 0xFFFF0000
        piece = struct.unpack("<f", struct.pack("<I", bits))[0]
        pieces.append(piece)
        x -= piece
    return tuple(pieces)


LOG2E = math.log2(math.e)
LOG2E_PIECES = _bf16_pieces(LOG2E, 3)


def _dot(a, b):
    return jnp.dot(a, b, preferred_element_type=F32)


def _dot_t(a, b):
    return lax.dot_general(a, b, (((1,), (1,)), ((), ())), preferred_element_type=F32)


def _rms(x, g):
    return x * lax.rsqrt(jnp.mean(x * x, axis=-1, keepdims=True) + EPS) * g


def _sigmoid(x):
    return 1.0 / (1.0 + jnp.exp(-x))


def _softplus(x):
    return jnp.maximum(x, 0.0) + jnp.log(1.0 + jnp.exp(-jnp.abs(x)))


def _params(*sem):
    return pltpu.CompilerParams(dimension_semantics=sem, vmem_limit_bytes=VMEM_LIMIT)


def _in_proj_kernel(seq, x_ref, g_ref, w_ref, wt_ref, cw_ref, cb_ref, wg_ref, bg_ref, lam_ref, nw_ref,
                    ya_ref, hg_ref, qt_ref, k_ref, vt_ref, km_ref, prev_ref, h_ref):
    @pl.when((pl.program_id(0) * ROW_TILE) % seq == 0)
    def _():
        prev_ref[...] = jnp.zeros_like(prev_ref)
        h_ref[...] = jnp.zeros_like(h_ref)

    hb = _rms(x_ref[...], g_ref[...]).astype(BF16)
    _rglru_tile(_dot(hb, w_ref[:, COLS_LRU[0]:COLS_LRU[1]]), cw_ref, cb_ref, wg_ref, bg_ref, lam_ref,
                nw_ref, ya_ref, prev_ref, h_ref)
    hg_ref[...] = _dot(hb, w_ref[:, COLS_HG[0]:COLS_HG[1]])
    qt_ref[...] = _dot_t(wt_ref[0:ATT_WIDTH, :], hb)
    k = _dot(hb, w_ref[:, COLS_K[0]:COLS_K[1]])
    k_ref[...] = k.astype(BF16)
    for i in range(ROW_TILE // MOBA_BLOCK):
        km_ref[i] = jnp.mean(k[i * MOBA_BLOCK:(i + 1) * MOBA_BLOCK], axis=0, keepdims=True)
    vt = _dot_t(wt_ref[ATT_WIDTH:2 * ATT_WIDTH, :], hb).astype(BF16)
    ones = jnp.ones((ONES_ROWS, ROW_TILE), BF16)
    for h in range(ATT_HEADS):
        vt_ref[h * V_ROWS:h * V_ROWS + ATT_HD, :] = vt[h * ATT_HD:(h + 1) * ATT_HD, :]
        vt_ref[h * V_ROWS + ATT_HD:(h + 1) * V_ROWS, :] = ones


def _in_proj(x2, g, w, wt, layer, seq, lru_params):
    t, d = x2.shape
    nblk = t // MOBA_BLOCK
    stacked = lambda a: pl.BlockSpec((None,) + a.shape[1:], lambda i: (layer, 0, 0))
    vec = lambda a: pl.BlockSpec(a.shape, lambda i: (0, 0))
    row = lambda width: pl.BlockSpec((ROW_TILE, width), lambda i: (i, 0))
    col = lambda height: pl.BlockSpec((height, ROW_TILE), lambda i: (0, i))
    assert seq % ROW_TILE == 0 and ROW_TILE % LRU_SUB == 0 and ROW_TILE % MOBA_BLOCK == 0
    return pl.pallas_call(
        functools.partial(_in_proj_kernel, seq),
        grid=(t // ROW_TILE,),
        in_specs=[row(d), vec(g), stacked(w), stacked(wt)] + [vec(a) for a in lru_params],
        out_specs=[row(LRU_WIDTH), row(4 * HG_WIDTH), col(ATT_WIDTH), row(ATT_WIDTH),
                   col(ATT_HEADS * V_ROWS),
                   pl.BlockSpec((ROW_TILE // MOBA_BLOCK, 1, ATT_WIDTH), lambda i: (i, 0, 0))],
        out_shape=[jax.ShapeDtypeStruct((t, LRU_WIDTH), BF16),
                   jax.ShapeDtypeStruct((t, 4 * HG_WIDTH), F32),
                   jax.ShapeDtypeStruct((ATT_WIDTH, t), F32),
                   jax.ShapeDtypeStruct((t, ATT_WIDTH), BF16),
                   jax.ShapeDtypeStruct((ATT_HEADS * V_ROWS, t), BF16),
                   jax.ShapeDtypeStruct((nblk, 1, ATT_WIDTH), F32)],
        scratch_shapes=[pltpu.VMEM((LRU_SUB, LRU_WIDTH), F32), pltpu.VMEM((1, LRU_WIDTH), F32)],
        compiler_params=_params("arbitrary"),
        name="in_proj",
    )(x2, g, w, wt, *lru_params)


def _rglru_tile(xy, cw_ref, cb_ref, wg_ref, bg_ref, lam_ref, nw_ref, o_ref, prev_ref, h_ref):
    w = LRU_WIDTH
    row = lax.broadcasted_iota(jnp.int32, (LRU_SUB, w), 0)
    sp = _softplus(-lam_ref[...])
    for c in range(ROW_TILE // LRU_SUB):
        rows = slice(c * LRU_SUB, (c + 1) * LRU_SUB)
        xb = xy[rows, 0:w]
        yb = xy[rows, w:2 * w]
        prev = prev_ref[...]
        xc = xb * cw_ref[LRU_CONV - 1:LRU_CONV, :] + cb_ref[...]
        for k in range(1, LRU_CONV):
            shifted = jnp.where(row >= k, pltpu.roll(xb, k, 0), pltpu.roll(prev, k, 0))
            xc = xc + shifted * cw_ref[LRU_CONV - 1 - k:LRU_CONV - k, :]
        prev_ref[...] = xb
        gates = _dot(xc.astype(BF16), wg_ref[...]) + bg_ref[...]
        r = _sigmoid(gates[:, 0:w])
        i = _sigmoid(gates[:, w:2 * w])
        log_a = -LRU_C * r * sp
        a = jnp.exp(log_a)
        b = jnp.sqrt(jnp.maximum(1.0 - a * a, 0.0)) * (i * xc)
        s = 1
        while s < LRU_SUB:
            keep = row >= s
            a_s = jnp.where(keep, pltpu.roll(a, s, 0), 1.0)
            b_s = jnp.where(keep, pltpu.roll(b, s, 0), 0.0)
            b = a * b_s + b
            a = a * a_s
            s *= 2
        h = b + a * h_ref[...]
        h_ref[...] = h[LRU_SUB - 1:LRU_SUB, :]
        y = h * jax.nn.gelu(yb, approximate=True)
        o_ref[rows, :] = _rms(y, nw_ref[...]).astype(o_ref.dtype)


def _hgrn2_kernel(layer, z_ref, lbp_ref, nw_ref, o_ref, state_ref):
    @pl.when(pl.program_id(1) == 0)
    def _():
        state_ref[...] = jnp.zeros_like(state_ref)

    n = HG_WIDTH
    L = HG_CHUNK
    lbp = lbp_ref[...]
    e = jnp.exp(lbp - jnp.max(lbp, axis=0, keepdims=True))
    soft = e / jnp.sum(e, axis=0, keepdims=True)
    lb = jnp.sum(soft[0:layer + 1], axis=0, keepdims=True) - soft[0:1]
    log_lb = jnp.log(jnp.maximum(lb, TINY))
    log_1m = jnp.log1p(-lb)

    lane_h = lax.broadcasted_iota(jnp.int32, (1, n), 1) // HG_DK
    head_of_row = lax.broadcasted_iota(jnp.int32, (n, n), 0) // HG_DK
    head_of_col = lax.broadcasted_iota(jnp.int32, (n, n), 1) // HG_DK
    same_head = head_of_row == head_of_col
    ones_bd = jnp.where(same_head, 1.0, 0.0).astype(BF16)
    row = lax.broadcasted_iota(jnp.int32, (L, n), 0)
    sub = lax.broadcasted_iota(jnp.int32, (1, SUBLANES, 1), 1)
    t_idx = lax.broadcasted_iota(jnp.int32, (L, L), 0)
    s_idx = lax.broadcasted_iota(jnp.int32, (L, L), 1)

    def chunk(ci, carry):
        r0 = pl.multiple_of(ci * L, L)
        q = z_ref[pl.ds(r0, L), 0:n]
        f = z_ref[pl.ds(r0, L), n:2 * n]
        v = z_ref[pl.ds(r0, L), 2 * n:3 * n]
        g = z_ref[pl.ds(r0, L), 3 * n:4 * n]
        lsig = -_softplus(-f)
        t1 = log_1m + lsig
        log_f = jnp.maximum(log_lb, t1) + jnp.log(1.0 + jnp.exp(-jnp.abs(log_lb - t1)))
        kk = (1.0 - lb) * _sigmoid(-f)
        qs = q * _sigmoid(q)
        b = log_f * LOG2E
        s = 1
        while s < L:
            b = b + jnp.where(row >= s, pltpu.roll(b, s, 0), 0.0)
            s *= 2

        g8 = L // SUBLANES
        q3 = qs.reshape(g8, SUBLANES, n)
        k3 = kk.reshape(g8, SUBLANES, n)
        b3 = b.reshape(g8, SUBLANES, n)
        v3 = v.reshape(g8, SUBLANES, n)
        o = jnp.zeros((L, n), F32)
        for j in range(SUBLANES):
            d = jnp.where(sub >= j, b3 - b3[:, j:j + 1, :], NEG)
            xj = q3 * k3[:, j:j + 1, :] * jnp.exp2(d)
            aj = _dot(xj.reshape(L, n).astype(BF16), ones_bd)
            o = o + (aj.reshape(g8, SUBLANES, n) * v3[:, j:j + 1, :]).reshape(L, n)

        att = [jnp.zeros((L, L), F32) for _ in range(HG_HEADS)]
        c = SUBLANES
        while c < L:
            bc = b.reshape(L // c, c, n)
            bend = jnp.broadcast_to(bc[:, c - 1:c, :], (L // c, c, n)).reshape(L, n)
            bprev = jnp.where(row >= c, pltpu.roll(bend, c, 0), 0.0)
            qt = qs * jnp.exp2(b - bprev)
            kt = (kk * jnp.exp2(bend - b)).astype(BF16)
            pair = ((t_idx // c) % 2 == 1) & ((s_idx // c) == (t_idx // c) - 1)
            for h in range(HG_HEADS):
                sc = _dot_t(jnp.where(lane_h == h, qt, 0.0).astype(BF16), kt)
                att[h] = att[h] + jnp.where(pair, sc, 0.0)
            c *= 2
        att_cat = jnp.concatenate([a.astype(BF16) for a in att], axis=1)
        v_stack = jnp.concatenate([jnp.where(lane_h == h, v, 0.0).astype(BF16)
                                   for h in range(HG_HEADS)], axis=0)
        o = o + _dot(att_cat, v_stack)

        state = state_ref[...]
        o = o + _dot_t((qs * jnp.exp2(b)).astype(BF16), state.astype(BF16))
        b_last = b[L - 1:L, :]
        kdec = (kk * jnp.exp2(b_last - b)).astype(BF16)
        upd = lax.dot_general(v.astype(BF16), kdec, (((0,), (0,)), ((), ())),
                              preferred_element_type=F32)
        state_ref[...] = state * jnp.exp2(b_last) + jnp.where(same_head, upd, 0.0)

        ms = _dot((o * o).astype(BF16), ones_bd) * (1.0 / HG_DK)
        y = o * lax.rsqrt(ms + EPS) * nw_ref[...] * (g * _sigmoid(g))
        o_ref[pl.ds(r0, L), :] = y.astype(o_ref.dtype)
        return carry

    lax.fori_loop(0, HG_TILE // L, chunk, 0)


def _hgrn2(z, bsz, seq, lbp, nw, layer):
    nt = seq // HG_TILE
    n = HG_WIDTH
    assert seq % HG_TILE == 0 and HG_TILE % HG_CHUNK == 0
    return pl.pallas_call(
        functools.partial(_hgrn2_kernel, layer),
        grid=(bsz, nt),
        in_specs=[pl.BlockSpec((HG_TILE, 4 * n), lambda b, t: (b * nt + t, 0)),
                  pl.BlockSpec(lbp.shape, lambda b, t: (0, 0)),
                  pl.BlockSpec(nw.shape, lambda b, t: (0, 0))],
        out_specs=pl.BlockSpec((HG_TILE, n), lambda b, t: (b * nt + t, 0)),
        out_shape=jax.ShapeDtypeStruct((bsz * seq, n), BF16),
        scratch_shapes=[pltpu.VMEM((n, n), F32)],
        compiler_params=_params("parallel", "arbitrary"),
        name="hgrn2",
    )(z, lbp, nw)


def _moba_kernel(nblk, qt_ref, k_ref, vt_ref, km_ref, kx_ref, o_ref, qa_ref, sa_ref, sb_ref,
                 ta_ref, tb_ref, m_ref, acc_ref):
    j = pl.program_id(2)
    hp = pl.program_id(1)
    blk = MOBA_BLOCK
    hd = ATT_HD
    pair = 2 * hd
    span = MOBA_GROUP * blk
    scale = 1.0 / math.sqrt(hd)
    feat = lax.broadcasted_iota(jnp.int32, (pair, 1), 0)
    n_t = lax.broadcasted_iota(jnp.int32, (nblk, blk), 0)
    b_row = lax.broadcasted_iota(jnp.int32, (SUBLANES, blk), 0)
    causal = (lax.broadcasted_iota(jnp.int32, (blk, blk), 0)
              <= lax.broadcasted_iota(jnp.int32, (blk, blk), 1))
    qt = qt_ref[...]
    km = km_ref[...]

    def keys(start, size):
        return jnp.concatenate([k_ref[pl.ds(start, size), :], kx_ref[pl.ds(start, size), :]], axis=1)

    pad = jnp.zeros((pair - nblk - SUBLANES, blk), F32)
    qs, bias, gate = [], [], []
    for h in range(2):
        slope = jnp.exp2(-8.0 * (2 * hp + h + 1).astype(F32) / ATT_HEADS)
        rows = jnp.zeros((SUBLANES, blk), F32)
        for i, piece in enumerate(LOG2E_PIECES):
            rows = jnp.where(b_row == i, slope * piece, rows)
            rows = jnp.where(b_row == len(LOG2E_PIECES) + i, slope * (piece * blk), rows)
        bias.append(rows)
        qm = jnp.where(feat // hd == h, qt, 0.0)
        qs.append((qm * (scale * LOG2E)).astype(BF16))
        gate.append(jnp.dot(km, qm, precision=HIGHEST, preferred_element_type=F32))

    j0 = pl.multiple_of(j * blk, blk)
    k_own = keys(j0, blk)
    for h in range(2):
        own = jnp.concatenate([jnp.where(n_t == j, 0.0, NEG), bias[h], pad], axis=0)
        s = _dot(k_own, jnp.concatenate([qs[h], own.astype(BF16)], axis=0))
        sb_ref[h, 0:blk, :] = jnp.where(causal, s, NEG)

    def score(g, buf, heads=(0, 1)):
        s_ref, top_ref = buf
        k_g = keys(pl.multiple_of(g * span, span), span)
        for h in heads:
            s = _dot(k_g, qa_ref[h])
            s_ref[h] = s
            top_ref[h] = jnp.max(s, axis=0, keepdims=True)

    def attend(g, buf):
        s_ref, top_ref = buf
        vt_g = vt_ref[:, pl.ds(pl.multiple_of(g * span, span), span)]
        for h in range(2):
            m_old = m_ref[h]
            m_new = jnp.maximum(m_old, top_ref[h])
            p = jnp.exp2((s_ref[h] - m_new).astype(BF16))
            m_ref[h] = m_new
            pv = _dot(vt_g[h * V_ROWS:(h + 1) * V_ROWS, :], p)
            acc_ref[h] = jnp.exp2(m_old - m_new) * acc_ref[h] + pv

    last = nblk // MOBA_GROUP - 1
    buf_a = (sa_ref, ta_ref)
    buf_b = (sb_ref, tb_ref)

    for h in range(2):
        g = jnp.where(n_t < j, gate[h], NEG)
        chosen = jnp.zeros((nblk, blk), F32)
        for r in range(MOBA_TOPK):
            top = jnp.max(g, axis=0, keepdims=True)
            first = jnp.min(jnp.where(g == top, n_t, nblk), axis=0, keepdims=True)
            pick = n_t == first
            chosen = jnp.where(pick, jnp.where(j > r, 1.0, 0.0), chosen)
            g = jnp.where(pick, -jnp.inf, g)
        past = jnp.concatenate([jnp.where(chosen > 0.0, 0.0, NEG), bias[h], pad], axis=0)
        qa_ref[h] = jnp.concatenate([qs[h], past.astype(BF16)], axis=0)
        score(0, buf_a, heads=(h,))

    vt_own = vt_ref[:, pl.ds(j0, blk)]
    for h in range(2):
        s = sb_ref[h, 0:blk, :]
        m = jnp.max(s, axis=0, keepdims=True)
        p = jnp.exp2((s - m).astype(BF16))
        m_ref[h] = m
        acc_ref[h] = _dot(vt_own[h * V_ROWS:(h + 1) * V_ROWS, :], p)

    def chain(g0, count, more):
        for t in range(count):
            cur, nxt = (buf_a, buf_b) if t % 2 == 0 else (buf_b, buf_a)
            if t + 1 < count or more:
                score(jnp.minimum(g0 + t + 1, last), nxt)
            attend(g0 + t, cur)

    groups = (j + MOBA_GROUP - 1) // MOBA_GROUP
    nlong = groups // MOBA_LONG
    left = groups % MOBA_LONG

    def body(i, carry):
        chain(i * MOBA_LONG, MOBA_LONG, True)
        return carry

    lax.fori_loop(0, nlong, body, 0)
    g0 = nlong * MOBA_LONG
    size = MOBA_LONG // 2
    while size >= 1:
        take = (left & size) != 0

        @pl.when(take)
        def _(g0=g0, size=size):
            chain(g0, size, size > 1)

        g0 = g0 + jnp.where(take, size, 0)
        size //= 2

    out = []
    for h in range(2):
        acc = acc_ref[h]
        out.append(acc[0:hd] / acc[hd:hd + 1])
    o_ref[...] = jnp.concatenate(out, axis=0).T


def _moba(qt, k, vt, km, bsz, seq):
    nblk = seq // MOBA_BLOCK
    blk = MOBA_BLOCK
    pair = 2 * ATT_HD
    assert nblk + SUBLANES <= pair and 2 * len(LOG2E_PIECES) <= SUBLANES
    assert MOBA_LONG & (MOBA_LONG - 1) == 0 and MOBA_LONG >= 2
    assert nblk % MOBA_GROUP == 0
    pos = jnp.arange(seq, dtype=jnp.int32)[:, None]
    col = jnp.arange(pair, dtype=jnp.int32)[None, :]
    extras = [pos % blk] * len(LOG2E_PIECES) + [pos // blk] * len(LOG2E_PIECES)
    kx = (pos // blk == col).astype(F32)
    for i, e in enumerate(extras):
        kx = jnp.where(col == nblk + i, e.astype(F32), kx)
    kx = kx.astype(BF16)
    return pl.pallas_call(
        functools.partial(_moba_kernel, nblk),
        grid=(bsz, ATT_HEADS // 2, nblk),
        in_specs=[pl.BlockSpec((pair, blk), lambda b, p, j: (p, b * nblk + j)),
                  pl.BlockSpec((seq, pair), lambda b, p, j: (b, p)),
                  pl.BlockSpec((2 * V_ROWS, seq), lambda b, p, j: (p, b)),
                  pl.BlockSpec((nblk, pair), lambda b, p, j: (b, p)),
                  pl.BlockSpec((seq, pair), lambda b, p, j: (0, 0))],
        out_specs=pl.BlockSpec((blk, pair), lambda b, p, j: (b * nblk + j, p)),
        out_shape=jax.ShapeDtypeStruct((bsz * seq, ATT_WIDTH), F32),
        scratch_shapes=[pltpu.VMEM((2, 2 * pair, blk), BF16),
                        pltpu.VMEM((2, MOBA_GROUP * blk, blk), F32),
                        pltpu.VMEM((2, MOBA_GROUP * blk, blk), F32),
                        pltpu.VMEM((2, 1, blk), F32),
                        pltpu.VMEM((2, 1, blk), F32),
                        pltpu.VMEM((2, 1, blk), F32),
                        pltpu.VMEM((2, V_ROWS, blk), F32)],
        compiler_params=_params("parallel", "parallel", "arbitrary"),
        name="moba",
    )(qt, k, vt, km, kx)


def _out_mlp_kernel(final, x_ref, ya_ref, yb_ref, yc_ref, an_ref, wo_ref, gm_ref, w1_ref, w2_ref,
                    gf_ref, o_ref):
    yc = _rms(yc_ref[...], an_ref[...]).astype(BF16)
    y = jnp.concatenate([ya_ref[...], yb_ref[...], yc], axis=1)
    x = x_ref[...] + _dot(y, wo_ref[...])
    hb = _rms(x, gm_ref[...]).astype(BF16)
    mlp = None
    for c in range(w1_ref.shape[1] // FF_CHUNK):
        cols = slice(c * FF_CHUNK, (c + 1) * FF_CHUNK)
        u = jnp.square(jnp.maximum(_dot(hb, w1_ref[:, cols]), 0.0))
        part = _dot(u.astype(BF16), w2_ref[cols, :])
        mlp = part if mlp is None else mlp + part
    x = x + mlp
    if final:
        x = _rms(x, gf_ref[...])
    o_ref[...] = x


def _out_mlp(x2, ya, yb, yc, an, wo, gm, w1, w2, gf, layer, final):
    t, d = x2.shape
    stacked = lambda a: pl.BlockSpec((None,) + a.shape[1:], lambda i: (layer, 0, 0))
    row = lambda a: pl.BlockSpec((ROW_TILE, a.shape[1]), lambda i: (i, 0))
    full = lambda a: pl.BlockSpec(a.shape, lambda i: (0, 0))
    assert t % ROW_TILE == 0 and w1.shape[2] % FF_CHUNK == 0
    return pl.pallas_call(
        functools.partial(_out_mlp_kernel, final),
        grid=(t // ROW_TILE,),
        in_specs=[row(x2), row(ya), row(yb), row(yc), full(an), stacked(wo), full(gm), stacked(w1),
                  stacked(w2), full(gf)],
        out_specs=pl.BlockSpec((ROW_TILE, d), lambda i: (i, 0)),
        out_shape=jax.ShapeDtypeStruct((t, d), F32),
        compiler_params=_params("parallel"),
        name="out_mlp",
    )(x2, ya, yb, yc, an, wo, gm, w1, w2, gf)


def _block_diag(w):
    g, n, _ = w.shape
    eye = jnp.eye(g, dtype=w.dtype)
    return (eye[:, None, :, None] * w[:, :, None, :]).reshape(g * n, g * n)


def kernel(x, w_in, w_out, norm_mix, norm_mlp, lru_conv_w, lru_conv_b, lru_wa, lru_ba, lru_wx, lru_bx,
           lru_lambda, hg_lower_bounds, hg_norm_w, lru_out_norm, att_out_norm, w_ff1, w_ff2, norm_final):
    bsz, seq, d = x.shape
    depth = w_in.shape[0]
    x2 = x.reshape(bsz * seq, d)
    r2 = lambda a: a.reshape(1, -1)
    w_in_b = w_in.astype(BF16)
    w_qv = jnp.concatenate([w_in[:, :, COLS_Q[0]:COLS_Q[1]], w_in[:, :, COLS_V[0]:COLS_V[1]]], axis=2)
    wt_b = jnp.swapaxes(lax.optimization_barrier(w_qv), 1, 2).astype(BF16)
    w_out_b, w_ff1_b, w_ff2_b = w_out.astype(BF16), w_ff1.astype(BF16), w_ff2.astype(BF16)
    for l in range(depth):
        wg = jnp.concatenate([_block_diag(lru_wa[l]), _block_diag(lru_wx[l])], axis=1).astype(BF16)
        bg = jnp.concatenate([lru_ba[l], lru_bx[l]]).reshape(1, -1)
        lru_params = (lru_conv_w[l], r2(lru_conv_b[l]), wg, bg, r2(lru_lambda[l]), r2(lru_out_norm[l]))
        ya, hg, cqt, ck, cvt, km = _in_proj(x2, r2(norm_mix[l]), w_in_b, wt_b, l, seq, lru_params)
        yb = _hgrn2(hg, bsz, seq, hg_lower_bounds, r2(jnp.tile(hg_norm_w[l], HG_HEADS)), l)
        yc = _moba(cqt, ck, cvt, km.reshape(-1, km.shape[-1]), bsz, seq)
        x2 = _out_mlp(x2, ya, yb, yc, r2(att_out_norm[l]), w_out_b, r2(norm_mlp[l]), w_ff1_b, w_ff2_b,
                      r2(norm_final), l, l == depth - 1)
    return x2.reshape(bsz, seq, d)
```

```python
import functools
import math
import struct

import jax
import jax.numpy as jnp
from jax import lax
from jax.experimental import pallas as pl
from jax.experimental.pallas import tpu as pltpu

F32 = jnp.float32
BF16 = jnp.bfloat16
HIGHEST = lax.Precision.HIGHEST

EPS = 1e-6
NEG = -1e30
TINY = 1e-30

LRU_WIDTH = 256
LRU_CONV = 4
LRU_C = 8.0
HG_HEADS = 4
HG_DK = 64
ATT_HEADS = 8
ATT_HD = 64
MOBA_BLOCK = 256
MOBA_TOPK = 3

HG_WIDTH = HG_HEADS * HG_DK
ATT_WIDTH = ATT_HEADS * ATT_HD
COLS_LRU = (0, 2 * LRU_WIDTH)
COLS_HG = (COLS_LRU[1], COLS_LRU[1] + 4 * HG_WIDTH)
COLS_Q = (COLS_HG[1], COLS_HG[1] + ATT_WIDTH)
COLS_K = (COLS_Q[1], COLS_Q[1] + ATT_WIDTH)
COLS_V = (COLS_K[1], COLS_K[1] + ATT_WIDTH)

SUBLANES = 8
VMEM_LIMIT = 56 * 1024 * 1024

ROW_TILE = 512
LRU_SUB = 256
HG_TILE = 512
HG_CHUNK = 256
FF_CHUNK = 1024
MOBA_GROUP = 2
MOBA_LONG = 8
ONES_ROWS = 16
V_ROWS = ATT_HD + ONES_ROWS


def _bf16_pieces(x, n):
    pieces = []
    for _ in range(n):
        bits = struct.unpack("<I", struct.pack("<f", x))[0]
        bits = (bits + 0x7FFF + ((bits >> 16) & 1)) & 0xFFFF0000
        piece = struct.unpack("<f", struct.pack("<I", bits))[0]
        pieces.append(piece)
        x -= piece
    return tuple(pieces)


LOG2E = math.log2(math.e)
LOG2E_PIECES = _bf16_pieces(LOG2E, 3)


def _dot(a, b):
    return jnp.dot(a, b, preferred_element_type=F32)


def _dot_t(a, b):
    return lax.dot_general(a, b, (((1,), (1,)), ((), ())), preferred_element_type=F32)


def _rms(x, g):
    return x * lax.rsqrt(jnp.mean(x * x, axis=-1, keepdims=True) + EPS) * g


def _sigmoid(x):
    return 1.0 / (1.0 + jnp.exp(-x))


def _softplus(x):
    return jnp.maximum(x, 0.0) + jnp.log(1.0 + jnp.exp(-jnp.abs(x)))


def _params(*sem):
    return pltpu.CompilerParams(dimension_semantics=sem, vmem_limit_bytes=VMEM_LIMIT)


def _in_proj_kernel(seq, x_ref, g_ref, w_ref, wt_ref, cw_ref, cb_ref, wg_ref, bg_ref, lam_ref, nw_ref,
                    ya_ref, hg_ref, qt_ref, k_ref, vt_ref, km_ref, prev_ref, h_ref):
    @pl.when((pl.program_id(0) * ROW_TILE) % seq == 0)
    def _():
        prev_ref[...] = jnp.zeros_like(prev_ref)
        h_ref[...] = jnp.zeros_like(h_ref)

    hb = _rms(x_ref[...], g_ref[...]).astype(BF16)
    _rglru_tile(_dot(hb, w_ref[:, COLS_LRU[0]:COLS_LRU[1]]), cw_ref, cb_ref, wg_ref, bg_ref, lam_ref,
                nw_ref, ya_ref, prev_ref, h_ref)
    hg_ref[...] = _dot(hb, w_ref[:, COLS_HG[0]:COLS_HG[1]])
    qt_ref[...] = _dot_t(wt_ref[0:ATT_WIDTH, :], hb)
    k = _dot(hb, w_ref[:, COLS_K[0]:COLS_K[1]])
    k_ref[...] = k.astype(BF16)
    for i in range(ROW_TILE // MOBA_BLOCK):
        km_ref[i] = jnp.mean(k[i * MOBA_BLOCK:(i + 1) * MOBA_BLOCK], axis=0, keepdims=True)
    vt = _dot_t(wt_ref[ATT_WIDTH:2 * ATT_WIDTH, :], hb).astype(BF16)
    ones = jnp.ones((ONES_ROWS, ROW_TILE), BF16)
    for h in range(ATT_HEADS):
        vt_ref[h * V_ROWS:h * V_ROWS + ATT_HD, :] = vt[h * ATT_HD:(h + 1) * ATT_HD, :]
        vt_ref[h * V_ROWS + ATT_HD:(h + 1) * V_ROWS, :] = ones


def _in_proj(x2, g, w, wt, layer, seq, lru_params):
    t, d = x2.shape
    nblk = t // MOBA_BLOCK
    stacked = lambda a: pl.BlockSpec((None,) + a.shape[1:], lambda i: (layer, 0, 0))
    vec = lambda a: pl.BlockSpec(a.shape, lambda i: (0, 0))
    row = lambda width: pl.BlockSpec((ROW_TILE, width), lambda i: (i, 0))
    col = lambda height: pl.BlockSpec((height, ROW_TILE), lambda i: (0, i))
    assert seq % ROW_TILE == 0 and ROW_TILE % LRU_SUB == 0 and ROW_TILE % MOBA_BLOCK == 0
    return pl.pallas_call(
        functools.partial(_in_proj_kernel, seq),
        grid=(t // ROW_TILE,),
        in_specs=[row(d), vec(g), stacked(w), stacked(wt)] + [vec(a) for a in lru_params],
        out_specs=[row(LRU_WIDTH), row(4 * HG_WIDTH), col(ATT_WIDTH), row(ATT_WIDTH),
                   col(ATT_HEADS * V_ROWS),
                   pl.BlockSpec((ROW_TILE // MOBA_BLOCK, 1, ATT_WIDTH), lambda i: (i, 0, 0))],
        out_shape=[jax.ShapeDtypeStruct((t, LRU_WIDTH), BF16),
                   jax.ShapeDtypeStruct((t, 4 * HG_WIDTH), F32),
                   jax.ShapeDtypeStruct((ATT_WIDTH, t), F32),
                   jax.ShapeDtypeStruct((t, ATT_WIDTH), BF16),
                   jax.ShapeDtypeStruct((ATT_HEADS * V_ROWS, t), BF16),
                   jax.ShapeDtypeStruct((nblk, 1, ATT_WIDTH), F32)],
        scratch_shapes=[pltpu.VMEM((LRU_SUB, LRU_WIDTH), F32), pltpu.VMEM((1, LRU_WIDTH), F32)],
        compiler_params=_params("arbitrary"),
        name="in_proj",
    )(x2, g, w, wt, *lru_params)


def _rglru_tile(xy, cw_ref, cb_ref, wg_ref, bg_ref, lam_ref, nw_ref, o_ref, prev_ref, h_ref):
    w = LRU_WIDTH
    row = lax.broadcasted_iota(jnp.int32, (LRU_SUB, w), 0)
    sp = _softplus(-lam_ref[...])
    for c in range(ROW_TILE // LRU_SUB):
        rows = slice(c * LRU_SUB, (c + 1) * LRU_SUB)
        xb = xy[rows, 0:w]
        yb = xy[rows, w:2 * w]
        prev = prev_ref[...]
        xc = xb * cw_ref[LRU_CONV - 1:LRU_CONV, :] + cb_ref[...]
        for k in range(1, LRU_CONV):
            shifted = jnp.where(row >= k, pltpu.roll(xb, k, 0), pltpu.roll(prev, k, 0))
            xc = xc + shifted * cw_ref[LRU_CONV - 1 - k:LRU_CONV - k, :]
        prev_ref[...] = xb
        gates = _dot(xc.astype(BF16), wg_ref[...]) + bg_ref[...]
        r = _sigmoid(gates[:, 0:w])
        i = _sigmoid(gates[:, w:2 * w])
        log_a = -LRU_C * r * sp
        a = jnp.exp(log_a)
        b = jnp.sqrt(jnp.maximum(1.0 - a * a, 0.0)) * (i * xc)
        s = 1
        while s < LRU_SUB:
            keep = row >= s
            a_s = jnp.where(keep, pltpu.roll(a, s, 0), 1.0)
            b_s = jnp.where(keep, pltpu.roll(b, s, 0), 0.0)
            b = a * b_s + b
            a = a * a_s
            s *= 2
        h = b + a * h_ref[...]
        h_ref[...] = h[LRU_SUB - 1:LRU_SUB, :]
        y = h * jax.nn.gelu(yb, approximate=True)
        o_ref[rows, :] = _rms(y, nw_ref[...]).astype(o_ref.dtype)


def _hgrn2_kernel(layer, z_ref, lbp_ref, nw_ref, o_ref, state_ref):
    @pl.when(pl.program_id(1) == 0)
    def _():
        state_ref[...] = jnp.zeros_like(state_ref)

    n = HG_WIDTH
    L = HG_CHUNK
    lbp = lbp_ref[...]
    e = jnp.exp(lbp - jnp.max(lbp, axis=0, keepdims=True))
    soft = e / jnp.sum(e, axis=0, keepdims=True)
    lb = jnp.sum(soft[0:layer + 1], axis=0, keepdims=True) - soft[0:1]
    log_lb = jnp.log(jnp.maximum(lb, TINY))
    log_1m = jnp.log1p(-lb)

    lane_h = lax.broadcasted_iota(jnp.int32, (1, n), 1) // HG_DK
    head_of_row = lax.broadcasted_iota(jnp.int32, (n, n), 0) // HG_DK
    head_of_col = lax.broadcasted_iota(jnp.int32, (n, n), 1) // HG_DK
    same_head = head_of_row == head_of_col
    ones_bd = jnp.where(same_head, 1.0, 0.0).astype(BF16)
    row = lax.broadcasted_iota(jnp.int32, (L, n), 0)
    sub = lax.broadcasted_iota(jnp.int32, (1, SUBLANES, 1), 1)
    t_idx = lax.broadcasted_iota(jnp.int32, (L, L), 0)
    s_idx = lax.broadcasted_iota(jnp.int32, (L, L), 1)

    def chunk(ci, carry):
        r0 = pl.multiple_of(ci * L, L)
        q = z_ref[pl.ds(r0, L), 0:n]
        f = z_ref[pl.ds(r0, L), n:2 * n]
        v = z_ref[pl.ds(r0, L), 2 * n:3 * n]
        g = z_ref[pl.ds(r0, L), 3 * n:4 * n]
        lsig = -_softplus(-f)
        t1 = log_1m + lsig
        log_f = jnp.maximum(log_lb, t1) + jnp.log(1.0 + jnp.exp(-jnp.abs(log_lb - t1)))
        kk = (1.0 - lb) * _sigmoid(-f)
        qs = q * _sigmoid(q)
        b = log_f * LOG2E
        s = 1
        while s < L:
            b = b + jnp.where(row >= s, pltpu.roll(b, s, 0), 0.0)
            s *= 2

        g8 = L // SUBLANES
        q3 = qs.reshape(g8, SUBLANES, n)
        k3 = kk.reshape(g8, SUBLANES, n)
        b3 = b.reshape(g8, SUBLANES, n)
        v3 = v.reshape(g8, SUBLANES, n)
        o = jnp.zeros((L, n), F32)
        for j in range(SUBLANES):
            d = jnp.where(sub >= j, b3 - b3[:, j:j + 1, :], NEG)
            xj = q3 * k3[:, j:j + 1, :] * jnp.exp2(d)
            aj = _dot(xj.reshape(L, n).astype(BF16), ones_bd)
            o = o + (aj.reshape(g8, SUBLANES, n) * v3[:, j:j + 1, :]).reshape(L, n)

        att = [jnp.zeros((L, L), F32) for _ in range(HG_HEADS)]
        c = SUBLANES
        while c < L:
            bc = b.reshape(L // c, c, n)
            bend = jnp.broadcast_to(bc[:, c - 1:c, :], (L // c, c, n)).reshape(L, n)
            bprev = jnp.where(row >= c, pltpu.roll(bend, c, 0), 0.0)
            qt = qs * jnp.exp2(b - bprev)
            kt = (kk * jnp.exp2(bend - b)).astype(BF16)
            pair = ((t_idx // c) % 2 == 1) & ((s_idx // c) == (t_idx // c) - 1)
            for h in range(HG_HEADS):
                sc = _dot_t(jnp.where(lane_h == h, qt, 0.0).astype(BF16), kt)
                att[h] = att[h] + jnp.where(pair, sc, 0.0)
            c *= 2
        att_cat = jnp.concatenate([a.astype(BF16) for a in att], axis=1)
        v_stack = jnp.concatenate([jnp.where(lane_h == h, v, 0.0).astype(BF16)
                                   for h in range(HG_HEADS)], axis=0)
        o = o + _dot(att_cat, v_stack)

        state = state_ref[...]
        o = o + _dot_t((qs * jnp.exp2(b)).astype(BF16), state.astype(BF16))
        b_last = b[L - 1:L, :]
        kdec = (kk * jnp.exp2(b_last - b)).astype(BF16)
        upd = lax.dot_general(v.astype(BF16), kdec, (((0,), (0,)), ((), ())),
                              preferred_element_type=F32)
        state_ref[...] = state * jnp.exp2(b_last) + jnp.where(same_head, upd, 0.0)

        ms = _dot((o * o).astype(BF16), ones_bd) * (1.0 / HG_DK)
        y = o * lax.rsqrt(ms + EPS) * nw_ref[...] * (g * _sigmoid(g))
        o_ref[pl.ds(r0, L), :] = y.astype(o_ref.dtype)
        return carry

    lax.fori_loop(0, HG_TILE // L, chunk, 0)


def _hgrn2(z, bsz, seq, lbp, nw, layer):
    nt = seq // HG_TILE
    n = HG_WIDTH
    assert seq % HG_TILE == 0 and HG_TILE % HG_CHUNK == 0
    return pl.pallas_call(
        functools.partial(_hgrn2_kernel, layer),
        grid=(bsz, nt),
        in_specs=[pl.BlockSpec((HG_TILE, 4 * n), lambda b, t: (b * nt + t, 0)),
                  pl.BlockSpec(lbp.shape, lambda b, t: (0, 0)),
                  pl.BlockSpec(nw.shape, lambda b, t: (0, 0))],
        out_specs=pl.BlockSpec((HG_TILE, n), lambda b, t: (b * nt + t, 0)),
        out_shape=jax.ShapeDtypeStruct((bsz * seq, n), BF16),
        scratch_shapes=[pltpu.VMEM((n, n), F32)],
        compiler_params=_params("parallel", "arbitrary"),
        name="hgrn2",
    )(z, lbp, nw)


def _moba_kernel(nblk, qt_ref, k_ref, vt_ref, km_ref, kx_ref, o_ref, qa_ref, sa_ref, sb_ref,
                 ta_ref, tb_ref, m_ref, acc_ref):
    j = pl.program_id(2)
    hp = pl.program_id(1)
    blk = MOBA_BLOCK
    hd = ATT_HD
    pair = 2 * hd
    span = MOBA_GROUP * blk
    scale = 1.0 / math.sqrt(hd)
    feat = lax.broadcasted_iota(jnp.int32, (pair, 1), 0)
    n_t = lax.broadcasted_iota(jnp.int32, (nblk, blk), 0)
    b_row = lax.broadcasted_iota(jnp.int32, (SUBLANES, blk), 0)
    causal = (lax.broadcasted_iota(jnp.int32, (blk, blk), 0)
              <= lax.broadcasted_iota(jnp.int32, (blk, blk), 1))
    qt = qt_ref[...]
    km = km_ref[...]

    def keys(start, size):
        return jnp.concatenate([k_ref[pl.ds(start, size), :], kx_ref[pl.ds(start, size), :]], axis=1)

    pad = jnp.zeros((pair - nblk - SUBLANES, blk), F32)
    qs, bias, gate = [], [], []
    for h in range(2):
        slope = jnp.exp2(-8.0 * (2 * hp + h + 1).astype(F32) / ATT_HEADS)
        rows = jnp.zeros((SUBLANES, blk), F32)
        for i, piece in enumerate(LOG2E_PIECES):
            rows = jnp.where(b_row == i, slope * piece, rows)
            rows = jnp.where(b_row == len(LOG2E_PIECES) + i, slope * (piece * blk), rows)
        bias.append(rows)
        qm = jnp.where(feat // hd == h, qt, 0.0)
        qs.append((qm * (scale * LOG2E)).astype(BF16))
        gate.append(jnp.dot(km, qm, precision=HIGHEST, preferred_element_type=F32))

    j0 = pl.multiple_of(j * blk, blk)
    k_own = keys(j0, blk)
    for h in range(2):
        own = jnp.concatenate([jnp.where(n_t == j, 0.0, NEG), bias[h], pad], axis=0)
        s = _dot(k_own, jnp.concatenate([qs[h], own.astype(BF16)], axis=0))
        sb_ref[h, 0:blk, :] = jnp.where(causal, s, NEG)

    def score(g, buf, heads=(0, 1)):
        s_ref, top_ref = buf
        k_g = keys(pl.multiple_of(g * span, span), span)
        for h in heads:
            s = _dot(k_g, qa_ref[h])
            s_ref[h] = s
            top_ref[h] = jnp.max(s, axis=0, keepdims=True)

    def attend(g, buf):
        s_ref, top_ref = buf
        vt_g = vt_ref[:, pl.ds(pl.multiple_of(g * span, span), span)]
        for h in range(2):
            m_old = m_ref[h]
            m_new = jnp.maximum(m_old, top_ref[h])
            p = jnp.exp2((s_ref[h] - m_new).astype(BF16))
            m_ref[h] = m_new
            pv = _dot(vt_g[h * V_ROWS:(h + 1) * V_ROWS, :], p)
            acc_ref[h] = jnp.exp2(m_old - m_new) * acc_ref[h] + pv

    last = nblk // MOBA_GROUP - 1
    buf_a = (sa_ref, ta_ref)
    buf_b = (sb_ref, tb_ref)

    for h in range(2):
        g = jnp.where(n_t < j, gate[h], NEG)
        chosen = jnp.zeros((nblk, blk), F32)
        for r in range(MOBA_TOPK):
            top = jnp.max(g, axis=0, keepdims=True)
            first = jnp.min(jnp.where(g == top, n_t, nblk), axis=0, keepdims=True)
            pick = n_t == first
            chosen = jnp.where(pick, jnp.where(j > r, 1.0, 0.0), chosen)
            g = jnp.where(pick, -jnp.inf, g)
        past = jnp.concatenate([jnp.where(chosen > 0.0, 0.0, NEG), bias[h], pad], axis=0)
        qa_ref[h] = jnp.concatenate([qs[h], past.astype(BF16)], axis=0)
        score(0, buf_a, heads=(h,))

    vt_own = vt_ref[:, pl.ds(j0, blk)]
    for h in range(2):
        s = sb_ref[h, 0:blk, :]
        m = jnp.max(s, axis=0, keepdims=True)
        p = jnp.exp2((s - m).astype(BF16))
        m_ref[h] = m
        acc_ref[h] = _dot(vt_own[h * V_ROWS:(h + 1) * V_ROWS, :], p)

    def chain(g0, count, more):
        for t in range(count):
            cur, nxt = (buf_a, buf_b) if t % 2 == 0 else (buf_b, buf_a)
            if t + 1 < count or more:
                score(jnp.minimum(g0 + t + 1, last), nxt)
            attend(g0 + t, cur)

    groups = (j + MOBA_GROUP - 1) // MOBA_GROUP
    nlong = groups // MOBA_LONG
    left = groups % MOBA_LONG

    def body(i, carry):
        chain(i * MOBA_LONG, MOBA_LONG, True)
        return carry

    lax.fori_loop(0, nlong, body, 0)
    g0 = nlong * MOBA_LONG
    size = MOBA_LONG // 2
    while size >= 1:
        take = (left & size) != 0

        @pl.when(take)
        def _(g0=g0, size=size):
            chain(g0, size, size > 1)

        g0 = g0 + jnp.where(take, size, 0)
        size //= 2

    out = []
    for h in range(2):
        acc = acc_ref[h]
        out.append(acc[0:hd] / acc[hd:hd + 1])
    o_ref[...] = jnp.concatenate(out, axis=0)


def _moba(qt, k, vt, km, bsz, seq):
    nblk = seq // MOBA_BLOCK
    blk = MOBA_BLOCK
    pair = 2 * ATT_HD
    assert nblk + SUBLANES <= pair and 2 * len(LOG2E_PIECES) <= SUBLANES
    assert MOBA_LONG & (MOBA_LONG - 1) == 0 and MOBA_LONG >= 2
    assert nblk % MOBA_GROUP == 0
    pos = jnp.arange(seq, dtype=jnp.int32)[:, None]
    col = jnp.arange(pair, dtype=jnp.int32)[None, :]
    extras = [pos % blk] * len(LOG2E_PIECES) + [pos // blk] * len(LOG2E_PIECES)
    kx = (pos // blk == col).astype(F32)
    for i, e in enumerate(extras):
        kx = jnp.where(col == nblk + i, e.astype(F32), kx)
    kx = kx.astype(BF16)
    return pl.pallas_call(
        functools.partial(_moba_kernel, nblk),
        grid=(bsz, ATT_HEADS // 2, nblk),
        in_specs=[pl.BlockSpec((pair, blk), lambda b, p, j: (p, b * nblk + j)),
                  pl.BlockSpec((seq, pair), lambda b, p, j: (b, p)),
                  pl.BlockSpec((2 * V_ROWS, seq), lambda b, p, j: (p, b)),
                  pl.BlockSpec((nblk, pair), lambda b, p, j: (b, p)),
                  pl.BlockSpec((seq, pair), lambda b, p, j: (0, 0))],
        out_specs=pl.BlockSpec((pair, blk), lambda b, p, j: (p, b * nblk + j)),
        out_shape=jax.ShapeDtypeStruct((ATT_WIDTH, bsz * seq), F32),
        scratch_shapes=[pltpu.VMEM((2, 2 * pair, blk), BF16),
                        pltpu.VMEM((2, MOBA_GROUP * blk, blk), F32),
                        pltpu.VMEM((2, MOBA_GROUP * blk, blk), F32),
                        pltpu.VMEM((2, 1, blk), F32),
                        pltpu.VMEM((2, 1, blk), F32),
                        pltpu.VMEM((2, 1, blk), F32),
                        pltpu.VMEM((2, V_ROWS, blk), F32)],
        compiler_params=_params("parallel", "parallel", "arbitrary"),
        name="moba",
    )(qt, k, vt, km, kx)


def _out_mlp_kernel(final, x_ref, ya_ref, yb_ref, yct_ref, an_ref, wo_ref, gm_ref, w1_ref, w2_ref,
                    gf_ref, o_ref):
    yc = _rms(yct_ref[...].T, an_ref[...]).astype(BF16)
    y = jnp.concatenate([ya_ref[...], yb_ref[...], yc], axis=1)
    x = x_ref[...] + _dot(y, wo_ref[...])
    hb = _rms(x, gm_ref[...]).astype(BF16)
    mlp = None
    for c in range(w1_ref.shape[1] // FF_CHUNK):
        cols = slice(c * FF_CHUNK, (c + 1) * FF_CHUNK)
        u = jnp.square(jnp.maximum(_dot(hb, w1_ref[:, cols]), 0.0))
        part = _dot(u.astype(BF16), w2_ref[cols, :])
        mlp = part if mlp is None else mlp + part
    x = x + mlp
    if final:
        x = _rms(x, gf_ref[...])
    o_ref[...] = x


def _out_mlp(x2, ya, yb, yct, an, wo, gm, w1, w2, gf, layer, final):
    t, d = x2.shape
    stacked = lambda a: pl.BlockSpec((None,) + a.shape[1:], lambda i: (layer, 0, 0))
    row = lambda a: pl.BlockSpec((ROW_TILE, a.shape[1]), lambda i: (i, 0))
    full = lambda a: pl.BlockSpec(a.shape, lambda i: (0, 0))
    assert t % ROW_TILE == 0 and w1.shape[2] % FF_CHUNK == 0
    return pl.pallas_call(
        functools.partial(_out_mlp_kernel, final),
        grid=(t // ROW_TILE,),
        in_specs=[row(x2), row(ya), row(yb), pl.BlockSpec((yct.shape[0], ROW_TILE), lambda i: (0, i)),
                  full(an), stacked(wo), full(gm), stacked(w1), stacked(w2), full(gf)],
        out_specs=pl.BlockSpec((ROW_TILE, d), lambda i: (i, 0)),
        out_shape=jax.ShapeDtypeStruct((t, d), F32),
        compiler_params=_params("parallel"),
        name="out_mlp",
    )(x2, ya, yb, yct, an, wo, gm, w1, w2, gf)


def _block_diag(w):
    g, n, _ = w.shape
    eye = jnp.eye(g, dtype=w.dtype)
    return (eye[:, None, :, None] * w[:, :, None, :]).reshape(g * n, g * n)


def kernel(x, w_in, w_out, norm_mix, norm_mlp, lru_conv_w, lru_conv_b, lru_wa, lru_ba, lru_wx, lru_bx,
           lru_lambda, hg_lower_bounds, hg_norm_w, lru_out_norm, att_out_norm, w_ff1, w_ff2, norm_final):
    bsz, seq, d = x.shape
    depth = w_in.shape[0]
    x2 = x.reshape(bsz * seq, d)
    r2 = lambda a: a.reshape(1, -1)
    w_in_b = w_in.astype(BF16)
    w_qv = jnp.concatenate([w_in[:, :, COLS_Q[0]:COLS_Q[1]], w_in[:, :, COLS_V[0]:COLS_V[1]]], axis=2)
    wt_b = jnp.swapaxes(lax.optimization_barrier(w_qv), 1, 2).astype(BF16)
    w_out_b, w_ff1_b, w_ff2_b = w_out.astype(BF16), w_ff1.astype(BF16), w_ff2.astype(BF16)
    for l in range(depth):
        wg = jnp.concatenate([_block_diag(lru_wa[l]), _block_diag(lru_wx[l])], axis=1).astype(BF16)
        bg = jnp.concatenate([lru_ba[l], lru_bx[l]]).reshape(1, -1)
        lru_params = (lru_conv_w[l], r2(lru_conv_b[l]), wg, bg, r2(lru_lambda[l]), r2(lru_out_norm[l]))
        ya, hg, cqt, ck, cvt, km = _in_proj(x2, r2(norm_mix[l]), w_in_b, wt_b, l, seq, lru_params)
        yb = _hgrn2(hg, bsz, seq, hg_lower_bounds, r2(jnp.tile(hg_norm_w[l], HG_HEADS)), l)
        yc = _moba(cqt, ck, cvt, km.reshape(-1, km.shape[-1]), bsz, seq)
        x2 = _out_mlp(x2, ya, yb, yc, r2(att_out_norm[l]), w_out_b, r2(norm_mlp[l]), w_ff1_b, w_ff2_b,
                      r2(norm_final), l, l == depth - 1)
    return x2.reshape(bsz, seq, d)
```

```python
import functools
import math
import struct

import jax
import jax.numpy as jnp
from jax import lax
from jax.experimental import pallas as pl
from jax.experimental.pallas import tpu as pltpu

F32 = jnp.float32
BF16 = jnp.bfloat16
HIGHEST = lax.Precision.HIGHEST

EPS = 1e-6
NEG = -1e30
TINY = 1e-30

LRU_WIDTH = 256
LRU_CONV = 4
LRU_C = 8.0
HG_HEADS = 4
HG_DK = 64
ATT_HEADS = 8
ATT_HD = 64
MOBA_BLOCK = 256
MOBA_TOPK = 3

HG_WIDTH = HG_HEADS * HG_DK
ATT_WIDTH = ATT_HEADS * ATT_HD
COLS_LRU = (0, 2 * LRU_WIDTH)
COLS_HG = (COLS_LRU[1], COLS_LRU[1] + 4 * HG_WIDTH)
COLS_Q = (COLS_HG[1], COLS_HG[1] + ATT_WIDTH)
COLS_K = (COLS_Q[1], COLS_Q[1] + ATT_WIDTH)
COLS_V = (COLS_K[1], COLS_K[1] + ATT_WIDTH)

SUBLANES = 8
VMEM_LIMIT = 56 * 1024 * 1024

ROW_TILE = 512
LRU_SUB = 256
HG_TILE = 1024
HG_CHUNK = 256
FF_CHUNK = 1024
MOBA_GROUP = 2
MOBA_LONG = 8
ONES_ROWS = 16
V_ROWS = ATT_HD + ONES_ROWS


def _bf16_pieces(x, n):
    pieces = []
    for _ in range(n):
        bits = struct.unpack("<I", struct.pack("<f", x))[0]
        bits = (bits + 0x7FFF + ((bits >> 16) & 1)) & 0xFFFF0000
        piece = struct.unpack("<f", struct.pack("<I", bits))[0]
        pieces.append(piece)
        x -= piece
    return tuple(pieces)


LOG2E = math.log2(math.e)
PIECES = 3
SLOPE_PIECES = tuple(_bf16_pieces(2.0 ** (-8.0 * (i + 1) / ATT_HEADS) * LOG2E, PIECES)
                     for i in range(ATT_HEADS))


def _dot(a, b):
    return jnp.dot(a, b, preferred_element_type=F32)


def _dot_t(a, b):
    return lax.dot_general(a, b, (((1,), (1,)), ((), ())), preferred_element_type=F32)


def _rms(x, g):
    return x * lax.rsqrt(jnp.mean(x * x, axis=-1, keepdims=True) + EPS) * g


def _sigmoid(x):
    return 1.0 / (1.0 + jnp.exp(-x))


def _softplus(x):
    return jnp.maximum(x, 0.0) + jnp.log(1.0 + jnp.exp(-jnp.abs(x)))


def _params(*sem):
    return pltpu.CompilerParams(dimension_semantics=sem, vmem_limit_bytes=VMEM_LIMIT)


def _in_proj_kernel(seq, x_ref, g_ref, w_ref, wt_ref, cw_ref, cb_ref, wg_ref, bg_ref, lam_ref, nw_ref,
                    ya_ref, hg_ref, qt_ref, k_ref, vt_ref, km_ref, prev_ref, h_ref):
    @pl.when((pl.program_id(0) * ROW_TILE) % seq == 0)
    def _():
        prev_ref[...] = jnp.zeros_like(prev_ref)
        h_ref[...] = jnp.zeros_like(h_ref)

    hb = _rms(x_ref[...], g_ref[...]).astype(BF16)
    _rglru_tile(_dot(hb, w_ref[:, COLS_LRU[0]:COLS_LRU[1]]), cw_ref, cb_ref, wg_ref, bg_ref, lam_ref,
                nw_ref, ya_ref, prev_ref, h_ref)
    hg_ref[...] = _dot(hb, w_ref[:, COLS_HG[0]:COLS_HG[1]])
    qt_ref[...] = _dot_t(wt_ref[0:ATT_WIDTH, :], hb)
    k = _dot(hb, w_ref[:, COLS_K[0]:COLS_K[1]])
    k_ref[...] = k.astype(BF16)
    for i in range(ROW_TILE // MOBA_BLOCK):
        km_ref[i] = jnp.mean(k[i * MOBA_BLOCK:(i + 1) * MOBA_BLOCK], axis=0, keepdims=True)
    vt = _dot_t(wt_ref[ATT_WIDTH:2 * ATT_WIDTH, :], hb).astype(BF16)
    ones = jnp.ones((ONES_ROWS, ROW_TILE), BF16)
    for h in range(ATT_HEADS):
        vt_ref[h * V_ROWS:h * V_ROWS + ATT_HD, :] = vt[h * ATT_HD:(h + 1) * ATT_HD, :]
        vt_ref[h * V_ROWS + ATT_HD:(h + 1) * V_ROWS, :] = ones


def _in_proj(x2, g, w, wt, layer, seq, lru_params):
    t, d = x2.shape
    nblk = t // MOBA_BLOCK
    stacked = lambda a: pl.BlockSpec((None,) + a.shape[1:], lambda i: (layer, 0, 0))
    vec = lambda a: pl.BlockSpec(a.shape, lambda i: (0, 0))
    row = lambda width: pl.BlockSpec((ROW_TILE, width), lambda i: (i, 0))
    col = lambda height: pl.BlockSpec((height, ROW_TILE), lambda i: (0, i))
    assert seq % ROW_TILE == 0 and ROW_TILE % LRU_SUB == 0 and ROW_TILE % MOBA_BLOCK == 0
    return pl.pallas_call(
        functools.partial(_in_proj_kernel, seq),
        grid=(t // ROW_TILE,),
        in_specs=[row(d), vec(g), stacked(w), stacked(wt)] + [vec(a) for a in lru_params],
        out_specs=[row(LRU_WIDTH), row(4 * HG_WIDTH), col(ATT_WIDTH), row(ATT_WIDTH),
                   col(ATT_HEADS * V_ROWS),
                   pl.BlockSpec((ROW_TILE // MOBA_BLOCK, 1, ATT_WIDTH), lambda i: (i, 0, 0))],
        out_shape=[jax.ShapeDtypeStruct((t, LRU_WIDTH), BF16),
                   jax.ShapeDtypeStruct((t, 4 * HG_WIDTH), F32),
                   jax.ShapeDtypeStruct((ATT_WIDTH, t), F32),
                   jax.ShapeDtypeStruct((t, ATT_WIDTH), BF16),
                   jax.ShapeDtypeStruct((ATT_HEADS * V_ROWS, t), BF16),
                   jax.ShapeDtypeStruct((nblk, 1, ATT_WIDTH), F32)],
        scratch_shapes=[pltpu.VMEM((LRU_SUB, LRU_WIDTH), F32), pltpu.VMEM((1, LRU_WIDTH), F32)],
        compiler_params=_params("arbitrary"),
        name="in_proj",
    )(x2, g, w, wt, *lru_params)


def _rglru_tile(xy, cw_ref, cb_ref, wg_ref, bg_ref, lam_ref, nw_ref, o_ref, prev_ref, h_ref):
    w = LRU_WIDTH
    row = lax.broadcasted_iota(jnp.int32, (LRU_SUB, w), 0)
    sp = _softplus(-lam_ref[...])
    for c in range(ROW_TILE // LRU_SUB):
        rows = slice(c * LRU_SUB, (c + 1) * LRU_SUB)
        xb = xy[rows, 0:w]
        yb = xy[rows, w:2 * w]
        prev = prev_ref[...]
        xc = xb * cw_ref[LRU_CONV - 1:LRU_CONV, :] + cb_ref[...]
        for k in range(1, LRU_CONV):
            shifted = jnp.where(row >= k, pltpu.roll(xb, k, 0), pltpu.roll(prev, k, 0))
            xc = xc + shifted * cw_ref[LRU_CONV - 1 - k:LRU_CONV - k, :]
        prev_ref[...] = xb
        gates = _dot(xc.astype(BF16), wg_ref[...]) + bg_ref[...]
        r = _sigmoid(gates[:, 0:w])
        i = _sigmoid(gates[:, w:2 * w])
        log_a = -LRU_C * r * sp
        a = jnp.exp(log_a)
        b = jnp.sqrt(jnp.maximum(1.0 - a * a, 0.0)) * (i * xc)
        s = 1
        while s < LRU_SUB:
            keep = row >= s
            a_s = jnp.where(keep, pltpu.roll(a, s, 0), 1.0)
            b_s = jnp.where(keep, pltpu.roll(b, s, 0), 0.0)
            b = a * b_s + b
            a = a * a_s
            s *= 2
        h = b + a * h_ref[...]
        h_ref[...] = h[LRU_SUB - 1:LRU_SUB, :]
        y = h * jax.nn.gelu(yb, approximate=True)
        o_ref[rows, :] = _rms(y, nw_ref[...]).astype(o_ref.dtype)


def _hgrn2_kernel(layer, z_ref, lbp_ref, nw_ref, o_ref, state_ref):
    @pl.when(pl.program_id(1) == 0)
    def _():
        state_ref[...] = jnp.zeros_like(state_ref)

    n = HG_WIDTH
    L = HG_CHUNK
    lbp = lbp_ref[...]
    e = jnp.exp(lbp - jnp.max(lbp, axis=0, keepdims=True))
    soft = e / jnp.sum(e, axis=0, keepdims=True)
    lb = jnp.sum(soft[0:layer + 1], axis=0, keepdims=True) - soft[0:1]
    log_lb = jnp.log(jnp.maximum(lb, TINY))
    log_1m = jnp.log1p(-lb)

    lane_h = lax.broadcasted_iota(jnp.int32, (1, n), 1) // HG_DK
    head_of_row = lax.broadcasted_iota(jnp.int32, (n, n), 0) // HG_DK
    head_of_col = lax.broadcasted_iota(jnp.int32, (n, n), 1) // HG_DK
    same_head = head_of_row == head_of_col
    ones_bd = jnp.where(same_head, 1.0, 0.0).astype(BF16)
    row = lax.broadcasted_iota(jnp.int32, (L, n), 0)
    sub = lax.broadcasted_iota(jnp.int32, (1, SUBLANES, 1), 1)
    t_idx = lax.broadcasted_iota(jnp.int32, (L, L), 0)
    s_idx = lax.broadcasted_iota(jnp.int32, (L, L), 1)

    def chunk(ci, carry):
        r0 = pl.multiple_of(ci * L, L)
        q = z_ref[pl.ds(r0, L), 0:n]
        f = z_ref[pl.ds(r0, L), n:2 * n]
        v = z_ref[pl.ds(r0, L), 2 * n:3 * n]
        g = z_ref[pl.ds(r0, L), 3 * n:4 * n]
        lsig = -_softplus(-f)
        t1 = log_1m + lsig
        log_f = jnp.maximum(log_lb, t1) + jnp.log(1.0 + jnp.exp(-jnp.abs(log_lb - t1)))
        kk = (1.0 - lb) * _sigmoid(-f)
        qs = q * _sigmoid(q)
        b = log_f * LOG2E
        s = 1
        while s < L:
            b = b + jnp.where(row >= s, pltpu.roll(b, s, 0), 0.0)
            s *= 2

        g8 = L // SUBLANES
        q3 = qs.reshape(g8, SUBLANES, n)
        k3 = kk.reshape(g8, SUBLANES, n)
        b3 = b.reshape(g8, SUBLANES, n)
        v3 = v.reshape(g8, SUBLANES, n)
        o = jnp.zeros((L, n), F32)
        for j in range(SUBLANES):
            d = jnp.where(sub >= j, b3 - b3[:, j:j + 1, :], NEG)
            xj = q3 * k3[:, j:j + 1, :] * jnp.exp2(d)
            aj = _dot(xj.reshape(L, n).astype(BF16), ones_bd)
            o = o + (aj.reshape(g8, SUBLANES, n) * v3[:, j:j + 1, :]).reshape(L, n)

        att = [jnp.zeros((L, L), F32) for _ in range(HG_HEADS)]
        c = SUBLANES
        while c < L:
            bc = b.reshape(L // c, c, n)
            bend = jnp.broadcast_to(bc[:, c - 1:c, :], (L // c, c, n)).reshape(L, n)
            bprev = jnp.where(row >= c, pltpu.roll(bend, c, 0), 0.0)
            qt = qs * jnp.exp2(b - bprev)
            kt = (kk * jnp.exp2(bend - b)).astype(BF16)
            pair = ((t_idx // c) % 2 == 1) & ((s_idx // c) == (t_idx // c) - 1)
            for h in range(HG_HEADS):
                sc = _dot_t(jnp.where(lane_h == h, qt, 0.0).astype(BF16), kt)
                att[h] = att[h] + jnp.where(pair, sc, 0.0)
            c *= 2
        att_cat = jnp.concatenate([a.astype(BF16) for a in att], axis=1)
        v_stack = jnp.concatenate([jnp.where(lane_h == h, v, 0.0).astype(BF16)
                                   for h in range(HG_HEADS)], axis=0)
        o = o + _dot(att_cat, v_stack)

        state = state_ref[...]
        o = o + _dot_t((qs * jnp.exp2(b)).astype(BF16), state.astype(BF16))
        b_last = b[L - 1:L, :]
        kdec = (kk * jnp.exp2(b_last - b)).astype(BF16)
        upd = lax.dot_general(v.astype(BF16), kdec, (((0,), (0,)), ((), ())),
                              preferred_element_type=F32)
        state_ref[...] = state * jnp.exp2(b_last) + jnp.where(same_head, upd, 0.0)

        ms = _dot((o * o).astype(BF16), ones_bd) * (1.0 / HG_DK)
        y = o * lax.rsqrt(ms + EPS) * nw_ref[...] * (g * _sigmoid(g))
        o_ref[pl.ds(r0, L), :] = y.astype(o_ref.dtype)
        return carry

    for ci in range(HG_TILE // L):
        chunk(ci, 0)


def _hgrn2(z, bsz, seq, lbp, nw, layer):
    nt = seq // HG_TILE
    n = HG_WIDTH
    assert seq % HG_TILE == 0 and HG_TILE % HG_CHUNK == 0
    return pl.pallas_call(
        functools.partial(_hgrn2_kernel, layer),
        grid=(bsz, nt),
        in_specs=[pl.BlockSpec((HG_TILE, 4 * n), lambda b, t: (b * nt + t, 0)),
                  pl.BlockSpec(lbp.shape, lambda b, t: (0, 0)),
                  pl.BlockSpec(nw.shape, lambda b, t: (0, 0))],
        out_specs=pl.BlockSpec((HG_TILE, n), lambda b, t: (b * nt + t, 0)),
        out_shape=jax.ShapeDtypeStruct((bsz * seq, n), BF16),
        scratch_shapes=[pltpu.VMEM((n, n), F32)],
        compiler_params=_params("parallel", "arbitrary"),
        name="hgrn2",
    )(z, lbp, nw)


def _moba_kernel(nblk, qt_ref, k_ref, vt_ref, km_ref, kx_ref, o_ref, qa_ref, sa_ref, sb_ref,
                 ta_ref, tb_ref, m_ref, acc_ref):
    j = pl.program_id(2)
    hp = pl.program_id(1)
    blk = MOBA_BLOCK
    hd = ATT_HD
    pair = 2 * hd
    span = MOBA_GROUP * blk
    scale = 1.0 / math.sqrt(hd)
    feat = lax.broadcasted_iota(jnp.int32, (pair, 1), 0)
    n_t = lax.broadcasted_iota(jnp.int32, (nblk, blk), 0)
    b_row = lax.broadcasted_iota(jnp.int32, (SUBLANES, blk), 0)
    causal = (lax.broadcasted_iota(jnp.int32, (blk, blk), 0)
              <= lax.broadcasted_iota(jnp.int32, (blk, blk), 1))
    qt = qt_ref[...]
    km = km_ref[...]

    def keys(start, size):
        return jnp.concatenate([k_ref[pl.ds(start, size), :], kx_ref[pl.ds(start, size), :]], axis=1)

    pad = jnp.zeros((pair - nblk - SUBLANES, blk), F32)
    qs, bias, gate = [], [], []
    for h in range(2):
        rows = jnp.zeros((SUBLANES, blk), F32)
        for i in range(PIECES):
            piece = jnp.float32(SLOPE_PIECES[h][i])
            for head in range(h + 2, ATT_HEADS, 2):
                piece = jnp.where(2 * hp + h == head, SLOPE_PIECES[head][i], piece)
            rows = jnp.where(b_row == i, piece, rows)
            rows = jnp.where(b_row == PIECES + i, piece * blk, rows)
        bias.append(rows)
        qm = jnp.where(feat // hd == h, qt, 0.0)
        qs.append((qm * (scale * LOG2E)).astype(BF16))
        gate.append(jnp.dot(km, qm, precision=HIGHEST, preferred_element_type=F32))

    j0 = pl.multiple_of(j * blk, blk)
    k_own = keys(j0, blk)
    for h in range(2):
        own = jnp.concatenate([jnp.where(n_t == j, 0.0, NEG), bias[h], pad], axis=0)
        s = _dot(k_own, jnp.concatenate([qs[h], own.astype(BF16)], axis=0))
        sb_ref[h, 0:blk, :] = jnp.where(causal, s, NEG)

    def score(g, buf, heads=(0, 1)):
        s_ref, top_ref = buf
        k_g = keys(pl.multiple_of(g * span, span), span)
        for h in heads:
            s = _dot(k_g, qa_ref[h])
            s_ref[h] = s
            top_ref[h] = jnp.max(s, axis=0, keepdims=True)

    def attend(g, buf):
        s_ref, top_ref = buf
        vt_g = vt_ref[:, pl.ds(pl.multiple_of(g * span, span), span)]
        for h in range(2):
            m_old = m_ref[h]
            m_new = jnp.maximum(m_old, top_ref[h])
            p = jnp.exp2((s_ref[h] - m_new).astype(BF16))
            m_ref[h] = m_new
            pv = _dot(vt_g[h * V_ROWS:(h + 1) * V_ROWS, :], p)
            acc_ref[h] = jnp.exp2(m_old - m_new) * acc_ref[h] + pv

    last = nblk // MOBA_GROUP - 1
    buf_a = (sa_ref, ta_ref)
    buf_b = (sb_ref, tb_ref)

    for h in range(2):
        g = jnp.where(n_t < j, gate[h], NEG)
        chosen = jnp.zeros((nblk, blk), F32)
        for r in range(MOBA_TOPK):
            top = jnp.max(g, axis=0, keepdims=True)
            first = jnp.min(jnp.where(g == top, n_t, nblk), axis=0, keepdims=True)
            pick = n_t == first
            chosen = jnp.where(pick, jnp.where(j > r, 1.0, 0.0), chosen)
            g = jnp.where(pick, -jnp.inf, g)
        past = jnp.concatenate([jnp.where(chosen > 0.0, 0.0, NEG), bias[h], pad], axis=0)
        qa_ref[h] = jnp.concatenate([qs[h], past.astype(BF16)], axis=0)
        score(0, buf_a, heads=(h,))

    vt_own = vt_ref[:, pl.ds(j0, blk)]
    for h in range(2):
        s = sb_ref[h, 0:blk, :]
        m = jnp.max(s, axis=0, keepdims=True)
        p = jnp.exp2((s - m).astype(BF16))
        m_ref[h] = m
        acc_ref[h] = _dot(vt_own[h * V_ROWS:(h + 1) * V_ROWS, :], p)

    def chain(g0, count, more):
        for t in range(count):
            cur, nxt = (buf_a, buf_b) if t % 2 == 0 else (buf_b, buf_a)
            if t + 1 < count or more:
                score(jnp.minimum(g0 + t + 1, last), nxt)
            attend(g0 + t, cur)

    groups = (j + MOBA_GROUP - 1) // MOBA_GROUP
    nlong = groups // MOBA_LONG
    left = groups % MOBA_LONG

    def body(i, carry):
        chain(i * MOBA_LONG, MOBA_LONG, True)
        return carry

    lax.fori_loop(0, nlong, body, 0)
    g0 = nlong * MOBA_LONG
    size = MOBA_LONG // 2
    while size >= 1:
        take = (left & size) != 0

        @pl.when(take)
        def _(g0=g0, size=size):
            chain(g0, size, size > 1)

        g0 = g0 + jnp.where(take, size, 0)
        size //= 2

    out = []
    for h in range(2):
        acc = acc_ref[h]
        out.append(acc[0:hd] / acc[hd:hd + 1])
    o_ref[...] = jnp.concatenate(out, axis=0)


def _moba(qt, k, vt, km, bsz, seq):
    nblk = seq // MOBA_BLOCK
    blk = MOBA_BLOCK
    pair = 2 * ATT_HD
    assert nblk + SUBLANES <= pair and 2 * PIECES <= SUBLANES
    assert MOBA_LONG & (MOBA_LONG - 1) == 0 and MOBA_LONG >= 2
    assert nblk % MOBA_GROUP == 0
    pos = jnp.arange(seq, dtype=jnp.int32)[:, None]
    col = jnp.arange(pair, dtype=jnp.int32)[None, :]
    extras = [pos % blk] * PIECES + [pos // blk] * PIECES
    kx = (pos // blk == col).astype(F32)
    for i, e in enumerate(extras):
        kx = jnp.where(col == nblk + i, e.astype(F32), kx)
    kx = kx.astype(BF16)
    return pl.pallas_call(
        functools.partial(_moba_kernel, nblk),
        grid=(bsz, ATT_HEADS // 2, nblk),
        in_specs=[pl.BlockSpec((pair, blk), lambda b, p, j: (p, b * nblk + j)),
                  pl.BlockSpec((seq, pair), lambda b, p, j: (b, p)),
                  pl.BlockSpec((2 * V_ROWS, seq), lambda b, p, j: (p, b)),
                  pl.BlockSpec((nblk, pair), lambda b, p, j: (b, p)),
                  pl.BlockSpec((seq, pair), lambda b, p, j: (0, 0))],
        out_specs=pl.BlockSpec((pair, blk), lambda b, p, j: (p, b * nblk + j)),
        out_shape=jax.ShapeDtypeStruct((ATT_WIDTH, bsz * seq), F32),
        scratch_shapes=[pltpu.VMEM((2, 2 * pair, blk), BF16),
                        pltpu.VMEM((2, MOBA_GROUP * blk, blk), F32),
                        pltpu.VMEM((2, MOBA_GROUP * blk, blk), F32),
                        pltpu.VMEM((2, 1, blk), F32),
                        pltpu.VMEM((2, 1, blk), F32),
                        pltpu.VMEM((2, 1, blk), F32),
                        pltpu.VMEM((2, V_ROWS, blk), F32)],
        compiler_params=_params("parallel", "parallel", "arbitrary"),
        name="moba",
    )(qt, k, vt, km, kx)


def _out_mlp_kernel(final, x_ref, ya_ref, yb_ref, yct_ref, an_ref, wo_ref, gm_ref, w1_ref, w2_ref,
                    gf_ref, o_ref):
    yc = _rms(yct_ref[...].T, an_ref[...]).astype(BF16)
    y = jnp.concatenate([ya_ref[...], yb_ref[...], yc], axis=1)
    x = x_ref[...] + _dot(y, wo_ref[...])
    hb = _rms(x, gm_ref[...]).astype(BF16)
    mlp = None
    for c in range(w1_ref.shape[1] // FF_CHUNK):
        cols = slice(c * FF_CHUNK, (c + 1) * FF_CHUNK)
        u = jnp.square(jnp.maximum(_dot(hb, w1_ref[:, cols]), 0.0))
        part = _dot(u.astype(BF16), w2_ref[cols, :])
        mlp = part if mlp is None else mlp + part
    x = x + mlp
    if final:
        x = _rms(x, gf_ref[...])
    o_ref[...] = x


def _out_mlp(x2, ya, yb, yct, an, wo, gm, w1, w2, gf, layer, final):
    t, d = x2.shape
    stacked = lambda a: pl.BlockSpec((None,) + a.shape[1:], lambda i: (layer, 0, 0))
    row = lambda a: pl.BlockSpec((ROW_TILE, a.shape[1]), lambda i: (i, 0))
    full = lambda a: pl.BlockSpec(a.shape, lambda i: (0, 0))
    assert t % ROW_TILE == 0 and w1.shape[2] % FF_CHUNK == 0
    return pl.pallas_call(
        functools.partial(_out_mlp_kernel, final),
        grid=(t // ROW_TILE,),
        in_specs=[row(x2), row(ya), row(yb), pl.BlockSpec((yct.shape[0], ROW_TILE), lambda i: (0, i)),
                  full(an), stacked(wo), full(gm), stacked(w1), stacked(w2), full(gf)],
        out_specs=pl.BlockSpec((ROW_TILE, d), lambda i: (i, 0)),
        out_shape=jax.ShapeDtypeStruct((t, d), F32),
        compiler_params=_params("parallel"),
        name="out_mlp",
    )(x2, ya, yb, yct, an, wo, gm, w1, w2, gf)


def _block_diag(w):
    g, n, _ = w.shape
    eye = jnp.eye(g, dtype=w.dtype)
    return (eye[:, None, :, None] * w[:, :, None, :]).reshape(g * n, g * n)


def kernel(x, w_in, w_out, norm_mix, norm_mlp, lru_conv_w, lru_conv_b, lru_wa, lru_ba, lru_wx, lru_bx,
           lru_lambda, hg_lower_bounds, hg_norm_w, lru_out_norm, att_out_norm, w_ff1, w_ff2, norm_final):
    bsz, seq, d = x.shape
    depth = w_in.shape[0]
    x2 = x.reshape(bsz * seq, d)
    r2 = lambda a: a.reshape(1, -1)
    w_in_b = w_in.astype(BF16)
    w_qv = jnp.concatenate([w_in[:, :, COLS_Q[0]:COLS_Q[1]], w_in[:, :, COLS_V[0]:COLS_V[1]]], axis=2)
    wt_b = jnp.swapaxes(lax.optimization_barrier(w_qv), 1, 2).astype(BF16)
    w_out_b, w_ff1_b, w_ff2_b = w_out.astype(BF16), w_ff1.astype(BF16), w_ff2.astype(BF16)
    for l in range(depth):
        wg = jnp.concatenate([_block_diag(lru_wa[l]), _block_diag(lru_wx[l])], axis=1).astype(BF16)
        bg = jnp.concatenate([lru_ba[l], lru_bx[l]]).reshape(1, -1)
        lru_params = (lru_conv_w[l], r2(lru_conv_b[l]), wg, bg, r2(lru_lambda[l]), r2(lru_out_norm[l]))
        ya, hg, cqt, ck, cvt, km = _in_proj(x2, r2(norm_mix[l]), w_in_b, wt_b, l, seq, lru_params)
        yb = _hgrn2(hg, bsz, seq, hg_lower_bounds, r2(jnp.tile(hg_norm_w[l], HG_HEADS)), l)
        yc = _moba(cqt, ck, cvt, km.reshape(-1, km.shape[-1]), bsz, seq)
        x2 = _out_mlp(x2, ya, yb, yc, r2(att_out_norm[l]), w_out_b, r2(norm_mlp[l]), w_ff1_b, w_ff2_b,
                      r2(norm_final), l, l == depth - 1)
    return x2.reshape(bsz, seq, d)
```

```python
import functools
import math
import struct

import jax
import jax.numpy as jnp
from jax import lax
from jax.experimental import pallas as pl
from jax.experimental.pallas import tpu as pltpu

F32 = jnp.float32
BF16 = jnp.bfloat16
HIGHEST = lax.Precision.HIGHEST

EPS = 1e-6
NEG = -1e30
TINY = 1e-30

LRU_WIDTH = 256
LRU_CONV = 4
LRU_C = 8.0
HG_HEADS = 4
HG_DK = 64
ATT_HEADS = 8
ATT_HD = 64
MOBA_BLOCK = 256
MOBA_TOPK = 3

HG_WIDTH = HG_HEADS * HG_DK
ATT_WIDTH = ATT_HEADS * ATT_HD
COLS_LRU = (0, 2 * LRU_WIDTH)
COLS_HG = (COLS_LRU[1], COLS_LRU[1] + 4 * HG_WIDTH)
COLS_Q = (COLS_HG[1], COLS_HG[1] + ATT_WIDTH)
COLS_K = (COLS_Q[1], COLS_Q[1] + ATT_WIDTH)
COLS_V = (COLS_K[1], COLS_K[1] + ATT_WIDTH)

SUBLANES = 8
VMEM_LIMIT = 56 * 1024 * 1024

ROW_TILE = 1024
MLP_TILE = 512
LRU_SUB = 256
HG_TILE = 1024
HG_CHUNK = 256
FF_CHUNK = 1024
MOBA_GROUP = 2
MOBA_LONG = 8
ONES_ROWS = 16
V_ROWS = ATT_HD + ONES_ROWS


def _bf16_pieces(x, n):
    pieces = []
    for _ in range(n):
        bits = struct.unpack("<I", struct.pack("<f", x))[0]
        bits = (bits + 0x7FFF + ((bits >> 16) & 1)) & 0xFFFF0000
        piece = struct.unpack("<f", struct.pack("<I", bits))[0]
        pieces.append(piece)
        x -= piece
    return tuple(pieces)


LOG2E = math.log2(math.e)
PIECES = 3
SLOPE_PIECES = tuple(_bf16_pieces(2.0 ** (-8.0 * (i + 1) / ATT_HEADS) * LOG2E, PIECES)
                     for i in range(ATT_HEADS))


def _dot(a, b):
    return jnp.dot(a, b, preferred_element_type=F32)


def _dot_t(a, b):
    return lax.dot_general(a, b, (((1,), (1,)), ((), ())), preferred_element_type=F32)


def _rms(x, g):
    return x * lax.rsqrt(jnp.mean(x * x, axis=-1, keepdims=True) + EPS) * g


def _sigmoid(x):
    return 1.0 / (1.0 + jnp.exp(-x))


def _softplus(x):
    return jnp.maximum(x, 0.0) + jnp.log(1.0 + jnp.exp(-jnp.abs(x)))


def _params(*sem):
    return pltpu.CompilerParams(dimension_semantics=sem, vmem_limit_bytes=VMEM_LIMIT)


def _in_proj_kernel(seq, x_ref, g_ref, w_ref, wt_ref, cw_ref, cb_ref, wg_ref, bg_ref, lam_ref, nw_ref,
                    ya_ref, hg_ref, qt_ref, k_ref, vt_ref, km_ref, prev_ref, h_ref):
    @pl.when((pl.program_id(0) * ROW_TILE) % seq == 0)
    def _():
        prev_ref[...] = jnp.zeros_like(prev_ref)
        h_ref[...] = jnp.zeros_like(h_ref)

    hb = _rms(x_ref[...], g_ref[...]).astype(BF16)
    _rglru_tile(_dot(hb, w_ref[:, COLS_LRU[0]:COLS_LRU[1]]), cw_ref, cb_ref, wg_ref, bg_ref, lam_ref,
                nw_ref, ya_ref, prev_ref, h_ref)
    hg_ref[...] = _dot(hb, w_ref[:, COLS_HG[0]:COLS_HG[1]])
    qt_ref[...] = _dot_t(wt_ref[0:ATT_WIDTH, :], hb)
    k = _dot(hb, w_ref[:, COLS_K[0]:COLS_K[1]])
    k_ref[...] = k.astype(BF16)
    for i in range(ROW_TILE // MOBA_BLOCK):
        km_ref[i] = jnp.mean(k[i * MOBA_BLOCK:(i + 1) * MOBA_BLOCK], axis=0, keepdims=True)
    vt = _dot_t(wt_ref[ATT_WIDTH:2 * ATT_WIDTH, :], hb).astype(BF16)
    ones = jnp.ones((ONES_ROWS, ROW_TILE), BF16)
    for h in range(ATT_HEADS):
        vt_ref[h * V_ROWS:h * V_ROWS + ATT_HD, :] = vt[h * ATT_HD:(h + 1) * ATT_HD, :]
        vt_ref[h * V_ROWS + ATT_HD:(h + 1) * V_ROWS, :] = ones


def _in_proj(x2, g, w, wt, layer, seq, lru_params):
    t, d = x2.shape
    nblk = t // MOBA_BLOCK
    stacked = lambda a: pl.BlockSpec((None,) + a.shape[1:], lambda i: (layer, 0, 0))
    vec = lambda a: pl.BlockSpec(a.shape, lambda i: (0, 0))
    row = lambda width: pl.BlockSpec((ROW_TILE, width), lambda i: (i, 0))
    col = lambda height: pl.BlockSpec((height, ROW_TILE), lambda i: (0, i))
    assert seq % ROW_TILE == 0 and ROW_TILE % LRU_SUB == 0 and ROW_TILE % MOBA_BLOCK == 0
    return pl.pallas_call(
        functools.partial(_in_proj_kernel, seq),
        grid=(t // ROW_TILE,),
        in_specs=[row(d), vec(g), stacked(w), stacked(wt)] + [vec(a) for a in lru_params],
        out_specs=[row(LRU_WIDTH), row(4 * HG_WIDTH), col(ATT_WIDTH), row(ATT_WIDTH),
                   col(ATT_HEADS * V_ROWS),
                   pl.BlockSpec((ROW_TILE // MOBA_BLOCK, 1, ATT_WIDTH), lambda i: (i, 0, 0))],
        out_shape=[jax.ShapeDtypeStruct((t, LRU_WIDTH), BF16),
                   jax.ShapeDtypeStruct((t, 4 * HG_WIDTH), F32),
                   jax.ShapeDtypeStruct((ATT_WIDTH, t), F32),
                   jax.ShapeDtypeStruct((t, ATT_WIDTH), BF16),
                   jax.ShapeDtypeStruct((ATT_HEADS * V_ROWS, t), BF16),
                   jax.ShapeDtypeStruct((nblk, 1, ATT_WIDTH), F32)],
        scratch_shapes=[pltpu.VMEM((LRU_SUB, LRU_WIDTH), F32), pltpu.VMEM((1, LRU_WIDTH), F32)],
        compiler_params=_params("arbitrary"),
        name="in_proj",
    )(x2, g, w, wt, *lru_params)


def _rglru_tile(xy, cw_ref, cb_ref, wg_ref, bg_ref, lam_ref, nw_ref, o_ref, prev_ref, h_ref):
    w = LRU_WIDTH
    row = lax.broadcasted_iota(jnp.int32, (LRU_SUB, w), 0)
    sp = _softplus(-lam_ref[...])
    for c in range(ROW_TILE // LRU_SUB):
        rows = slice(c * LRU_SUB, (c + 1) * LRU_SUB)
        xb = xy[rows, 0:w]
        yb = xy[rows, w:2 * w]
        prev = prev_ref[...]
        xc = xb * cw_ref[LRU_CONV - 1:LRU_CONV, :] + cb_ref[...]
        for k in range(1, LRU_CONV):
            shifted = jnp.where(row >= k, pltpu.roll(xb, k, 0), pltpu.roll(prev, k, 0))
            xc = xc + shifted * cw_ref[LRU_CONV - 1 - k:LRU_CONV - k, :]
        prev_ref[...] = xb
        gates = _dot(xc.astype(BF16), wg_ref[...]) + bg_ref[...]
        r = _sigmoid(gates[:, 0:w])
        i = _sigmoid(gates[:, w:2 * w])
        log_a = -LRU_C * r * sp
        a = jnp.exp(log_a)
        b = jnp.sqrt(jnp.maximum(1.0 - a * a, 0.0)) * (i * xc)
        s = 1
        while s < LRU_SUB:
            keep = row >= s
            a_s = jnp.where(keep, pltpu.roll(a, s, 0), 1.0)
            b_s = jnp.where(keep, pltpu.roll(b, s, 0), 0.0)
            b = a * b_s + b
            a = a * a_s
            s *= 2
        h = b + a * h_ref[...]
        h_ref[...] = h[LRU_SUB - 1:LRU_SUB, :]
        y = h * jax.nn.gelu(yb, approximate=True)
        o_ref[rows, :] = _rms(y, nw_ref[...]).astype(o_ref.dtype)


def _hgrn2_kernel(layer, z_ref, lbp_ref, nw_ref, o_ref, state_ref):
    @pl.when(pl.program_id(1) == 0)
    def _():
        state_ref[...] = jnp.zeros_like(state_ref)

    n = HG_WIDTH
    L = HG_CHUNK
    lbp = lbp_ref[...]
    e = jnp.exp(lbp - jnp.max(lbp, axis=0, keepdims=True))
    soft = e / jnp.sum(e, axis=0, keepdims=True)
    lb = jnp.sum(soft[0:layer + 1], axis=0, keepdims=True) - soft[0:1]
    log_lb = jnp.log(jnp.maximum(lb, TINY))
    log_1m = jnp.log1p(-lb)

    lane_h = lax.broadcasted_iota(jnp.int32, (1, n), 1) // HG_DK
    head_of_row = lax.broadcasted_iota(jnp.int32, (n, n), 0) // HG_DK
    head_of_col = lax.broadcasted_iota(jnp.int32, (n, n), 1) // HG_DK
    same_head = head_of_row == head_of_col
    ones_bd = jnp.where(same_head, 1.0, 0.0).astype(BF16)
    row = lax.broadcasted_iota(jnp.int32, (L, n), 0)
    sub = lax.broadcasted_iota(jnp.int32, (1, SUBLANES, 1), 1)
    t_idx = lax.broadcasted_iota(jnp.int32, (L, L), 0)
    s_idx = lax.broadcasted_iota(jnp.int32, (L, L), 1)

    def chunk(ci, carry):
        r0 = pl.multiple_of(ci * L, L)
        q = z_ref[pl.ds(r0, L), 0:n]
        f = z_ref[pl.ds(r0, L), n:2 * n]
        v = z_ref[pl.ds(r0, L), 2 * n:3 * n]
        g = z_ref[pl.ds(r0, L), 3 * n:4 * n]
        lsig = -_softplus(-f)
        t1 = log_1m + lsig
        log_f = jnp.maximum(log_lb, t1) + jnp.log(1.0 + jnp.exp(-jnp.abs(log_lb - t1)))
        kk = (1.0 - lb) * _sigmoid(-f)
        qs = q * _sigmoid(q)
        b = log_f * LOG2E
        s = 1
        while s < L:
            b = b + jnp.where(row >= s, pltpu.roll(b, s, 0), 0.0)
            s *= 2

        parts = []
        for r in range(0, L, L // 2):
            g8 = L // 2 // SUBLANES
            q3 = qs[r:r + L // 2].reshape(g8, SUBLANES, n)
            k3 = kk[r:r + L // 2].reshape(g8, SUBLANES, n)
            b3 = b[r:r + L // 2].reshape(g8, SUBLANES, n)
            v3 = v[r:r + L // 2].reshape(g8, SUBLANES, n)
            o_half = jnp.zeros((L // 2, n), F32)
            for j in range(SUBLANES):
                d = jnp.where(sub >= j, b3 - b3[:, j:j + 1, :], NEG)
                xj = q3 * k3[:, j:j + 1, :] * jnp.exp2(d)
                aj = _dot(xj.reshape(L // 2, n).astype(BF16), ones_bd)
                o_half = o_half + (aj.reshape(g8, SUBLANES, n) * v3[:, j:j + 1, :]).reshape(L // 2, n)
            parts.append(o_half)
        o = jnp.concatenate(parts, axis=0)

        att = [jnp.zeros((L, L), F32) for _ in range(HG_HEADS)]
        c = SUBLANES
        while c < L:
            bc = b.reshape(L // c, c, n)
            bend = jnp.broadcast_to(bc[:, c - 1:c, :], (L // c, c, n)).reshape(L, n)
            bprev = jnp.where(row >= c, pltpu.roll(bend, c, 0), 0.0)
            qt = qs * jnp.exp2(b - bprev)
            kt = (kk * jnp.exp2(bend - b)).astype(BF16)
            pair = ((t_idx // c) % 2 == 1) & ((s_idx // c) == (t_idx // c) - 1)
            for h in range(HG_HEADS):
                sc = _dot_t(jnp.where(lane_h == h, qt, 0.0).astype(BF16), kt)
                att[h] = att[h] + jnp.where(pair, sc, 0.0)
            c *= 2
        att_cat = jnp.concatenate([a.astype(BF16) for a in att], axis=1)
        v_stack = jnp.concatenate([jnp.where(lane_h == h, v, 0.0).astype(BF16)
                                   for h in range(HG_HEADS)], axis=0)
        o = o + _dot(att_cat, v_stack)

        state = state_ref[...]
        o = o + _dot_t((qs * jnp.exp2(b)).astype(BF16), state.astype(BF16))
        b_last = b[L - 1:L, :]
        kdec = (kk * jnp.exp2(b_last - b)).astype(BF16)
        upd = lax.dot_general(v.astype(BF16), kdec, (((0,), (0,)), ((), ())),
                              preferred_element_type=F32)
        state_ref[...] = state * jnp.exp2(b_last) + jnp.where(same_head, upd, 0.0)

        ms = _dot((o * o).astype(BF16), ones_bd) * (1.0 / HG_DK)
        y = o * lax.rsqrt(ms + EPS) * nw_ref[...] * (g * _sigmoid(g))
        o_ref[pl.ds(r0, L), :] = y.astype(o_ref.dtype)
        return carry

    for ci in range(HG_TILE // L):
        chunk(ci, 0)


def _hgrn2(z, bsz, seq, lbp, nw, layer):
    nt = seq // HG_TILE
    n = HG_WIDTH
    assert seq % HG_TILE == 0 and HG_TILE % HG_CHUNK == 0
    return pl.pallas_call(
        functools.partial(_hgrn2_kernel, layer),
        grid=(bsz, nt),
        in_specs=[pl.BlockSpec((HG_TILE, 4 * n), lambda b, t: (b * nt + t, 0)),
                  pl.BlockSpec(lbp.shape, lambda b, t: (0, 0)),
                  pl.BlockSpec(nw.shape, lambda b, t: (0, 0))],
        out_specs=pl.BlockSpec((HG_TILE, n), lambda b, t: (b * nt + t, 0)),
        out_shape=jax.ShapeDtypeStruct((bsz * seq, n), BF16),
        scratch_shapes=[pltpu.VMEM((n, n), F32)],
        compiler_params=_params("parallel", "arbitrary"),
        name="hgrn2",
    )(z, lbp, nw)


def _moba_kernel(nblk, qt_ref, k_ref, vt_ref, km_ref, kx_ref, o_ref, qa_ref, sa_ref, sb_ref,
                 ta_ref, tb_ref, m_ref, acc_ref):
    j = pl.program_id(2)
    hp = pl.program_id(1)
    blk = MOBA_BLOCK
    hd = ATT_HD
    pair = 2 * hd
    span = MOBA_GROUP * blk
    scale = 1.0 / math.sqrt(hd)
    feat = lax.broadcasted_iota(jnp.int32, (pair, 1), 0)
    n_t = lax.broadcasted_iota(jnp.int32, (nblk, blk), 0)
    b_row = lax.broadcasted_iota(jnp.int32, (SUBLANES, blk), 0)
    causal = (lax.broadcasted_iota(jnp.int32, (blk, blk), 0)
              <= lax.broadcasted_iota(jnp.int32, (blk, blk), 1))
    qt = qt_ref[...]
    km = km_ref[...]

    def keys(start, size):
        return jnp.concatenate([k_ref[pl.ds(start, size), :], kx_ref[pl.ds(start, size), :]], axis=1)

    pad = jnp.zeros((pair - nblk - SUBLANES, blk), F32)
    qs, bias, gate = [], [], []
    for h in range(2):
        rows = jnp.zeros((SUBLANES, blk), F32)
        for i in range(PIECES):
            piece = jnp.float32(SLOPE_PIECES[h][i])
            for head in range(h + 2, ATT_HEADS, 2):
                piece = jnp.where(2 * hp + h == head, SLOPE_PIECES[head][i], piece)
            rows = jnp.where(b_row == i, piece, rows)
            rows = jnp.where(b_row == PIECES + i, piece * blk, rows)
        bias.append(rows)
        qm = jnp.where(feat // hd == h, qt, 0.0)
        qs.append((qm * (scale * LOG2E)).astype(BF16))
        gate.append(jnp.dot(km, qm, precision=HIGHEST, preferred_element_type=F32))

    j0 = pl.multiple_of(j * blk, blk)
    k_own = keys(j0, blk)
    for h in range(2):
        own = jnp.concatenate([jnp.where(n_t == j, 0.0, NEG), bias[h], pad], axis=0)
        s = _dot(k_own, jnp.concatenate([qs[h], own.astype(BF16)], axis=0))
        sb_ref[h, 0:blk, :] = jnp.where(causal, s, NEG)

    def score(g, buf, heads=(0, 1)):
        s_ref, top_ref = buf
        k_g = keys(pl.multiple_of(g * span, span), span)
        for h in heads:
            s = _dot(k_g, qa_ref[h])
            s_ref[h] = s
            top_ref[h] = jnp.max(s, axis=0, keepdims=True)

    def attend(g, buf):
        s_ref, top_ref = buf
        vt_g = vt_ref[:, pl.ds(pl.multiple_of(g * span, span), span)]
        for h in range(2):
            m_old = m_ref[h]
            m_new = jnp.maximum(m_old, top_ref[h])
            p = jnp.exp2((s_ref[h] - m_new).astype(BF16))
            m_ref[h] = m_new
            pv = _dot(vt_g[h * V_ROWS:(h + 1) * V_ROWS, :], p)
            acc_ref[h] = jnp.exp2(m_old - m_new) * acc_ref[h] + pv

    last = nblk // MOBA_GROUP - 1
    buf_a = (sa_ref, ta_ref)
    buf_b = (sb_ref, tb_ref)

    for h in range(2):
        g = jnp.where(n_t < j, gate[h], NEG)
        chosen = jnp.zeros((nblk, blk), F32)
        for r in range(MOBA_TOPK):
            top = jnp.max(g, axis=0, keepdims=True)
            first = jnp.min(jnp.where(g == top, n_t, nblk), axis=0, keepdims=True)
            pick = n_t == first
            chosen = jnp.where(pick, jnp.where(j > r, 1.0, 0.0), chosen)
            g = jnp.where(pick, -jnp.inf, g)
        past = jnp.concatenate([jnp.where(chosen > 0.0, 0.0, NEG), bias[h], pad], axis=0)
        qa_ref[h] = jnp.concatenate([qs[h], past.astype(BF16)], axis=0)
        score(0, buf_a, heads=(h,))

    vt_own = vt_ref[:, pl.ds(j0, blk)]
    for h in range(2):
        s = sb_ref[h, 0:blk, :]
        m = jnp.max(s, axis=0, keepdims=True)
        p = jnp.exp2((s - m).astype(BF16))
        m_ref[h] = m
        acc_ref[h] = _dot(vt_own[h * V_ROWS:(h + 1) * V_ROWS, :], p)

    def chain(g0, count, more):
        for t in range(count):
            cur, nxt = (buf_a, buf_b) if t % 2 == 0 else (buf_b, buf_a)
            if t + 1 < count or more:
                score(jnp.minimum(g0 + t + 1, last), nxt)
            attend(g0 + t, cur)

    groups = (j + MOBA_GROUP - 1) // MOBA_GROUP
    nlong = groups // MOBA_LONG
    left = groups % MOBA_LONG

    def body(i, carry):
        chain(i * MOBA_LONG, MOBA_LONG, True)
        return carry

    lax.fori_loop(0, nlong, body, 0)
    g0 = nlong * MOBA_LONG
    size = MOBA_LONG // 2
    while size >= 1:
        take = (left & size) != 0

        @pl.when(take)
        def _(g0=g0, size=size):
            chain(g0, size, size > 1)

        g0 = g0 + jnp.where(take, size, 0)
        size //= 2

    out = []
    for h in range(2):
        acc = acc_ref[h]
        out.append(acc[0:hd] / acc[hd:hd + 1])
    o_ref[...] = jnp.concatenate(out, axis=0)


def _moba(qt, k, vt, km, bsz, seq):
    nblk = seq // MOBA_BLOCK
    blk = MOBA_BLOCK
    pair = 2 * ATT_HD
    assert nblk + SUBLANES <= pair and 2 * PIECES <= SUBLANES
    assert MOBA_LONG & (MOBA_LONG - 1) == 0 and MOBA_LONG >= 2
    assert nblk % MOBA_GROUP == 0
    pos = jnp.arange(seq, dtype=jnp.int32)[:, None]
    col = jnp.arange(pair, dtype=jnp.int32)[None, :]
    extras = [pos % blk] * PIECES + [pos // blk] * PIECES
    kx = (pos // blk == col).astype(F32)
    for i, e in enumerate(extras):
        kx = jnp.where(col == nblk + i, e.astype(F32), kx)
    kx = kx.astype(BF16)
    return pl.pallas_call(
        functools.partial(_moba_kernel, nblk),
        grid=(bsz, ATT_HEADS // 2, nblk),
        in_specs=[pl.BlockSpec((pair, blk), lambda b, p, j: (p, b * nblk + j)),
                  pl.BlockSpec((seq, pair), lambda b, p, j: (b, p)),
                  pl.BlockSpec((2 * V_ROWS, seq), lambda b, p, j: (p, b)),
                  pl.BlockSpec((nblk, pair), lambda b, p, j: (b, p)),
                  pl.BlockSpec((seq, pair), lambda b, p, j: (0, 0))],
        out_specs=pl.BlockSpec((pair, blk), lambda b, p, j: (p, b * nblk + j)),
        out_shape=jax.ShapeDtypeStruct((ATT_WIDTH, bsz * seq), F32),
        scratch_shapes=[pltpu.VMEM((2, 2 * pair, blk), BF16),
                        pltpu.VMEM((2, MOBA_GROUP * blk, blk), F32),
                        pltpu.VMEM((2, MOBA_GROUP * blk, blk), F32),
                        pltpu.VMEM((2, 1, blk), F32),
                        pltpu.VMEM((2, 1, blk), F32),
                        pltpu.VMEM((2, 1, blk), F32),
                        pltpu.VMEM((2, V_ROWS, blk), F32)],
        compiler_params=_params("parallel", "parallel", "arbitrary"),
        name="moba",
    )(qt, k, vt, km, kx)


def _out_mlp_kernel(final, x_ref, ya_ref, yb_ref, yct_ref, an_ref, wo_ref, gm_ref, w1_ref, w2_ref,
                    gf_ref, o_ref):
    yc = _rms(yct_ref[...].T, an_ref[...]).astype(BF16)
    y = jnp.concatenate([ya_ref[...], yb_ref[...], yc], axis=1)
    x = x_ref[...] + _dot(y, wo_ref[...])
    hb = _rms(x, gm_ref[...]).astype(BF16)
    mlp = None
    for c in range(w1_ref.shape[1] // FF_CHUNK):
        cols = slice(c * FF_CHUNK, (c + 1) * FF_CHUNK)
        u = jnp.square(jnp.maximum(_dot(hb, w1_ref[:, cols]), 0.0))
        part = _dot(u.astype(BF16), w2_ref[cols, :])
        mlp = part if mlp is None else mlp + part
    x = x + mlp
    if final:
        x = _rms(x, gf_ref[...])
    o_ref[...] = x


def _out_mlp(x2, ya, yb, yct, an, wo, gm, w1, w2, gf, layer, final):
    t, d = x2.shape
    stacked = lambda a: pl.BlockSpec((None,) + a.shape[1:], lambda i: (layer, 0, 0))
    row = lambda a: pl.BlockSpec((MLP_TILE, a.shape[1]), lambda i: (i, 0))
    full = lambda a: pl.BlockSpec(a.shape, lambda i: (0, 0))
    assert t % MLP_TILE == 0 and w1.shape[2] % FF_CHUNK == 0
    return pl.pallas_call(
        functools.partial(_out_mlp_kernel, final),
        grid=(t // MLP_TILE,),
        in_specs=[row(x2), row(ya), row(yb), pl.BlockSpec((yct.shape[0], MLP_TILE), lambda i: (0, i)),
                  full(an), stacked(wo), full(gm), stacked(w1), stacked(w2), full(gf)],
        out_specs=pl.BlockSpec((MLP_TILE, d), lambda i: (i, 0)),
        out_shape=jax.ShapeDtypeStruct((t, d), F32),
        compiler_params=_params("parallel"),
        name="out_mlp",
    )(x2, ya, yb, yct, an, wo, gm, w1, w2, gf)


def _block_diag(w):
    g, n, _ = w.shape
    eye = jnp.eye(g, dtype=w.dtype)
    return (eye[:, None, :, None] * w[:, :, None, :]).reshape(g * n, g * n)


def kernel(x, w_in, w_out, norm_mix, norm_mlp, lru_conv_w, lru_conv_b, lru_wa, lru_ba, lru_wx, lru_bx,
           lru_lambda, hg_lower_bounds, hg_norm_w, lru_out_norm, att_out_norm, w_ff1, w_ff2, norm_final):
    bsz, seq, d = x.shape
    depth = w_in.shape[0]
    x2 = x.reshape(bsz * seq, d)
    r2 = lambda a: a.reshape(1, -1)
    w_in_b = w_in.astype(BF16)
    w_qv = jnp.concatenate([w_in[:, :, COLS_Q[0]:COLS_Q[1]], w_in[:, :, COLS_V[0]:COLS_V[1]]], axis=2)
    wt_b = jnp.swapaxes(lax.optimization_barrier(w_qv), 1, 2).astype(BF16)
    w_out_b, w_ff1_b, w_ff2_b = w_out.astype(BF16), w_ff1.astype(BF16), w_ff2.astype(BF16)
    for l in range(depth):
        wg = jnp.concatenate([_block_diag(lru_wa[l]), _block_diag(lru_wx[l])], axis=1).astype(BF16)
        bg = jnp.concatenate([lru_ba[l], lru_bx[l]]).reshape(1, -1)
        lru_params = (lru_conv_w[l], r2(lru_conv_b[l]), wg, bg, r2(lru_lambda[l]), r2(lru_out_norm[l]))
        ya, hg, cqt, ck, cvt, km = _in_proj(x2, r2(norm_mix[l]), w_in_b, wt_b, l, seq, lru_params)
        yb = _hgrn2(hg, bsz, seq, hg_lower_bounds, r2(jnp.tile(hg_norm_w[l], HG_HEADS)), l)
        yc = _moba(cqt, ck, cvt, km.reshape(-1, km.shape[-1]), bsz, seq)
        x2 = _out_mlp(x2, ya, yb, yc, r2(att_out_norm[l]), w_out_b, r2(norm_mlp[l]), w_ff1_b, w_ff2_b,
                      r2(norm_final), l, l == depth - 1)
    return x2.reshape(bsz, seq, d)
```
